```python
import math
import jax
import jax.numpy as jnp
from jax import lax
import numpy as np


D_MODEL = 1024
BATCH = 8
SEQ = 4096
DEPTH = 2

MEM_LEN = 256
EPS = 1e-6
N_BRANCH = 4
BRANCH_W = 256
CONV_W = 3
GLA_HEADS = 4
GLA_DK = 64
GLA_DV = 64
GLA_RANK = 16
GLA_TAU = 16.0
GLA_CHUNK = 64
S5_GROUP = 16
S5_GROUPS = BRANCH_W // S5_GROUP
S5_STATE = 64
S5_DT_MIN = 1e-3
S5_DT_MAX = 1e-1
DSA_HEADS = 4
DSA_HEAD_DIM = 64
IDX_HEADS = 8
IDX_DIM = 32
DSA_TOPK = 256
Q_BLOCK = 128
ROPE_THETA = 500000.0
ROPE_FRAC = 4
X_HEADS = 4
X_HEAD_DIM = 128
N_GROUPS = 4
EXPERTS_PER_GROUP = 4
N_EXPERTS = N_GROUPS * EXPERTS_PER_GROUP
TOPK_IN_GROUP = 2
D_FF_EXPERT = 512

IN_SPLITS = (
    BRANCH_W, BRANCH_W, BRANCH_W,
    GLA_HEADS * GLA_DK, GLA_HEADS * GLA_DK, GLA_HEADS * GLA_DV,
    GLA_HEADS * GLA_DV, GLA_RANK,
    BRANCH_W,
    DSA_HEADS * DSA_HEAD_DIM, DSA_HEAD_DIM, DSA_HEAD_DIM,
    IDX_HEADS * IDX_DIM, IDX_DIM, IDX_HEADS,
    N_BRANCH * D_MODEL,
)
D_IN = sum(IN_SPLITS)
IN_OFFSETS = tuple(sum(IN_SPLITS[:i + 1]) for i in range(len(IN_SPLITS) - 1))

kernel_name = 'hybrid_gated_conv_gla_s5_dsa_hmoe'


def rmsnorm(x, g):
    xf = x.astype(jnp.float32)
    y = xf * lax.rsqrt(jnp.mean(xf * xf, axis=-1, keepdims=True) + EPS)
    return (y * g.astype(jnp.float32)).astype(x.dtype)


def rope_partial(t, positions):
    dh = t.shape[-1]
    rd = dh // ROPE_FRAC
    half = rd // 2
    inv_freq = ROPE_THETA ** (-jnp.arange(half, dtype=jnp.float32) * (2.0 / rd))
    ang = positions.astype(jnp.float32)[:, :, None] * inv_freq
    cos = jnp.cos(ang)[:, :, None, :]
    sin = jnp.sin(ang)[:, :, None, :]
    t1 = t[..., :half].astype(jnp.float32)
    t2 = t[..., half:rd].astype(jnp.float32)
    rot = jnp.concatenate([t1 * cos - t2 * sin, t2 * cos + t1 * sin], axis=-1).astype(t.dtype)
    return jnp.concatenate([rot, t[..., rd:]], axis=-1)


def short_conv_mixer(h_in, b_gate, c_gate, conv_w, conv_b):
    u = c_gate * h_in
    y = lax.conv_general_dilated(
        u, conv_w[:, None, :], window_strides=(1,), padding=[(CONV_W - 1, 0)],
        dimension_numbers=('NWC', 'WIO', 'NWC'), feature_group_count=u.shape[-1])
    return b_gate * (y + conv_b)


def gla_mixer(q, k, v, r, a_low, a_up, a_b, g_norm):
    B, S, _ = q.shape
    H, dk, dv, C = GLA_HEADS, GLA_DK, GLA_DV, GLA_CHUNK
    N = S // C
    f32 = jnp.float32
    log_a = jax.nn.log_sigmoid((a_low @ a_up).astype(f32) + a_b.astype(f32)) / GLA_TAU
    shp = lambda t, d: t.astype(f32).reshape(B, N, C, H, d)
    qc = shp(q, dk) * (dk ** -0.5)
    kc = shp(k, dk)
    vc = shp(v, dv)
    bcum = jnp.cumsum(shp(log_a, dk), axis=2)
    blast = bcum[:, :, -1:]
    q_dec = qc * jnp.exp(bcum)
    k_inv = kc * jnp.exp(-bcum)
    k_end = kc * jnp.exp(blast - bcum)
    causal = jnp.tril(jnp.ones((C, C), dtype=bool))
    att = jnp.einsum('bnihd,bnjhd->bnhij', q_dec, k_inv)
    att = jnp.where(causal, att, 0.0)
    o_intra = jnp.einsum('bnhij,bnjhe->bnihe', att, vc)

    def step(state, inp):
        qd, ke, vv, dec = inp
        o = jnp.einsum('bihd,bhde->bihe', qd, state)
        state = state * dec[..., None] + jnp.einsum('bjhd,bjhe->bhde', ke, vv)
        return state, o

    xs = (jnp.moveaxis(q_dec, 1, 0), jnp.moveaxis(k_end, 1, 0), jnp.moveaxis(vc, 1, 0),
          jnp.moveaxis(jnp.exp(blast[:, :, 0]), 1, 0))
    state0 = jnp.zeros((B, H, dk, dv), f32)
    _, o_inter = lax.scan(step, state0, xs)
    o = (o_intra + jnp.moveaxis(o_inter, 0, 1)).reshape(B, S, H, dv)
    o = o * lax.rsqrt(jnp.mean(o * o, axis=-1, keepdims=True) + EPS)
    o = o.reshape(B, S, H * dv) * g_norm.astype(f32) * jax.nn.silu(r.astype(f32))
    return o.astype(q.dtype)


def s5_mixer(u, lam_re, lam_im, log_dt, b_re, b_im, c_re, c_im, d_skip, w_glu, b_glu):
    B, S, _ = u.shape
    G, P = S5_GROUPS, S5_STATE
    f32 = jnp.float32
    uf = u.astype(f32).reshape(B, S, G, S5_GROUP)
    lam_re = lam_re.astype(f32)
    lam_im = lam_im.astype(f32)
    dt = jnp.exp(log_dt.astype(f32))[:, None]
    mag = jnp.exp(lam_re * dt)
    lb_re = mag * jnp.cos(lam_im * dt)
    lb_im = mag * jnp.sin(lam_im * dt)
    den = lam_re * lam_re + lam_im * lam_im
    f_re = ((lb_re - 1.0) * lam_re + lb_im * lam_im) / den
    f_im = (lb_im * lam_re - (lb_re - 1.0) * lam_im) / den
    b_re = b_re.astype(f32)
    b_im = b_im.astype(f32)
    bb_re = f_re[..., None] * b_re - f_im[..., None] * b_im
    bb_im = f_re[..., None] * b_im + f_im[..., None] * b_re
    bu_re = jnp.einsum('bsgc,gpc->bsgp', uf, bb_re)
    bu_im = jnp.einsum('bsgc,gpc->bsgp', uf, bb_im)
    a_re = jnp.broadcast_to(lb_re, bu_re.shape)
    a_im = jnp.broadcast_to(lb_im, bu_im.shape)

    def combine(e1, e2):
        a1r, a1i, b1r, b1i = e1
        a2r, a2i, b2r, b2i = e2
        return (a2r * a1r - a2i * a1i, a2r * a1i + a2i * a1r,
                a2r * b1r - a2i * b1i + b2r, a2r * b1i + a2i * b1r + b2i)

    _, _, x_re, x_im = lax.associative_scan(combine, (a_re, a_im, bu_re, bu_im), axis=1)
    y = (jnp.einsum('bsgp,gcp->bsgc', x_re, c_re.astype(f32))
         - jnp.einsum('bsgp,gcp->bsgc', x_im, c_im.astype(f32)))
    y = y.reshape(B, S, G * S5_GROUP) + d_skip.astype(f32) * uf.reshape(B, S, G * S5_GROUP)
    y = jax.nn.gelu(y)
    y = y * jax.nn.sigmoid(y @ w_glu.astype(f32) + b_glu.astype(f32))
    return y.astype(u.dtype)


def dsa_mixer(q, k, v, qi, ki, wi, positions):
    B, S, _ = q.shape
    f32 = jnp.float32
    q = rope_partial(q.reshape(B, S, DSA_HEADS, DSA_HEAD_DIM), positions)
    k = rope_partial(k[:, :, None, :], positions)[:, :, 0]
    qi = rope_partial(qi.reshape(B, S, IDX_HEADS, IDX_DIM), positions)
    ki = rope_partial(ki[:, :, None, :], positions)[:, :, 0]
    wi = wi * ((IDX_HEADS ** -0.5) * (IDX_DIM ** -0.5))
    topk = min(DSA_TOPK, S // 4)
    nb = S // Q_BLOCK
    ki32 = ki.astype(f32)

    def blockify(t):
        return jnp.moveaxis(t.reshape((B, nb, Q_BLOCK) + t.shape[2:]), 1, 0)

    def one_block(args):
        qb, qib, wib, start = args
        t_idx = start + jnp.arange(Q_BLOCK)
        iscore = jnp.einsum('bqhd,bsd->bqhs', qib.astype(f32), ki32)
        iscore = jnp.einsum('bqhs,bqh->bqs', jax.nn.relu(iscore), wib.astype(f32))
        causal = jnp.arange(S)[None, :] <= t_idx[:, None]
        iscore = jnp.where(causal[None], iscore, -jnp.inf)
        _, sel = lax.top_k(iscore, topk)
        valid = sel <= t_idx[None, :, None]
        k_sel = jax.vmap(lambda kk, ii: kk[ii])(k, sel)
        v_sel = jax.vmap(lambda vv, ii: vv[ii])(v, sel)
        logits = jnp.einsum('bqhd,bqkd->bqhk', qb.astype(f32), k_sel.astype(f32)) * (DSA_HEAD_DIM ** -0.5)
        logits = jnp.where(valid[:, :, None, :], logits, -jnp.inf)
        p = jax.nn.softmax(logits, axis=-1)
        o = jnp.einsum('bqhk,bqkd->bqhd', p, v_sel.astype(f32))
        return o.astype(qb.dtype)

    starts = jnp.arange(nb, dtype=jnp.int32) * Q_BLOCK
    out = lax.map(one_block, (blockify(q), blockify(qi), blockify(wi), starts))
    return jnp.moveaxis(out, 0, 1).reshape(B, S, DSA_HEADS * DSA_HEAD_DIM)


def mem_cross_attn(h, memn, w_q, w_kv, w_o):
    B, S, _ = h.shape
    M = memn.shape[1]
    q = (h @ w_q).reshape(B, S, X_HEADS, X_HEAD_DIM)
    k, v = jnp.split(memn @ w_kv, 2, axis=-1)
    k = k.reshape(B, M, X_HEADS, X_HEAD_DIM)
    v = v.reshape(B, M, X_HEADS, X_HEAD_DIM)
    logits = jnp.einsum('bshd,bmhd->bhsm', q, k).astype(jnp.float32) * (X_HEAD_DIM ** -0.5)
    p = jax.nn.softmax(logits, axis=-1).astype(v.dtype)
    o = jnp.einsum('bhsm,bmhd->bshd', p, v).reshape(B, S, X_HEADS * X_HEAD_DIM)
    return o @ w_o


def hier_moe(h, w_rg, b_rg, w_re, b_re, w_gate, w_up, w_down):
    B, S, _ = h.shape
    f32 = jnp.float32
    g_prob = jax.nn.softmax((h @ w_rg).astype(f32) + b_rg.astype(f32), axis=-1)
    p_g, g_sel = lax.top_k(g_prob, 1)
    g_onehot = jax.nn.one_hot(g_sel[..., 0], N_GROUPS, dtype=f32)
    e_logits = ((h @ w_re).astype(f32) + b_re.astype(f32)).reshape(B, S, N_GROUPS, EXPERTS_PER_GROUP)
    e_prob = jax.nn.softmax(jnp.einsum('bsge,bsg->bse', e_logits, g_onehot), axis=-1)
    top_p, top_i = lax.top_k(e_prob, TOPK_IN_GROUP)
    top_p = top_p / jnp.sum(top_p, axis=-1, keepdims=True)
    expert_id = g_sel * EXPERTS_PER_GROUP + top_i
    combine = jnp.einsum('bske,bsk->bse', jax.nn.one_hot(expert_id, N_EXPERTS, dtype=f32),
                         p_g * top_p).astype(h.dtype)
    out = jnp.zeros_like(h)
    for e in range(N_EXPERTS):
        he = jax.nn.silu(h @ w_gate[e]) * (h @ w_up[e])
        out = out + combine[..., e:e + 1] * (he @ w_down[e])
    return out


def setup_inputs(seed: int = 0) -> dict:
    key = jax.random.key(seed)
    ks = iter(jax.random.split(key, 48))
    f32 = jnp.float32
    L, D, W = DEPTH, D_MODEL, BRANCH_W
    G, P = S5_GROUPS, S5_STATE
    nrm = lambda shape, scale: jax.random.normal(next(ks), shape, f32) * scale
    gain = lambda shape: 1.0 + nrm(shape, 0.02)
    x = nrm((BATCH, SEQ, D), 1.0)
    mem = nrm((BATCH, MEM_LEN, D), 1.0)
    positions = (jax.random.randint(next(ks), (BATCH, 1), 0, 1024)
                 + jnp.arange(SEQ)[None, :]).astype(jnp.int32)
    n_idx = jnp.arange(P, dtype=f32)
    return {
        'x': x,
        'mem': mem,
        'positions': positions,
        'norm_mix': gain((L, D)),
        'w_in': nrm((L, D, D_IN), D ** -0.5),
        'conv_w': nrm((L, CONV_W, W), CONV_W ** -0.5),
        'conv_b': nrm((L, W), 0.02),
        'gla_a_up': nrm((L, GLA_RANK, GLA_HEADS * GLA_DK), GLA_RANK ** -0.5),
        'gla_a_b': nrm((L, GLA_HEADS * GLA_DK), 0.1),
        'gla_norm': gain((L, GLA_HEADS * GLA_DV)),
        's5_lambda_re': -0.5 + nrm((L, G, P), 0.01),
        's5_lambda_im': math.pi * n_idx + nrm((L, G, P), 0.01),
        's5_log_dt': jax.random.uniform(next(ks), (L, G), f32, math.log(S5_DT_MIN), math.log(S5_DT_MAX)),
        's5_b_re': nrm((L, G, P, S5_GROUP), (2.0 * S5_GROUP) ** -0.5),
        's5_b_im': nrm((L, G, P, S5_GROUP), (2.0 * S5_GROUP) ** -0.5),
        's5_c_re': nrm((L, G, S5_GROUP, P), (2.0 * P) ** -0.5),
        's5_c_im': nrm((L, G, S5_GROUP, P), (2.0 * P) ** -0.5),
        's5_d': nrm((L, W), 1.0),
        's5_w_glu': nrm((L, W, W), W ** -0.5),
        's5_b_glu': nrm((L, W), 0.02),
        'w_branch': nrm((L, N_BRANCH, W, D), W ** -0.5),
        'w_out': nrm((L, D, D), D ** -0.5),
        'norm_cross': gain((L, D)),
        'w_cq': nrm((L, D, X_HEADS * X_HEAD_DIM), D ** -0.5),
        'w_ckv': nrm((L, D, 2 * X_HEADS * X_HEAD_DIM), D ** -0.5),
        'w_co': nrm((L, X_HEADS * X_HEAD_DIM, D), (X_HEADS * X_HEAD_DIM) ** -0.5),
        'norm_ffn': gain((L, D)),
        'w_route_group': nrm((L, D, N_GROUPS), D ** -0.5),
        'b_route_group': nrm((L, N_GROUPS), 0.01),
        'w_route_expert': nrm((L, D, N_EXPERTS), D ** -0.5),
        'b_route_expert': nrm((L, N_EXPERTS), 0.01),
        'w_e_gate': nrm((L, N_EXPERTS, D, D_FF_EXPERT), D ** -0.5),
        'w_e_up': nrm((L, N_EXPERTS, D, D_FF_EXPERT), D ** -0.5),
        'w_e_down': nrm((L, N_EXPERTS, D_FF_EXPERT, D), D_FF_EXPERT ** -0.5),
        'norm_mem': gain((D,)),
        'norm_final': gain((D,)),
    }


def reference(x, mem, positions, norm_mix, w_in, conv_w, conv_b, gla_a_up, gla_a_b, gla_norm,
              s5_lambda_re, s5_lambda_im, s5_log_dt, s5_b_re, s5_b_im, s5_c_re, s5_c_im, s5_d,
              s5_w_glu, s5_b_glu, w_branch, w_out, norm_cross, w_cq, w_ckv, w_co, norm_ffn,
              w_route_group, b_route_group, w_route_expert, b_route_expert, w_e_gate, w_e_up,
              w_e_down, norm_mem, norm_final):
    memn = rmsnorm(mem, norm_mem)
    for l in range(DEPTH):
        h = rmsnorm(x, norm_mix[l])
        (c_h, c_b, c_c, g_q, g_k, g_v, g_r, g_a, s_u,
         d_q, d_k, d_v, i_q, i_k, i_w, gates) = jnp.split(h @ w_in[l], IN_OFFSETS, axis=-1)
        y_a = short_conv_mixer(c_h, c_b, c_c, conv_w[l], conv_b[l])
        y_b = gla_mixer(g_q, g_k, g_v, g_r, g_a, gla_a_up[l], gla_a_b[l], gla_norm[l])
        y_c = s5_mixer(s_u, s5_lambda_re[l], s5_lambda_im[l], s5_log_dt[l], s5_b_re[l], s5_b_im[l],
                       s5_c_re[l], s5_c_im[l], s5_d[l], s5_w_glu[l], s5_b_glu[l])
        y_d = dsa_mixer(d_q, d_k, d_v, i_q, i_k, i_w, positions)
        gate_parts = jnp.split(gates, N_BRANCH, axis=-1)
        branches = (y_a, y_b, y_c, y_d)
        merged = jnp.zeros_like(x)
        for i in range(N_BRANCH):
            merged = merged + jax.nn.sigmoid(gate_parts[i]) * (branches[i] @ w_branch[l, i])
        x = x + merged @ w_out[l]
        x = x + mem_cross_attn(rmsnorm(x, norm_cross[l]), memn, w_cq[l], w_ckv[l], w_co[l])
        x = x + hier_moe(rmsnorm(x, norm_ffn[l]), w_route_group[l], b_route_group[l],
                         w_route_expert[l], b_route_expert[l], w_e_gate[l], w_e_up[l], w_e_down[l])
    return rmsnorm(x, norm_final)
```

```python
import functools
import math

import numpy as np
import jax
import jax.numpy as jnp
from jax import lax
from jax.experimental import pallas as pl
from jax.experimental.pallas import tpu as pltpu

F32 = jnp.float32
BF16 = jnp.bfloat16
I32 = jnp.int32

D_MODEL = 1024
DEPTH = 2
EPS = 1e-6
N_BRANCH = 4
BRANCH_W = 256
CONV_W = 3
GLA_HEADS = 4
GLA_DK = 64
GLA_DV = 64
GLA_RANK = 16
GLA_TAU = 16.0
GLA_CHUNK = 64
S5_GROUP = 16
S5_GROUPS = BRANCH_W // S5_GROUP
S5_STATE = 64
DSA_HEADS = 4
DSA_HEAD_DIM = 64
IDX_HEADS = 8
IDX_DIM = 32
DSA_TOPK = 256
Q_BLOCK = 128
ROPE_THETA = 500000.0
ROPE_FRAC = 4
X_HEADS = 4
X_HEAD_DIM = 128
N_GROUPS = 4
EXPERTS_PER_GROUP = 4
N_EXPERTS = N_GROUPS * EXPERTS_PER_GROUP
D_FF_EXPERT = 512

LANES = 128
SUBLANES = 8
VMEM_LIMIT = 48 * 1024 * 1024

P_CONV = 0
P_S5U = 768
P_GLA = 1024
P_DQ = 2048
P_IQ = 2304
P_KV = 2560
P_IKW = 2688
P_GA = 2816
P_GATES = 3072
P_TOTAL = P_GATES + N_BRANCH * D_MODEL

INT_MIN = -2147483648


def _cparams(sem):
    return pltpu.CompilerParams(dimension_semantics=sem, vmem_limit_bytes=VMEM_LIMIT)


def _dot(a, b):
    return jnp.dot(a, b, preferred_element_type=F32)


def _dot_nt(a, b):
    return lax.dot_general(a, b, (((1,), (1,)), ((), ())), preferred_element_type=F32)


def _dot_tn(a, b):
    return lax.dot_general(a, b, (((0,), (0,)), ((), ())), preferred_element_type=F32)


def _split_dot(exact_bf16, x):
    hi = x.astype(BF16)
    lo = (x - hi.astype(F32)).astype(BF16)
    return _dot(exact_bf16, hi) + _dot(exact_bf16, lo)


def _split_dot_r(x, exact_bf16):
    hi = x.astype(BF16)
    lo = (x - hi.astype(F32)).astype(BF16)
    return _dot(hi, exact_bf16) + _dot(lo, exact_bf16)


def _rms(x, g):
    return x * lax.rsqrt(jnp.mean(x * x, axis=-1, keepdims=True) + EPS) * g


def _sigmoid(x):
    return 1.0 / (1.0 + jnp.exp(-x))


def _norm_matmul_kernel(x_ref, g_ref, w_ref, o_ref, hn_ref):
    @pl.when(pl.program_id(1) == 0)
    def _():
        hn_ref[...] = _rms(x_ref[...], g_ref[...]).astype(BF16)

    o_ref[...] = _dot(hn_ref[...], w_ref[...]).astype(o_ref.dtype)


def norm_matmul(x, g, w, *, tm, tn, out_dtype=BF16):
    t, d = x.shape
    n = w.shape[1]
    return pl.pallas_call(
        _norm_matmul_kernel,
        grid=(t // tm, n // tn),
        in_specs=[
            pl.BlockSpec((tm, d), lambda i, j: (i, 0)),
            pl.BlockSpec((1, d), lambda i, j: (0, 0)),
            pl.BlockSpec((d, tn), lambda i, j: (0, j)),
        ],
        out_specs=pl.BlockSpec((tm, tn), lambda i, j: (i, j)),
        out_shape=jax.ShapeDtypeStruct((t, n), out_dtype),
        scratch_shapes=[pltpu.VMEM((tm, d), BF16)],
        compiler_params=_cparams(("parallel", "arbitrary")),
        name="norm_matmul",
    )(x, g, w)


def _conv_kernel(cur_ref, prev_ref, w_ref, b_ref, o_ref):
    i = pl.program_id(1)
    w_ = BRANCH_W
    cur = cur_ref[0].astype(F32)
    u = cur[:, 2 * w_:3 * w_] * cur[:, 0:w_]
    pv = prev_ref[0].astype(F32)
    pu = pv[:, 2 * w_:3 * w_] * pv[:, 0:w_]
    pu = jnp.where(i > 0, pu, 0.0)
    row = lax.broadcasted_iota(I32, u.shape, 0)
    u1 = jnp.where(row == 0, pu[7:8], pltpu.roll(u, 1, 0))
    u2 = jnp.where(row == 0, pu[6:7], jnp.where(row == 1, pu[7:8], pltpu.roll(u, 2, 0)))
    w = w_ref[...]
    y = w[0:1] * u2 + w[1:2] * u1 + w[2:3] * u + b_ref[...]
    o_ref[0] = (cur[:, w_:2 * w_] * y).astype(o_ref.dtype)


def conv_branch(proj3, conv_w, conv_b, *, ts):
    b, s, _ = proj3.shape
    wc = 3 * BRANCH_W
    hb = ts // SUBLANES
    return pl.pallas_call(
        _conv_kernel,
        grid=(b, s // ts),
        in_specs=[
            pl.BlockSpec((1, ts, wc), lambda bi, i: (bi, i, P_CONV // wc)),
            pl.BlockSpec((1, SUBLANES, wc), lambda bi, i: (bi, jnp.maximum(i * hb - 1, 0), P_CONV // wc)),
            pl.BlockSpec((CONV_W, BRANCH_W), lambda bi, i: (0, 0)),
            pl.BlockSpec((1, BRANCH_W), lambda bi, i: (0, 0)),
        ],
        out_specs=pl.BlockSpec((1, ts, BRANCH_W), lambda bi, i: (bi, i, 0)),
        out_shape=jax.ShapeDtypeStruct((b, s, BRANCH_W), BF16),
        compiler_params=_cparams(("parallel", "arbitrary")),
        name="conv_branch",
    )(proj3, proj3, conv_w, conv_b)


def _gla_kernel(g_ref, a_ref, aup_ref, ab_ref, gn_ref, ltb_ref, ltf_ref, bob_ref, bd_ref, bdb_ref,
                o_ref, st_ref, *, tt):
    hw = GLA_HEADS * GLA_DK
    c = GLA_CHUNK

    @pl.when(pl.program_id(1) == 0)
    def _():
        st_ref[...] = jnp.zeros_like(st_ref)

    blk = g_ref[0]
    q = blk[:, 0:hw].astype(F32)
    k = blk[:, hw:2 * hw].astype(F32)
    vb = blk[:, 2 * hw:3 * hw]
    v = vb.astype(F32)
    r = blk[:, 3 * hw:4 * hw].astype(F32)

    pre = _dot(a_ref[0], aup_ref[...]) + ab_ref[...]
    la = (jnp.minimum(pre, 0.0) - jnp.log(1.0 + jnp.exp(-jnp.abs(pre)))) * (1.0 / GLA_TAU)
    cum = _split_dot(ltb_ref[...], la)
    tot = _split_dot(bob_ref[...], la)
    q_dec = q * (GLA_DK ** -0.5) * jnp.exp(cum)
    k_inv = (k * jnp.exp(-cum)).astype(BF16)
    k_end = (k * jnp.exp(tot - cum)).astype(BF16)
    qdb = q_dec.astype(BF16)

    lane = lax.broadcasted_iota(I32, (1, hw), 1)
    ltmask = ltf_ref[...] > 0.0
    o = jnp.zeros((tt, hw), F32)
    for h in range(GLA_HEADS):
        hm = (lane // GLA_DK) == h
        qh = jnp.where(hm, q_dec, 0.0).astype(BF16)
        att = jnp.where(ltmask, _dot_nt(qh, k_inv), 0.0)
        vh = jnp.where(hm, v, 0.0).astype(BF16)
        o = o + _dot(att.astype(BF16), vh)

    st = st_ref[...]
    bd = bd_ref[...]
    inter = []
    for n in range(tt // c):
        rows = slice(n * c, (n + 1) * c)
        inter.append(_dot_nt(qdb[rows], st.astype(BF16)))
        dec = jnp.exp(tot[n * c:n * c + 1, :])
        st = st * dec + _dot_tn(vb[rows], k_end[rows]) * bd
    st_ref[...] = st
    o = o + jnp.concatenate(inter, axis=0)

    msq = _split_dot_r(o * o, bdb_ref[...]) * (1.0 / GLA_DV)
    y = o * lax.rsqrt(msq + EPS) * gn_ref[...] * (r * _sigmoid(r))
    o_ref[0] = y.astype(o_ref.dtype)


def _gla_consts(tt):
    ri = np.arange(tt)[:, None]
    ci = np.arange(tt)[None, :]
    same = (ri // GLA_CHUNK) == (ci // GLA_CHUNK)
    lt = (same & (ci <= ri)).astype(np.float32)
    hw = GLA_HEADS * GLA_DK
    hi = np.arange(hw)
    bd = ((hi[:, None] // GLA_DK) == (hi[None, :] // GLA_DK)).astype(np.float32)
    return (jnp.asarray(lt, BF16), jnp.asarray(lt, F32), jnp.asarray(same.astype(np.float32), BF16),
            jnp.asarray(bd, F32), jnp.asarray(bd, BF16))


def gla_branch(proj3, a_up_p, a_b, g_norm, *, tt):
    b, s, _ = proj3.shape
    hw = GLA_HEADS * GLA_DK
    ltb, ltf, bob, bd, bdb = _gla_consts(tt)
    const = lambda shape: pl.BlockSpec(shape, lambda bi, i: (0,) * len(shape))
    return pl.pallas_call(
        functools.partial(_gla_kernel, tt=tt),
        grid=(b, s // tt),
        in_specs=[
            pl.BlockSpec((1, tt, 4 * hw), lambda bi, i: (bi, i, P_GLA // (4 * hw))),
            pl.BlockSpec((1, tt, LANES), lambda bi, i: (bi, i, P_GA // LANES)),
            const((LANES, hw)), const((1, hw)), const((1, hw)),
            const((tt, tt)), const((tt, tt)), const((tt, tt)), const((hw, hw)), const((hw, hw)),
        ],
        out_specs=pl.BlockSpec((1, tt, hw), lambda bi, i: (bi, i, 0)),
        out_shape=jax.ShapeDtypeStruct((b, s, hw), BF16),
        scratch_shapes=[pltpu.VMEM((hw, hw), F32)],
        compiler_params=_cparams(("parallel", "arbitrary")),
        name="gla_branch",
    )(proj3, proj3, a_up_p, a_b, g_norm, ltb, ltf, bob, bd, bdb)


def _s5_param_kernel(lre_ref, lim_ref, ldt_ref, bre_ref, bim_ref, bbre_ref, bbim_ref, tab_ref):
    lre = lre_ref[...]
    lim = lim_ref[...]
    dt = jnp.exp(ldt_ref[...])
    mag = jnp.exp(lre * dt)
    lbr = mag * jnp.cos(lim * dt)
    lbi = mag * jnp.sin(lim * dt)
    den = lre * lre + lim * lim
    fre = ((lbr - 1.0) * lre + lbi * lim) / den
    fim = (lbi * lre - (lbr - 1.0) * lim) / den
    bre = bre_ref[...]
    bim = bim_ref[...]
    bbre_ref[...] = fre * bre - fim * bim
    bbim_ref[...] = fre * bim + fim * bre

    pw = [None, (lbr, lbi)]
    for _ in range(2, SUBLANES + 1):
        pr, pi = pw[-1]
        pw.append((pr * lbr - pi * lbi, pr * lbi + pi * lbr))
    n = lre.shape[-1]
    row = lax.broadcasted_iota(I32, (SUBLANES, n), 0)
    zero = jnp.zeros((SUBLANES, n), F32)
    for idx, sft in enumerate((1, 2, 4)):
        tab_ref[2 * idx] = jnp.where(row >= sft, jnp.broadcast_to(pw[sft][0], (SUBLANES, n)), zero)
        tab_ref[2 * idx + 1] = jnp.where(row >= sft, jnp.broadcast_to(pw[sft][1], (SUBLANES, n)), zero)
    cr, ci = zero, zero
    for rr in range(SUBLANES):
        cr = jnp.where(row == rr, jnp.broadcast_to(pw[rr + 1][0], (SUBLANES, n)), cr)
        ci = jnp.where(row == rr, jnp.broadcast_to(pw[rr + 1][1], (SUBLANES, n)), ci)
    tab_ref[6] = cr
    tab_ref[7] = ci


def s5_params(lam_re, lam_im, log_dt, b_re, b_im):
    g, p = lam_re.shape
    n = g * p
    row = lambda a: a.reshape(1, n)
    ldt = jnp.broadcast_to(log_dt[:, None], (g, p))
    bt = lambda a: jnp.transpose(a, (2, 0, 1)).reshape(S5_GROUP, n)
    full = lambda shape: pl.BlockSpec(shape, lambda: (0,) * len(shape))
    return pl.pallas_call(
        _s5_param_kernel,
        in_specs=[full((1, n))] * 3 + [full((S5_GROUP, n))] * 2,
        out_specs=[full((S5_GROUP, n)), full((S5_GROUP, n)), full((8, SUBLANES, n))],
        out_shape=[jax.ShapeDtypeStruct((S5_GROUP, n), F32), jax.ShapeDtypeStruct((S5_GROUP, n), F32),
                   jax.ShapeDtypeStruct((8, SUBLANES, n), F32)],
        name="s5_params",
    )(row(lam_re), row(lam_im), row(ldt), bt(b_re), bt(b_im))


def _s5_kernel(u_ref, bm_ref, tab_ref, cm_ref, d_ref, wg_ref, bg_ref, o_ref, xs_ref, car_ref, *, tt, n):
    @pl.when(pl.program_id(1) == 0)
    def _():
        car_ref[...] = jnp.zeros_like(car_ref)

    ub = u_ref[0]
    xs_ref[...] = _dot(ub, bm_ref[...])

    def group(gi, carry):
        r0 = pl.multiple_of(gi * SUBLANES, SUBLANES)
        for j in range(n // LANES):
            cre = slice(j * LANES, (j + 1) * LANES)
            cim = slice(n + j * LANES, n + (j + 1) * LANES)
            re = xs_ref[pl.ds(r0, SUBLANES), cre]
            im = xs_ref[pl.ds(r0, SUBLANES), cim]
            for idx, sft in enumerate((1, 2, 4)):
                ar = tab_ref[2 * idx, :, cre]
                ai = tab_ref[2 * idx + 1, :, cre]
                sr = pltpu.roll(re, sft, 0)
                si = pltpu.roll(im, sft, 0)
                re, im = re + ar * sr - ai * si, im + ar * si + ai * sr
            pr = tab_ref[6, :, cre]
            pi = tab_ref[7, :, cre]
            cr = car_ref[0, :, cre]
            ci = car_ref[1, :, cre]
            re, im = re + pr * cr - pi * ci, im + pr * ci + pi * cr
            xs_ref[pl.ds(r0, SUBLANES), cre] = re
            xs_ref[pl.ds(r0, SUBLANES), cim] = im
            car_ref[0, :, cre] = jnp.broadcast_to(re[SUBLANES - 1:SUBLANES], (SUBLANES, LANES))
            car_ref[1, :, cre] = jnp.broadcast_to(im[SUBLANES - 1:SUBLANES], (SUBLANES, LANES))
        return carry

    lax.fori_loop(0, tt // SUBLANES, group, 0)

    y = _dot(xs_ref[...].astype(BF16), cm_ref[...]) + d_ref[...] * ub.astype(F32)
    y = 0.5 * y * (1.0 + jnp.tanh(math.sqrt(2.0 / math.pi) * (y + 0.044715 * (y * y * y))))
    z = _dot(y.astype(BF16), wg_ref[...]) + bg_ref[...]
    o_ref[0] = (y * _sigmoid(z)).astype(o_ref.dtype)


def s5_branch(proj3, bmat, tab, cmat, d_skip, w_glu, b_glu, *, tt):
    b, s, _ = proj3.shape
    w_ = BRANCH_W
    n = S5_GROUPS * S5_STATE
    const = lambda shape: pl.BlockSpec(shape, lambda bi, i: (0,) * len(shape))
    return pl.pallas_call(
        functools.partial(_s5_kernel, tt=tt, n=n),
        grid=(b, s // tt),
        in_specs=[
            pl.BlockSpec((1, tt, w_), lambda bi, i: (bi, i, P_S5U // w_)),
            const((w_, 2 * n)), const((8, SUBLANES, n)), const((2 * n, w_)),
            const((1, w_)), const((w_, w_)), const((1, w_)),
        ],
        out_specs=pl.BlockSpec((1, tt, w_), lambda bi, i: (bi, i, 0)),
        out_shape=jax.ShapeDtypeStruct((b, s, w_), BF16),
        scratch_shapes=[pltpu.VMEM((tt, 2 * n), F32), pltpu.VMEM((2, SUBLANES, n), F32)],
        compiler_params=_cparams(("parallel", "arbitrary")),
        name="s5_branch",
    )(proj3, bmat, tab, cmat, d_skip, w_glu, b_glu)


def _rope_freq_rows():
    rows = np.zeros((8, LANES), np.float32)
    for pat, dh in enumerate((DSA_HEAD_DIM, IDX_DIM)):
        rd = dh // ROPE_FRAC
        half = rd // 2
        inv = (np.float32(ROPE_THETA) ** (-np.arange(half, dtype=np.float32) * np.float32(2.0 / rd))).astype(np.float32)
        for l in range(LANES):
            i = l % dh
            if i < half:
                rows[3 * pat, l] = inv[i]
                rows[3 * pat + 1, l] = -1.0
            elif i < rd:
                rows[3 * pat, l] = inv[i - half]
                rows[3 * pat + 2, l] = 1.0
    return rows


def _rope_tab_kernel(pos_ref, fr_ref, o_ref):
    pos = pos_ref[0]
    fr = fr_ref[...]
    for pat in range(2):
        ang = pos * fr[3 * pat:3 * pat + 1]
        c = jnp.cos(ang)
        s = jnp.sin(ang)
        o_ref[0, 3 * pat] = c
        o_ref[0, 3 * pat + 1] = s * fr[3 * pat + 1:3 * pat + 2]
        o_ref[0, 3 * pat + 2] = s * fr[3 * pat + 2:3 * pat + 3]


def rope_tables(positions, *, ts):
    b, s = positions.shape
    pos = positions.astype(F32).reshape(b, s, 1)
    fr = jnp.asarray(_rope_freq_rows())
    return pl.pallas_call(
        _rope_tab_kernel,
        grid=(b, s // ts),
        in_specs=[pl.BlockSpec((1, ts, 1), lambda bi, i: (bi, i, 0)),
                  pl.BlockSpec((8, LANES), lambda bi, i: (0, 0))],
        out_specs=pl.BlockSpec((1, 6, ts, LANES), lambda bi, i: (bi, 0, i, 0)),
        out_shape=jax.ShapeDtypeStruct((b, 6, s, LANES), F32),
        compiler_params=_cparams(("parallel", "arbitrary")),
        name="rope_tables",
    )(pos, fr)


def _rope(t, c, sm, sp, half):
    n = t.shape[-1]
    return t * c + sm * pltpu.roll(t, n - half, 1) + sp * pltpu.roll(t, half, 1)


def _dsa_prep_kernel(dq_ref, iq_ref, kv_ref, ikw_ref, tab_ref,
                     qs_ref, qis_ref, kr_ref, vt_ref, kir_ref, wt_ref):
    two = lambda a: jnp.concatenate([a, a], axis=1)
    c1, sm1, sp1 = tab_ref[0, 0], tab_ref[0, 1], tab_ref[0, 2]
    c2, sm2, sp2 = tab_ref[0, 3], tab_ref[0, 4], tab_ref[0, 5]
    h1 = DSA_HEAD_DIM // ROPE_FRAC // 2
    h2 = IDX_DIM // ROPE_FRAC // 2
    lane = lax.broadcasted_iota(I32, (1, LANES), 1)

    q = _rope(dq_ref[0].astype(F32), two(c1), two(sm1), two(sp1), h1) * (DSA_HEAD_DIM ** -0.5)
    for h in range(DSA_HEADS):
        qs_ref[0, h] = q[:, h * DSA_HEAD_DIM:(h + 1) * DSA_HEAD_DIM].astype(BF16)
    qi = _rope(iq_ref[0].astype(F32), two(c2), two(sm2), two(sp2), h2)
    for h in range(IDX_HEADS):
        qis_ref[0, h] = qi[:, h * IDX_DIM:(h + 1) * IDX_DIM].astype(BF16)

    kv = kv_ref[0].astype(F32)
    isk = lane < DSA_HEAD_DIM
    kvr = _rope(kv, jnp.where(isk, c1, 1.0), jnp.where(isk, sm1, 0.0), jnp.where(isk, sp1, 0.0), h1)
    kr_ref[0] = kvr[:, 0:DSA_HEAD_DIM].astype(BF16)
    vt_ref[0, 0] = kvr.T[DSA_HEAD_DIM:2 * DSA_HEAD_DIM].astype(BF16)

    ikw = ikw_ref[0].astype(F32)
    isi = lane < IDX_DIM
    ikr = _rope(ikw, jnp.where(isi, c2, 1.0), jnp.where(isi, sm2, 0.0), jnp.where(isi, sp2, 0.0), h2)
    kir_ref[0] = ikr[:, 0:IDX_DIM].astype(BF16)
    wt_ref[0] = ikr.T[IDX_DIM:IDX_DIM + IDX_HEADS] * ((IDX_HEADS ** -0.5) * (IDX_DIM ** -0.5))


def dsa_prep(proj3, tabs, *, ts):
    b, s, _ = proj3.shape
    qw = DSA_HEADS * DSA_HEAD_DIM
    iw = IDX_HEADS * IDX_DIM
    return pl.pallas_call(
        _dsa_prep_kernel,
        grid=(b, s // ts),
        in_specs=[
            pl.BlockSpec((1, ts, qw), lambda bi, i: (bi, i, P_DQ // qw)),
            pl.BlockSpec((1, ts, iw), lambda bi, i: (bi, i, P_IQ // iw)),
            pl.BlockSpec((1, ts, LANES), lambda bi, i: (bi, i, P_KV // LANES)),
            pl.BlockSpec((1, ts, LANES), lambda bi, i: (bi, i, P_IKW // LANES)),
            pl.BlockSpec((1, 6, ts, LANES), lambda bi, i: (bi, 0, i, 0)),
        ],
        out_specs=[
            pl.BlockSpec((1, DSA_HEADS, ts, DSA_HEAD_DIM), lambda bi, i: (bi, 0, i, 0)),
            pl.BlockSpec((1, IDX_HEADS, ts, IDX_DIM), lambda bi, i: (bi, 0, i, 0)),
            pl.BlockSpec((1, ts, DSA_HEAD_DIM), lambda bi, i: (bi, i, 0)),
            pl.BlockSpec((1, 1, DSA_HEAD_DIM, ts), lambda bi, i: (bi, i, 0, 0)),
            pl.BlockSpec((1, ts, IDX_DIM), lambda bi, i: (bi, i, 0)),
            pl.BlockSpec((1, IDX_HEADS, ts), lambda bi, i: (bi, 0, i)),
        ],
        out_shape=[
            jax.ShapeDtypeStruct((b, DSA_HEADS, s, DSA_HEAD_DIM), BF16),
            jax.ShapeDtypeStruct((b, IDX_HEADS, s, IDX_DIM), BF16),
            jax.ShapeDtypeStruct((b, s, DSA_HEAD_DIM), BF16),
            jax.ShapeDtypeStruct((b, s // ts, DSA_HEAD_DIM, ts), BF16),
            jax.ShapeDtypeStruct((b, s, IDX_DIM), BF16),
            jax.ShapeDtypeStruct((b, IDX_HEADS, s), F32),
        ],
        compiler_params=_cparams(("parallel", "arbitrary")),
        name="dsa_prep",
    )(proj3, proj3, proj3, proj3, tabs)


def _dsa_kernel(q_ref, qi_ref, w_ref, k_ref, vt_ref, ki_ref, lts_ref, o_ref, sc_ref, *, qb, kt, topk):
    i = pl.program_id(1)
    q0 = i * qb
    nkt = (q0 + qb + kt - 1) // kt
    kf = float(topk)
    qi = qi_ref[0].reshape(IDX_HEADS * qb, IDX_DIM)
    wrow = w_ref[0]
    tq = q0 + lax.broadcasted_iota(I32, (1, qb), 1)

    def scores(t, carry):
        k0 = pl.multiple_of(t * kt, kt)
        s = _dot_nt(ki_ref[0, pl.ds(k0, kt), :], qi)
        acc = jnp.zeros((kt, qb), F32)
        for h in range(IDX_HEADS):
            acc = acc + jnp.maximum(s[:, h * qb:(h + 1) * qb], 0.0) * wrow[h:h + 1]
        acc = jnp.where(acc == 0.0, 0.0, acc)
        bits = pltpu.bitcast(acc, I32)
        key = bits ^ ((bits >> 31) & 0x7FFFFFFF)
        srow = k0 + lax.broadcasted_iota(I32, (kt, qb), 0)
        sc_ref[t] = jnp.where(srow <= tq, key, INT_MIN)
        return carry

    lax.fori_loop(0, nkt, scores, 0)

    def count(pred):
        def body(t, c):
            return c + jnp.sum(jnp.where(pred(sc_ref[t]), 1.0, 0.0), axis=0, keepdims=True)
        return lax.fori_loop(0, nkt, body, jnp.zeros((1, qb), F32))

    need = (tq + 1) > topk

    def rs_cond(c):
        bit, _, done = c
        return jnp.logical_and(bit >= 0, jnp.sum(1.0 - done) > 0.0)

    def rs_body(c):
        bit, thr, done = c
        step = jnp.where(bit == 31, INT_MIN, jnp.left_shift(1, jnp.minimum(bit, 30)))
        cand = thr + step
        cnt = count(lambda kk: kk >= cand)
        active = done == 0.0
        thr = jnp.where(jnp.logical_and(active, cnt >= kf), cand, thr)
        done = jnp.where(cnt == kf, 1.0, done)
        return bit - 1, thr, done

    thr0 = jnp.full((1, qb), INT_MIN, I32)
    done0 = jnp.where(need, 0.0, 1.0)
    _, thr, done = lax.while_loop(rs_cond, rs_body, (jnp.int32(31), thr0, done0))
    thr = jnp.where(need, thr, INT_MIN + 1)

    cge = count(lambda kk: kk >= thr)
    tie = jnp.logical_and(need, cge > kf)

    @pl.when(jnp.sum(jnp.where(tie, 1.0, 0.0)) > 0.0)
    def _():
        cgt = count(lambda kk: kk > thr)
        room = kf - cgt

        def body(t, seen):
            kk = sc_ref[t]
            eq = jnp.logical_and(kk == thr, tie)
            eqf = jnp.where(eq, 1.0, 0.0)
            before = _dot(lts_ref[...], eqf.astype(BF16)) + seen
            sc_ref[t] = jnp.where(jnp.logical_and(eq, before >= room), INT_MIN, kk)
            return seen + jnp.sum(eqf, axis=0, keepdims=True)

        lax.fori_loop(0, nkt, body, jnp.zeros((1, qb), F32))

    nh = DSA_HEADS
    qst = q_ref[0].reshape(nh * qb, DSA_HEAD_DIM)

    def attend(t, c):
        m, l, acc = c
        k0 = pl.multiple_of(t * kt, kt)
        lg = _dot_nt(k_ref[0, pl.ds(k0, kt), :], qst)
        bias = jnp.where(sc_ref[t] >= thr, 0.0, -jnp.inf)
        lg = lg + jnp.concatenate([bias] * nh, axis=1)
        m_new = jnp.maximum(m, jnp.max(lg, axis=0, keepdims=True))
        m_safe = jnp.where(m_new == -jnp.inf, 0.0, m_new)
        p = jnp.exp(lg - m_safe)
        alpha = jnp.exp(m - m_safe)
        l = l * alpha + jnp.sum(p, axis=0, keepdims=True)
        acc = acc * alpha + _dot(vt_ref[0, t], p.astype(BF16))
        return m_new, l, acc

    init = (jnp.full((1, nh * qb), -jnp.inf, F32), jnp.zeros((1, nh * qb), F32),
            jnp.zeros((DSA_HEAD_DIM, nh * qb), F32))
    _, l, acc = lax.fori_loop(0, nkt, attend, init)
    ot = acc / l
    for h in range(nh):
        o_ref[0, :, h * DSA_HEAD_DIM:(h + 1) * DSA_HEAD_DIM] = ot[:, h * qb:(h + 1) * qb].T.astype(o_ref.dtype)


def dsa_attend(qs, qis, wt, kr, vt4, kir, *, qb, kt, topk):
    b, nh, s, dh = qs.shape
    nkt = s // kt
    ri = np.arange(kt)
    lts = jnp.asarray((ri[None, :] < ri[:, None]).astype(np.float32), BF16)
    return pl.pallas_call(
        functools.partial(_dsa_kernel, qb=qb, kt=kt, topk=topk),
        grid=(b, s // qb),
        in_specs=[
            pl.BlockSpec((1, nh, qb, dh), lambda bi, i: (bi, 0, i, 0)),
            pl.BlockSpec((1, IDX_HEADS, qb, IDX_DIM), lambda bi, i: (bi, 0, i, 0)),
            pl.BlockSpec((1, IDX_HEADS, qb), lambda bi, i: (bi, 0, i)),
            pl.BlockSpec((1, s, dh), lambda bi, i: (bi, 0, 0)),
            pl.BlockSpec((1, nkt, dh, kt), lambda bi, i: (bi, 0, 0, 0)),
            pl.BlockSpec((1, s, IDX_DIM), lambda bi, i: (bi, 0, 0)),
            pl.BlockSpec((kt, kt), lambda bi, i: (0, 0)),
        ],
        out_specs=pl.BlockSpec((1, qb, nh * dh), lambda bi, i: (bi, i, 0)),
        out_shape=jax.ShapeDtypeStruct((b, s, nh * dh), BF16),
        scratch_shapes=[pltpu.VMEM((nkt, kt, qb), I32)],
        compiler_params=_cparams(("parallel", "arbitrary")),
        name="dsa_attend",
    )(qs, qis, wt, kr, vt4, kir, lts)


def _merge_kernel(x_ref, ya_ref, yb_ref, yc_ref, yd_ref, g0_ref, g1_ref, g2_ref, g3_ref, wb_ref, wo_ref, o_ref):
    ys = (ya_ref, yb_ref, yc_ref, yd_ref)
    gs = (g0_ref, g1_ref, g2_ref, g3_ref)
    merged = jnp.zeros(o_ref.shape, F32)
    for i in range(N_BRANCH):
        merged = merged + _sigmoid(gs[i][...].astype(F32)) * _dot(ys[i][...], wb_ref[i])
    o_ref[...] = x_ref[...] + _dot(merged.astype(BF16), wo_ref[...])


def merge_out(x2, ya, yb, yc, yd, proj, w_branch, w_out, *, tm):
    t, d = x2.shape
    w_ = BRANCH_W
    yspec = pl.BlockSpec((tm, w_), lambda i: (i, 0))
    gspec = lambda n: pl.BlockSpec((tm, d), lambda i: (i, P_GATES // d + n))
    return pl.pallas_call(
        _merge_kernel,
        grid=(t // tm,),
        in_specs=[pl.BlockSpec((tm, d), lambda i: (i, 0)), yspec, yspec, yspec, yspec,
                  gspec(0), gspec(1), gspec(2), gspec(3),
                  pl.BlockSpec((N_BRANCH, w_, d), lambda i: (0, 0, 0)),
                  pl.BlockSpec((d, d), lambda i: (0, 0))],
        out_specs=pl.BlockSpec((tm, d), lambda i: (i, 0)),
        out_shape=jax.ShapeDtypeStruct((t, d), F32),
        compiler_params=_cparams(("parallel",)),
        name="merge_out",
    )(x2, ya, yb, yc, yd, proj, proj, proj, proj, w_branch, w_out)


def _cross_kernel(x_ref, g_ref, wq_ref, kv_ref, wo_ref, o_ref):
    x = x_ref[0]
    hn = _rms(x, g_ref[...]).astype(BF16)
    q = _dot(hn, wq_ref[...]) * (X_HEAD_DIM ** -0.5)
    kv = kv_ref[0]
    hd = X_HEADS * X_HEAD_DIM
    outs = []
    for h in range(X_HEADS):
        cs = slice(h * X_HEAD_DIM, (h + 1) * X_HEAD_DIM)
        lg = _dot_nt(q[:, cs].astype(BF16), kv[:, cs])
        p = jnp.exp(lg - jnp.max(lg, axis=-1, keepdims=True))
        p = p / jnp.sum(p, axis=-1, keepdims=True)
        outs.append(_dot(p.astype(BF16), kv[:, hd + h * X_HEAD_DIM:hd + (h + 1) * X_HEAD_DIM]))
    o = jnp.concatenate(outs, axis=1).astype(BF16)
    o_ref[0] = x + _dot(o, wo_ref[...])


def cross_attn(x3, g, w_q, kv, w_o, *, tm):
    b, s, d = x3.shape
    m = kv.shape[1]
    hd = X_HEADS * X_HEAD_DIM
    return pl.pallas_call(
        _cross_kernel,
        grid=(b, s // tm),
        in_specs=[pl.BlockSpec((1, tm, d), lambda bi, i: (bi, i, 0)),
                  pl.BlockSpec((1, d), lambda bi, i: (0, 0)),
                  pl.BlockSpec((d, hd), lambda bi, i: (0, 0)),
                  pl.BlockSpec((1, m, 2 * hd), lambda bi, i: (bi, 0, 0)),
                  pl.BlockSpec((hd, d), lambda bi, i: (0, 0))],
        out_specs=pl.BlockSpec((1, tm, d), lambda bi, i: (bi, i, 0)),
        out_shape=jax.ShapeDtypeStruct((b, s, d), F32),
        compiler_params=_cparams(("parallel", "arbitrary")),
        name="cross_attn",
    )(x3, g, w_q, kv, w_o)


def _router_kernel(x_ref, g_ref, wr_ref, br_ref, hn_ref, cmb_ref):
    hn = _rms(x_ref[...], g_ref[...])
    hn_ref[...] = hn.astype(BF16)
    lg = jnp.dot(hn, wr_ref[...], precision=lax.Precision.HIGHEST, preferred_element_type=F32) + br_ref[...]
    lane = lax.broadcasted_iota(I32, lg.shape, 1)
    lanef = lane.astype(F32)
    ninf = -jnp.inf
    big = float(LANES)
    first = lambda mask: jnp.min(jnp.where(mask, lanef, big), axis=-1, keepdims=True)

    isg = jnp.logical_and(lane >= N_EXPERTS, lane < N_EXPERTS + N_GROUPS)
    gl = jnp.where(isg, lg, ninf)
    gmax = jnp.max(gl, axis=-1, keepdims=True)
    gsel = first(gl == gmax) - float(N_EXPERTS)
    pg = 1.0 / jnp.sum(jnp.exp(gl - gmax), axis=-1, keepdims=True)

    ise = jnp.floor(lanef * (1.0 / EXPERTS_PER_GROUP)) == gsel
    el = jnp.where(ise, lg, ninf)
    e1 = jnp.max(el, axis=-1, keepdims=True)
    i1 = first(el == e1)
    el2 = jnp.where(lanef == i1, ninf, el)
    e2 = jnp.max(el2, axis=-1, keepdims=True)
    i2 = first(el2 == e2)
    d = jnp.exp(e2 - e1)
    w1 = 1.0 / (1.0 + d)
    w2 = d / (1.0 + d)
    cmb_ref[...] = jnp.where(lanef == i1, pg * w1, jnp.where(lanef == i2, pg * w2, 0.0))


def moe_router(x2, g, wr, br, *, tm):
    t, d = x2.shape
    return pl.pallas_call(
        _router_kernel,
        grid=(t // tm,),
        in_specs=[pl.BlockSpec((tm, d), lambda i: (i, 0)), pl.BlockSpec((1, d), lambda i: (0, 0)),
                  pl.BlockSpec((d, LANES), lambda i: (0, 0)), pl.BlockSpec((1, LANES), lambda i: (0, 0))],
        out_specs=[pl.BlockSpec((tm, d), lambda i: (i, 0)), pl.BlockSpec((tm, LANES), lambda i: (i, 0))],
        out_shape=[jax.ShapeDtypeStruct((t, d), BF16), jax.ShapeDtypeStruct((t, LANES), F32)],
        compiler_params=_cparams(("parallel",)),
        name="moe_router",
    )(x2, g, wr, br)


def _experts_kernel(x_ref, hn_ref, cmb_ref, wg_ref, wu_ref, wd_ref, gf_ref, o_ref, acc_ref, *, final_norm):
    e = pl.program_id(1)

    @pl.when(e == 0)
    def _():
        acc_ref[...] = x_ref[...]

    hn = hn_ref[...]
    g = _dot(hn, wg_ref[0])
    u = _dot(hn, wu_ref[0])
    cmb = cmb_ref[...]
    lane = lax.broadcasted_iota(I32, cmb.shape, 1)
    c = jnp.sum(jnp.where(lane == e, cmb, 0.0), axis=-1, keepdims=True)
    he = (g * _sigmoid(g)) * u * c
    acc_ref[...] += _dot(he.astype(BF16), wd_ref[0])

    @pl.when(e == pl.num_programs(1) - 1)
    def _():
        if final_norm:
            o_ref[...] = _rms(acc_ref[...], gf_ref[...])
        else:
            o_ref[...] = acc_ref[...]


def moe_experts(x2, hn, cmb, wg, wu, wd, g_final, *, tm, final_norm):
    t, d = x2.shape
    ne, _, f = wg.shape
    return pl.pallas_call(
        functools.partial(_experts_kernel, final_norm=final_norm),
        grid=(t // tm, ne),
        in_specs=[pl.BlockSpec((tm, d), lambda i, e: (i, 0)),
                  pl.BlockSpec((tm, d), lambda i, e: (i, 0)),
                  pl.BlockSpec((tm, LANES), lambda i, e: (i, 0)),
                  pl.BlockSpec((1, d, f), lambda i, e: (e, 0, 0)),
                  pl.BlockSpec((1, d, f), lambda i, e: (e, 0, 0)),
                  pl.BlockSpec((1, f, d), lambda i, e: (e, 0, 0)),
                  pl.BlockSpec((1, d), lambda i, e: (0, 0))],
        out_specs=pl.BlockSpec((tm, d), lambda i, e: (i, 0)),
        out_shape=jax.ShapeDtypeStruct((t, d), F32),
        scratch_shapes=[pltpu.VMEM((tm, d), F32)],
        compiler_params=_cparams(("parallel", "arbitrary")),
        name="moe_experts",
    )(x2, hn, cmb, wg, wu, wd, g_final)


def _permute_w_in(w):
    d = w.shape[0]
    o = 0
    seg = {}
    for name, width in (("c", 768), ("gla", 1024), ("ga", GLA_RANK), ("s5u", 256), ("dq", 256), ("dk", 64),
                        ("dv", 64), ("iq", 256), ("ik", IDX_DIM), ("iw", IDX_HEADS), ("gates", 4096)):
        seg[name] = w[:, o:o + width]
        o += width
    z = lambda n: jnp.zeros((d, n), w.dtype)
    cols = [seg["c"], seg["s5u"], seg["gla"], seg["dq"], seg["iq"], seg["dk"], seg["dv"],
            seg["ik"], seg["iw"], z(LANES - IDX_DIM - IDX_HEADS), seg["ga"], z(LANES - GLA_RANK),
            z(P_GATES - P_GA - LANES), seg["gates"]]
    out = jnp.concatenate(cols, axis=1).astype(BF16)
    assert out.shape[1] == P_TOTAL
    return out


def _s5_matrices(bb_re, bb_im, c_re, c_im):
    g, p, c = S5_GROUPS, S5_STATE, S5_GROUP
    n = g * p
    rows_g = jnp.arange(g * c) // c
    cols_g = jnp.arange(n) // p
    mask = (rows_g[:, None] == cols_g[None, :]).astype(F32)
    bm = jnp.concatenate([jnp.tile(bb_re, (g, 1)) * mask, jnp.tile(bb_im, (g, 1)) * mask], axis=1)
    ct = lambda a: jnp.tile(jnp.transpose(a, (0, 2, 1)).reshape(n, c), (1, g)) * mask.T
    cm = jnp.concatenate([ct(c_re), -ct(c_im)], axis=0)
    return bm.astype(BF16), cm.astype(BF16)


def _pick(s, pref):
    for c in pref:
        if s % c == 0:
            return c
    return s


def kernel(x, mem, positions, norm_mix, w_in, conv_w, conv_b, gla_a_up, gla_a_b, gla_norm, s5_lambda_re,
           s5_lambda_im, s5_log_dt, s5_b_re, s5_b_im, s5_c_re, s5_c_im, s5_d, s5_w_glu, s5_b_glu, w_branch,
           w_out, norm_cross, w_cq, w_ckv, w_co, norm_ffn, w_route_group, b_route_group, w_route_expert,
           b_route_expert, w_e_gate, w_e_up, w_e_down, norm_mem, norm_final):
    b, s, d = x.shape
    t = b * s
    m = mem.shape[1]
    depth = w_in.shape[0]
    topk = min(DSA_TOPK, s // 4)
    ts = _pick(s, (512, 256, 128))
    tm = _pick(t, (1024, 512, 256, 128))
    row = lambda a: a.reshape(1, -1)

    tabs = rope_tables(positions, ts=ts)
    x2 = x.reshape(t, d)
    mem2 = mem.reshape(b * m, d)
    for l in range(depth):
        proj = norm_matmul(x2, row(norm_mix[l]), _permute_w_in(w_in[l]), tm=tm, tn=1024)
        proj3 = proj.reshape(b, s, P_TOTAL)
        ya = conv_branch(proj3, conv_w[l], row(conv_b[l]), ts=ts)
        a_up_p = jnp.concatenate(
            [gla_a_up[l], jnp.zeros((LANES - GLA_RANK, gla_a_up.shape[2]), F32)], axis=0).astype(BF16)
        yb = gla_branch(proj3, a_up_p, row(gla_a_b[l]), row(gla_norm[l]), tt=ts)
        bb_re, bb_im, tab = s5_params(s5_lambda_re[l], s5_lambda_im[l], s5_log_dt[l], s5_b_re[l], s5_b_im[l])
        bm, cm = _s5_matrices(bb_re, bb_im, s5_c_re[l], s5_c_im[l])
        yc = s5_branch(proj3, bm, tab, cm, row(s5_d[l]), s5_w_glu[l].astype(BF16), row(s5_b_glu[l]), tt=ts)
        qs, qis, kr, vt4, kir, wt = dsa_prep(proj3, tabs, ts=ts)
        yd = dsa_attend(qs, qis, wt, kr, vt4, kir, qb=Q_BLOCK, kt=ts, topk=topk)
        w2 = lambda a: a.reshape(t, a.shape[-1])
        x2 = merge_out(x2, w2(ya), w2(yb), w2(yc), w2(yd), proj, w_branch[l].astype(BF16),
                       w_out[l].astype(BF16), tm=min(tm, 512))
        kv = norm_matmul(mem2, row(norm_mem), w_ckv[l].astype(BF16), tm=_pick(b * m, (1024, 512, 256)), tn=1024)
        x2 = cross_attn(x2.reshape(b, s, d), row(norm_cross[l]), w_cq[l].astype(BF16),
                        kv.reshape(b, m, -1), w_co[l].astype(BF16), tm=ts).reshape(t, d)
        pad = jnp.zeros((d, LANES - N_EXPERTS - N_GROUPS), F32)
        wr = jnp.concatenate([w_route_expert[l], w_route_group[l], pad], axis=1)
        br = jnp.concatenate([b_route_expert[l], b_route_group[l], pad[0]], axis=0).reshape(1, LANES)
        hn, cmb = moe_router(x2, row(norm_ffn[l]), wr, br, tm=min(tm, 512))
        x2 = moe_experts(x2, hn, cmb, w_e_gate[l].astype(BF16), w_e_up[l].astype(BF16),
                         w_e_down[l].astype(BF16), row(norm_final), tm=tm, final_norm=(l == depth - 1))
    return x2.reshape(b, s, d)
```

```python
import functools
import math

import numpy as np
import jax
import jax.numpy as jnp
from jax import lax
from jax.experimental import pallas as pl
from jax.experimental.pallas import tpu as pltpu

F32 = jnp.float32
BF16 = jnp.bfloat16
I32 = jnp.int32

D_MODEL = 1024
DEPTH = 2
EPS = 1e-6
N_BRANCH = 4
BRANCH_W = 256
CONV_W = 3
GLA_HEADS = 4
GLA_DK = 64
GLA_DV = 64
GLA_RANK = 16
GLA_TAU = 16.0
GLA_CHUNK = 64
S5_GROUP = 16
S5_GROUPS = BRANCH_W // S5_GROUP
S5_STATE = 64
DSA_HEADS = 4
DSA_HEAD_DIM = 64
IDX_HEADS = 8
IDX_DIM = 32
DSA_TOPK = 256
Q_BLOCK = 128
ROPE_THETA = 500000.0
ROPE_FRAC = 4
X_HEADS = 4
X_HEAD_DIM = 128
N_GROUPS = 4
EXPERTS_PER_GROUP = 4
N_EXPERTS = N_GROUPS * EXPERTS_PER_GROUP
D_FF_EXPERT = 512

LANES = 128
SUBLANES = 8
VMEM_LIMIT = 48 * 1024 * 1024

P_CONV = 0
P_S5U = 768
P_GLA = 1024
P_DQ = 2048
P_IQ = 2304
P_KV = 2560
P_IKW = 2688
P_GA = 2816
P_GATES = 3072
P_TOTAL = P_GATES + N_BRANCH * D_MODEL

INT_MIN = -2147483648


def _cparams(sem):
    return pltpu.CompilerParams(dimension_semantics=sem, vmem_limit_bytes=VMEM_LIMIT)


def _dot(a, b):
    return jnp.dot(a, b, preferred_element_type=F32)


def _dot_nt(a, b):
    return lax.dot_general(a, b, (((1,), (1,)), ((), ())), preferred_element_type=F32)


def _dot_tn(a, b):
    return lax.dot_general(a, b, (((0,), (0,)), ((), ())), preferred_element_type=F32)


def _split_dot(exact_bf16, x):
    hi = x.astype(BF16)
    lo = (x - hi.astype(F32)).astype(BF16)
    return _dot(exact_bf16, hi) + _dot(exact_bf16, lo)


def _split_dot_r(x, exact_bf16):
    hi = x.astype(BF16)
    lo = (x - hi.astype(F32)).astype(BF16)
    return _dot(hi, exact_bf16) + _dot(lo, exact_bf16)


def _rms(x, g):
    return x * lax.rsqrt(jnp.mean(x * x, axis=-1, keepdims=True) + EPS) * g


def _sigmoid(x):
    return 1.0 / (1.0 + jnp.exp(-x))


def _norm_matmul_kernel(x_ref, g_ref, w_ref, o_ref, hn_ref):
    @pl.when(pl.program_id(1) == 0)
    def _():
        hn_ref[...] = _rms(x_ref[...], g_ref[...]).astype(BF16)

    o_ref[...] = _dot(hn_ref[...], w_ref[...]).astype(o_ref.dtype)


def norm_matmul(x, g, w, *, tm, tn, out_dtype=BF16):
    t, d = x.shape
    n = w.shape[1]
    return pl.pallas_call(
        _norm_matmul_kernel,
        grid=(t // tm, n // tn),
        in_specs=[
            pl.BlockSpec((tm, d), lambda i, j: (i, 0)),
            pl.BlockSpec((1, d), lambda i, j: (0, 0)),
            pl.BlockSpec((d, tn), lambda i, j: (0, j)),
        ],
        out_specs=pl.BlockSpec((tm, tn), lambda i, j: (i, j)),
        out_shape=jax.ShapeDtypeStruct((t, n), out_dtype),
        scratch_shapes=[pltpu.VMEM((tm, d), BF16)],
        compiler_params=_cparams(("parallel", "arbitrary")),
        name="norm_matmul",
    )(x, g, w)


def _conv_kernel(cur_ref, prev_ref, w_ref, b_ref, o_ref):
    i = pl.program_id(1)
    w_ = BRANCH_W
    cur = cur_ref[0].astype(F32)
    u = cur[:, 2 * w_:3 * w_] * cur[:, 0:w_]
    pv = prev_ref[0].astype(F32)
    pu = pv[:, 2 * w_:3 * w_] * pv[:, 0:w_]
    pu = jnp.where(i > 0, pu, 0.0)
    row = lax.broadcasted_iota(I32, u.shape, 0)
    u1 = jnp.where(row == 0, pu[7:8], pltpu.roll(u, 1, 0))
    u2 = jnp.where(row == 0, pu[6:7], jnp.where(row == 1, pu[7:8], pltpu.roll(u, 2, 0)))
    w = w_ref[...]
    y = w[0:1] * u2 + w[1:2] * u1 + w[2:3] * u + b_ref[...]
    o_ref[0] = (cur[:, w_:2 * w_] * y).astype(o_ref.dtype)


def conv_branch(proj3, conv_w, conv_b, *, ts):
    b, s, _ = proj3.shape
    wc = 3 * BRANCH_W
    hb = ts // SUBLANES
    return pl.pallas_call(
        _conv_kernel,
        grid=(b, s // ts),
        in_specs=[
            pl.BlockSpec((1, ts, wc), lambda bi, i: (bi, i, P_CONV // wc)),
            pl.BlockSpec((1, SUBLANES, wc), lambda bi, i: (bi, jnp.maximum(i * hb - 1, 0), P_CONV // wc)),
            pl.BlockSpec((CONV_W, BRANCH_W), lambda bi, i: (0, 0)),
            pl.BlockSpec((1, BRANCH_W), lambda bi, i: (0, 0)),
        ],
        out_specs=pl.BlockSpec((1, ts, BRANCH_W), lambda bi, i: (bi, i, 0)),
        out_shape=jax.ShapeDtypeStruct((b, s, BRANCH_W), BF16),
        compiler_params=_cparams(("parallel", "arbitrary")),
        name="conv_branch",
    )(proj3, proj3, conv_w, conv_b)


def _gla_kernel(g_ref, a_ref, aup_ref, ab_ref, gn_ref, ltb_ref, ltf_ref, bob_ref, bd_ref, bdb_ref,
                o_ref, st_ref, *, tt):
    hw = GLA_HEADS * GLA_DK
    c = GLA_CHUNK

    @pl.when(pl.program_id(1) == 0)
    def _():
        st_ref[...] = jnp.zeros_like(st_ref)

    blk = g_ref[0]
    q = blk[:, 0:hw].astype(F32)
    k = blk[:, hw:2 * hw].astype(F32)
    vb = blk[:, 2 * hw:3 * hw]
    v = vb.astype(F32)
    r = blk[:, 3 * hw:4 * hw].astype(F32)

    pre = _dot(a_ref[0], aup_ref[...]) + ab_ref[...]
    la = (jnp.minimum(pre, 0.0) - jnp.log(1.0 + jnp.exp(-jnp.abs(pre)))) * (1.0 / GLA_TAU)
    cum = _split_dot(ltb_ref[...], la)
    tot = _split_dot(bob_ref[...], la)
    q_dec = q * (GLA_DK ** -0.5) * jnp.exp(cum)
    k_inv = (k * jnp.exp(-cum)).astype(BF16)
    k_end = (k * jnp.exp(tot - cum)).astype(BF16)
    qdb = q_dec.astype(BF16)

    lane = lax.broadcasted_iota(I32, (1, hw), 1)
    ltmask = ltf_ref[...] > 0.0
    o = jnp.zeros((tt, hw), F32)
    for h in range(GLA_HEADS):
        hm = (lane // GLA_DK) == h
        qh = jnp.where(hm, q_dec, 0.0).astype(BF16)
        att = jnp.where(ltmask, _dot_nt(qh, k_inv), 0.0)
        vh = jnp.where(hm, v, 0.0).astype(BF16)
        o = o + _dot(att.astype(BF16), vh)

    st = st_ref[...]
    bd = bd_ref[...]
    inter = []
    for n in range(tt // c):
        rows = slice(n * c, (n + 1) * c)
        inter.append(_dot_nt(qdb[rows], st.astype(BF16)))
        dec = jnp.exp(tot[n * c:n * c + 1, :])
        st = st * dec + _dot_tn(vb[rows], k_end[rows]) * bd
    st_ref[...] = st
    o = o + jnp.concatenate(inter, axis=0)

    msq = _split_dot_r(o * o, bdb_ref[...]) * (1.0 / GLA_DV)
    y = o * lax.rsqrt(msq + EPS) * gn_ref[...] * (r * _sigmoid(r))
    o_ref[0] = y.astype(o_ref.dtype)


def _gla_consts(tt):
    ri = np.arange(tt)[:, None]
    ci = np.arange(tt)[None, :]
    same = (ri // GLA_CHUNK) == (ci // GLA_CHUNK)
    lt = (same & (ci <= ri)).astype(np.float32)
    hw = GLA_HEADS * GLA_DK
    hi = np.arange(hw)
    bd = ((hi[:, None] // GLA_DK) == (hi[None, :] // GLA_DK)).astype(np.float32)
    return (jnp.asarray(lt, BF16), jnp.asarray(lt, F32), jnp.asarray(same.astype(np.float32), BF16),
            jnp.asarray(bd, F32), jnp.asarray(bd, BF16))


def gla_branch(proj3, a_up_p, a_b, g_norm, *, tt):
    b, s, _ = proj3.shape
    hw = GLA_HEADS * GLA_DK
    ltb, ltf, bob, bd, bdb = _gla_consts(tt)
    const = lambda shape: pl.BlockSpec(shape, lambda bi, i: (0,) * len(shape))
    return pl.pallas_call(
        functools.partial(_gla_kernel, tt=tt),
        grid=(b, s // tt),
        in_specs=[
            pl.BlockSpec((1, tt, 4 * hw), lambda bi, i: (bi, i, P_GLA // (4 * hw))),
            pl.BlockSpec((1, tt, LANES), lambda bi, i: (bi, i, P_GA // LANES)),
            const((LANES, hw)), const((1, hw)), const((1, hw)),
            const((tt, tt)), const((tt, tt)), const((tt, tt)), const((hw, hw)), const((hw, hw)),
        ],
        out_specs=pl.BlockSpec((1, tt, hw), lambda bi, i: (bi, i, 0)),
        out_shape=jax.ShapeDtypeStruct((b, s, hw), BF16),
        scratch_shapes=[pltpu.VMEM((hw, hw), F32)],
        compiler_params=_cparams(("parallel", "arbitrary")),
        name="gla_branch",
    )(proj3, proj3, a_up_p, a_b, g_norm, ltb, ltf, bob, bd, bdb)


def _s5_param_kernel(lre_ref, lim_ref, ldt_ref, bre_ref, bim_ref, bbre_ref, bbim_ref, tab_ref):
    lre = lre_ref[...]
    lim = lim_ref[...]
    dt = jnp.exp(ldt_ref[...])
    mag = jnp.exp(lre * dt)
    lbr = mag * jnp.cos(lim * dt)
    lbi = mag * jnp.sin(lim * dt)
    den = lre * lre + lim * lim
    fre = ((lbr - 1.0) * lre + lbi * lim) / den
    fim = (lbi * lre - (lbr - 1.0) * lim) / den
    bre = bre_ref[...]
    bim = bim_ref[...]
    bbre_ref[...] = fre * bre - fim * bim
    bbim_ref[...] = fre * bim + fim * bre

    pw = [None, (lbr, lbi)]
    for _ in range(2, SUBLANES + 1):
        pr, pi = pw[-1]
        pw.append((pr * lbr - pi * lbi, pr * lbi + pi * lbr))
    n = lre.shape[-1]
    row = lax.broadcasted_iota(I32, (SUBLANES, n), 0)
    zero = jnp.zeros((SUBLANES, n), F32)
    for idx, sft in enumerate((1, 2, 4)):
        tab_ref[2 * idx] = jnp.where(row >= sft, jnp.broadcast_to(pw[sft][0], (SUBLANES, n)), zero)
        tab_ref[2 * idx + 1] = jnp.where(row >= sft, jnp.broadcast_to(pw[sft][1], (SUBLANES, n)), zero)
    cr, ci = zero, zero
    for rr in range(SUBLANES):
        cr = jnp.where(row == rr, jnp.broadcast_to(pw[rr + 1][0], (SUBLANES, n)), cr)
        ci = jnp.where(row == rr, jnp.broadcast_to(pw[rr + 1][1], (SUBLANES, n)), ci)
    tab_ref[6] = cr
    tab_ref[7] = ci


def s5_params(lam_re, lam_im, log_dt, b_re, b_im):
    g, p = lam_re.shape
    n = g * p
    row = lambda a: a.reshape(1, n)
    ldt = jnp.broadcast_to(log_dt[:, None], (g, p))
    bt = lambda a: jnp.transpose(a, (2, 0, 1)).reshape(S5_GROUP, n)
    full = lambda shape: pl.BlockSpec(shape, lambda: (0,) * len(shape))
    return pl.pallas_call(
        _s5_param_kernel,
        in_specs=[full((1, n))] * 3 + [full((S5_GROUP, n))] * 2,
        out_specs=[full((S5_GROUP, n)), full((S5_GROUP, n)), full((8, SUBLANES, n))],
        out_shape=[jax.ShapeDtypeStruct((S5_GROUP, n), F32), jax.ShapeDtypeStruct((S5_GROUP, n), F32),
                   jax.ShapeDtypeStruct((8, SUBLANES, n), F32)],
        name="s5_params",
    )(row(lam_re), row(lam_im), row(ldt), bt(b_re), bt(b_im))


def _s5_kernel(u_ref, bm_ref, tab_ref, cm_ref, d_ref, wg_ref, bg_ref, o_ref, xs_ref, car_ref, *, tt, n):
    @pl.when(pl.program_id(1) == 0)
    def _():
        car_ref[...] = jnp.zeros_like(car_ref)

    ub = u_ref[0]
    xs_ref[...] = _dot(ub, bm_ref[...])

    def group(gi, carry):
        r0 = pl.multiple_of(gi * SUBLANES, SUBLANES)
        for j in range(n // LANES):
            cre = slice(j * LANES, (j + 1) * LANES)
            cim = slice(n + j * LANES, n + (j + 1) * LANES)
            re = xs_ref[pl.ds(r0, SUBLANES), cre]
            im = xs_ref[pl.ds(r0, SUBLANES), cim]
            for idx, sft in enumerate((1, 2, 4)):
                ar = tab_ref[2 * idx, :, cre]
                ai = tab_ref[2 * idx + 1, :, cre]
                sr = pltpu.roll(re, sft, 0)
                si = pltpu.roll(im, sft, 0)
                re, im = re + ar * sr - ai * si, im + ar * si + ai * sr
            pr = tab_ref[6, :, cre]
            pi = tab_ref[7, :, cre]
            cr = car_ref[0, :, cre]
            ci = car_ref[1, :, cre]
            re, im = re + pr * cr - pi * ci, im + pr * ci + pi * cr
            xs_ref[pl.ds(r0, SUBLANES), cre] = re
            xs_ref[pl.ds(r0, SUBLANES), cim] = im
            car_ref[0, :, cre] = jnp.broadcast_to(re[SUBLANES - 1:SUBLANES], (SUBLANES, LANES))
            car_ref[1, :, cre] = jnp.broadcast_to(im[SUBLANES - 1:SUBLANES], (SUBLANES, LANES))
        return carry

    lax.fori_loop(0, tt // SUBLANES, group, 0)

    y = _dot(xs_ref[...].astype(BF16), cm_ref[...]) + d_ref[...] * ub.astype(F32)
    y = 0.5 * y * (1.0 + jnp.tanh(math.sqrt(2.0 / math.pi) * (y + 0.044715 * (y * y * y))))
    z = _dot(y.astype(BF16), wg_ref[...]) + bg_ref[...]
    o_ref[0] = (y * _sigmoid(z)).astype(o_ref.dtype)


def s5_branch(proj3, bmat, tab, cmat, d_skip, w_glu, b_glu, *, tt):
    b, s, _ = proj3.shape
    w_ = BRANCH_W
    n = S5_GROUPS * S5_STATE
    const = lambda shape: pl.BlockSpec(shape, lambda bi, i: (0,) * len(shape))
    return pl.pallas_call(
        functools.partial(_s5_kernel, tt=tt, n=n),
        grid=(b, s // tt),
        in_specs=[
            pl.BlockSpec((1, tt, w_), lambda bi, i: (bi, i, P_S5U // w_)),
            const((w_, 2 * n)), const((8, SUBLANES, n)), const((2 * n, w_)),
            const((1, w_)), const((w_, w_)), const((1, w_)),
        ],
        out_specs=pl.BlockSpec((1, tt, w_), lambda bi, i: (bi, i, 0)),
        out_shape=jax.ShapeDtypeStruct((b, s, w_), BF16),
        scratch_shapes=[pltpu.VMEM((tt, 2 * n), F32), pltpu.VMEM((2, SUBLANES, n), F32)],
        compiler_params=_cparams(("parallel", "arbitrary")),
        name="s5_branch",
    )(proj3, bmat, tab, cmat, d_skip, w_glu, b_glu)


def _rope_freq_rows():
    rows = np.zeros((8, LANES), np.float32)
    for pat, dh in enumerate((DSA_HEAD_DIM, IDX_DIM)):
        rd = dh // ROPE_FRAC
        half = rd // 2
        inv = (np.float32(ROPE_THETA) ** (-np.arange(half, dtype=np.float32) * np.float32(2.0 / rd))).astype(np.float32)
        for l in range(LANES):
            i = l % dh
            if i < half:
                rows[3 * pat, l] = inv[i]
                rows[3 * pat + 1, l] = -1.0
            elif i < rd:
                rows[3 * pat, l] = inv[i - half]
                rows[3 * pat + 2, l] = 1.0
    return rows


def _rope_tab_kernel(pos_ref, fr_ref, o_ref):
    pos = pos_ref[0]
    fr = fr_ref[...]
    for pat in range(2):
        ang = pos * fr[3 * pat:3 * pat + 1]
        c = jnp.cos(ang)
        s = jnp.sin(ang)
        o_ref[0, 3 * pat] = c
        o_ref[0, 3 * pat + 1] = s * fr[3 * pat + 1:3 * pat + 2]
        o_ref[0, 3 * pat + 2] = s * fr[3 * pat + 2:3 * pat + 3]


def rope_tables(positions, *, ts):
    b, s = positions.shape
    pos = positions.astype(F32).reshape(b, s, 1)
    fr = jnp.asarray(_rope_freq_rows())
    return pl.pallas_call(
        _rope_tab_kernel,
        grid=(b, s // ts),
        in_specs=[pl.BlockSpec((1, ts, 1), lambda bi, i: (bi, i, 0)),
                  pl.BlockSpec((8, LANES), lambda bi, i: (0, 0))],
        out_specs=pl.BlockSpec((1, 6, ts, LANES), lambda bi, i: (bi, 0, i, 0)),
        out_shape=jax.ShapeDtypeStruct((b, 6, s, LANES), F32),
        compiler_params=_cparams(("parallel", "arbitrary")),
        name="rope_tables",
    )(pos, fr)


def _rope(t, c, sm, sp, half):
    n = t.shape[-1]
    return t * c + sm * pltpu.roll(t, n - half, 1) + sp * pltpu.roll(t, half, 1)


def _dsa_prep_kernel(dq_ref, iq_ref, kv_ref, ikw_ref, tab_ref,
                     qs_ref, qis_ref, kr_ref, vt_ref, kir_ref, wt_ref):
    two = lambda a: jnp.concatenate([a, a], axis=1)
    c1, sm1, sp1 = tab_ref[0, 0], tab_ref[0, 1], tab_ref[0, 2]
    c2, sm2, sp2 = tab_ref[0, 3], tab_ref[0, 4], tab_ref[0, 5]
    h1 = DSA_HEAD_DIM // ROPE_FRAC // 2
    h2 = IDX_DIM // ROPE_FRAC // 2
    lane = lax.broadcasted_iota(I32, (1, LANES), 1)

    q = _rope(dq_ref[0].astype(F32), two(c1), two(sm1), two(sp1), h1) * (DSA_HEAD_DIM ** -0.5 * math.log2(math.e))
    for h in range(DSA_HEADS):
        qs_ref[0, h] = q[:, h * DSA_HEAD_DIM:(h + 1) * DSA_HEAD_DIM].astype(BF16)
    qi = _rope(iq_ref[0].astype(F32), two(c2), two(sm2), two(sp2), h2)
    for h in range(IDX_HEADS):
        qis_ref[0, h] = qi[:, h * IDX_DIM:(h + 1) * IDX_DIM].astype(BF16)

    kv = kv_ref[0].astype(F32)
    isk = lane < DSA_HEAD_DIM
    kvr = _rope(kv, jnp.where(isk, c1, 1.0), jnp.where(isk, sm1, 0.0), jnp.where(isk, sp1, 0.0), h1)
    kr_ref[0] = kvr[:, 0:DSA_HEAD_DIM].astype(BF16)
    vt_ref[0, 0] = kvr.T[DSA_HEAD_DIM:2 * DSA_HEAD_DIM].astype(BF16)

    ikw = ikw_ref[0].astype(F32)
    isi = lane < IDX_DIM
    ikr = _rope(ikw, jnp.where(isi, c2, 1.0), jnp.where(isi, sm2, 0.0), jnp.where(isi, sp2, 0.0), h2)
    kir_ref[0] = ikr[:, 0:IDX_DIM].astype(BF16)
    wt_ref[0] = ikr.T[IDX_DIM:IDX_DIM + IDX_HEADS] * ((IDX_HEADS ** -0.5) * (IDX_DIM ** -0.5))


def dsa_prep(proj3, tabs, *, ts):
    b, s, _ = proj3.shape
    qw = DSA_HEADS * DSA_HEAD_DIM
    iw = IDX_HEADS * IDX_DIM
    return pl.pallas_call(
        _dsa_prep_kernel,
        grid=(b, s // ts),
        in_specs=[
            pl.BlockSpec((1, ts, qw), lambda bi, i: (bi, i, P_DQ // qw)),
            pl.BlockSpec((1, ts, iw), lambda bi, i: (bi, i, P_IQ // iw)),
            pl.BlockSpec((1, ts, LANES), lambda bi, i: (bi, i, P_KV // LANES)),
            pl.BlockSpec((1, ts, LANES), lambda bi, i: (bi, i, P_IKW // LANES)),
            pl.BlockSpec((1, 6, ts, LANES), lambda bi, i: (bi, 0, i, 0)),
        ],
        out_specs=[
            pl.BlockSpec((1, DSA_HEADS, ts, DSA_HEAD_DIM), lambda bi, i: (bi, 0, i, 0)),
            pl.BlockSpec((1, IDX_HEADS, ts, IDX_DIM), lambda bi, i: (bi, 0, i, 0)),
            pl.BlockSpec((1, ts, DSA_HEAD_DIM), lambda bi, i: (bi, i, 0)),
            pl.BlockSpec((1, 1, DSA_HEAD_DIM, ts), lambda bi, i: (bi, i, 0, 0)),
            pl.BlockSpec((1, ts, IDX_DIM), lambda bi, i: (bi, i, 0)),
            pl.BlockSpec((1, IDX_HEADS, ts), lambda bi, i: (bi, 0, i)),
        ],
        out_shape=[
            jax.ShapeDtypeStruct((b, DSA_HEADS, s, DSA_HEAD_DIM), BF16),
            jax.ShapeDtypeStruct((b, IDX_HEADS, s, IDX_DIM), BF16),
            jax.ShapeDtypeStruct((b, s, DSA_HEAD_DIM), BF16),
            jax.ShapeDtypeStruct((b, s // ts, DSA_HEAD_DIM, ts), BF16),
            jax.ShapeDtypeStruct((b, s, IDX_DIM), BF16),
            jax.ShapeDtypeStruct((b, IDX_HEADS, s), F32),
        ],
        compiler_params=_cparams(("parallel", "arbitrary")),
        name="dsa_prep",
    )(proj3, proj3, proj3, proj3, tabs)


def _dsa_kernel(q_ref, qi_ref, w_ref, k_ref, vt_ref, ki_ref, lts_ref, o_ref, sc_ref, lg_ref, *, qb, kt, topk):
    i = pl.program_id(1)
    q0 = i * qb
    nkt = (q0 + qb + kt - 1) // kt
    kf = float(topk)
    sg = 8 * SUBLANES
    wrow = w_ref[0]
    tq = q0 + lax.broadcasted_iota(I32, (1, qb), 1)

    def to_key(x):
        bits = pltpu.bitcast(x, I32)
        return bits ^ ((bits >> 31) & 0x7FFFFFFF)

    def scores(t, gmax):
        k0 = pl.multiple_of(t * kt, kt)
        kit = ki_ref[0, pl.ds(k0, kt), :]
        acc = jnp.zeros((kt, qb), F32)
        for h in range(IDX_HEADS):
            acc = acc + jnp.maximum(_dot_nt(kit, qi_ref[0, h]), 0.0) * wrow[h:h + 1]
        acc = jnp.where(acc == 0.0, 0.0, acc)
        srow = k0 + lax.broadcasted_iota(I32, (kt, qb), 0)
        causal = srow <= tq
        sc_ref[t] = jnp.where(causal, to_key(acc), INT_MIN)
        accm = jnp.where(causal, acc, -jnp.inf)
        return jnp.maximum(gmax, jnp.max(accm.reshape(kt // topk, topk, qb), axis=0))

    gmax = lax.fori_loop(0, nkt, scores, jnp.full((topk, qb), -jnp.inf, F32))

    def count(pred):
        def body(t, c):
            m = jnp.where(pred(sc_ref[t]), 1.0, 0.0)
            return c + jnp.sum(m.reshape(kt // sg, sg, qb), axis=0)
        c = lax.fori_loop(0, nkt, body, jnp.zeros((sg, qb), F32))
        return jnp.sum(c, axis=0, keepdims=True)

    need = (tq + 1) > topk
    lo0 = to_key(jnp.min(gmax, axis=0, keepdims=True))
    hi0 = to_key(jnp.max(gmax, axis=0, keepdims=True))
    done0 = jnp.where(jnp.logical_and(need, lo0 < hi0), 0.0, 1.0)

    def bs_cond(c):
        it, _, _, _, done = c
        return jnp.logical_and(it < 34, jnp.sum(1.0 - done) > 0.0)

    def bs_step(lo, hi, thr, done, probe=None):
        mid = (lo >> 1) + (hi >> 1) + (((lo & 1) + (hi & 1) + 1) >> 1)
        if probe is not None:
            mid = jnp.where(jnp.logical_and(lo < probe, probe <= hi), probe, mid)
        cnt = count(lambda kk: kk >= mid)
        ge = cnt >= kf
        lo = jnp.where(ge, mid, lo)
        hi = jnp.where(ge, hi, mid - 1)
        fin = jnp.where(cnt == kf, 1.0, jnp.where(lo == hi, 1.0, 0.0))
        val = jnp.where(cnt == kf, mid, lo)
        thr = jnp.where(done == 0.0, val, thr)
        return lo, hi, thr, jnp.maximum(done, fin)

    def bs_body(c):
        it, lo, hi, thr, done = c
        lo, hi, thr, done = bs_step(*bs_step(lo, hi, thr, done))
        return it + 2, lo, hi, thr, done

    st = bs_step(*bs_step(lo0, hi0, lo0, done0, probe=1), probe=0)
    _, _, _, thr, _ = lax.while_loop(bs_cond, bs_body, (jnp.int32(2),) + st)
    thr = jnp.where(need, thr, INT_MIN + 1)

    cge = count(lambda kk: kk >= thr)
    tie = jnp.logical_and(need, cge > kf)

    @pl.when(jnp.sum(jnp.where(tie, 1.0, 0.0)) > 0.0)
    def _():
        cgt = count(lambda kk: kk > thr)
        room = kf - cgt

        def body(t, seen):
            kk = sc_ref[t]
            eq = jnp.logical_and(kk == thr, tie)
            eqf = jnp.where(eq, 1.0, 0.0)
            before = _dot(lts_ref[...], eqf.astype(BF16)) + seen
            sc_ref[t] = jnp.where(jnp.logical_and(eq, before >= room), INT_MIN, kk)
            return seen + jnp.sum(eqf, axis=0, keepdims=True)

        lax.fori_loop(0, nkt, body, jnp.zeros((1, qb), F32))

    nh = DSA_HEADS

    def logits(t, c):
        k0 = pl.multiple_of(t * kt, kt)
        kt_ = k_ref[0, pl.ds(k0, kt), :]
        bias = jnp.where(sc_ref[t] >= thr, 0.0, -jnp.inf)
        out = []
        for h in range(nh):
            lg = _dot_nt(kt_, q_ref[0, h]) + bias
            lg_ref[t, h] = lg
            out.append(jnp.maximum(c[h], jnp.max(lg.reshape(kt // sg, sg, qb), axis=0)))
        return tuple(out)

    mparts = lax.fori_loop(0, nkt, logits, (jnp.full((sg, qb), -jnp.inf, F32),) * nh)
    ms = []
    for h in range(nh):
        m = jnp.max(mparts[h], axis=0, keepdims=True)
        ms.append(jnp.where(m == -jnp.inf, 0.0, m))

    def attend(t, c):
        vt = vt_ref[0, t]
        out = []
        for h in range(nh):
            l, acc = c[2 * h:2 * h + 2]
            p = jnp.exp2(lg_ref[t, h] - ms[h])
            out += [l + jnp.sum(p.reshape(kt // sg, sg, qb), axis=0), acc + _dot(vt, p.astype(BF16))]
        return tuple(out)

    init = (jnp.zeros((sg, qb), F32), jnp.zeros((DSA_HEAD_DIM, qb), F32)) * nh
    res = lax.fori_loop(0, nkt, attend, init)
    for h in range(nh):
        ot = res[2 * h + 1] / jnp.sum(res[2 * h], axis=0, keepdims=True)
        o_ref[0, :, h * DSA_HEAD_DIM:(h + 1) * DSA_HEAD_DIM] = ot.T.astype(o_ref.dtype)


def dsa_attend(qs, qis, wt, kr, vt4, kir, *, qb, kt, topk):
    b, nh, s, dh = qs.shape
    nkt = s // kt
    ri = np.arange(kt)
    lts = jnp.asarray((ri[None, :] < ri[:, None]).astype(np.float32), BF16)
    return pl.pallas_call(
        functools.partial(_dsa_kernel, qb=qb, kt=kt, topk=topk),
        grid=(b, s // qb),
        in_specs=[
            pl.BlockSpec((1, nh, qb, dh), lambda bi, i: (bi, 0, i, 0)),
            pl.BlockSpec((1, IDX_HEADS, qb, IDX_DIM), lambda bi, i: (bi, 0, i, 0)),
            pl.BlockSpec((1, IDX_HEADS, qb), lambda bi, i: (bi, 0, i)),
            pl.BlockSpec((1, s, dh), lambda bi, i: (bi, 0, 0)),
            pl.BlockSpec((1, nkt, dh, kt), lambda bi, i: (bi, 0, 0, 0)),
            pl.BlockSpec((1, s, IDX_DIM), lambda bi, i: (bi, 0, 0)),
            pl.BlockSpec((kt, kt), lambda bi, i: (0, 0)),
        ],
        out_specs=pl.BlockSpec((1, qb, nh * dh), lambda bi, i: (bi, i, 0)),
        out_shape=jax.ShapeDtypeStruct((b, s, nh * dh), BF16),
        scratch_shapes=[pltpu.VMEM((nkt, kt, qb), I32), pltpu.VMEM((nkt, nh, kt, qb), F32)],
        compiler_params=_cparams(("parallel", "arbitrary")),
        name="dsa_attend",
    )(qs, qis, wt, kr, vt4, kir, lts)


def _merge_kernel(x_ref, ya_ref, yb_ref, yc_ref, yd_ref, g0_ref, g1_ref, g2_ref, g3_ref, wb_ref, wo_ref, o_ref):
    ys = (ya_ref, yb_ref, yc_ref, yd_ref)
    gs = (g0_ref, g1_ref, g2_ref, g3_ref)
    merged = jnp.zeros(o_ref.shape, F32)
    for i in range(N_BRANCH):
        merged = merged + _sigmoid(gs[i][...].astype(F32)) * _dot(ys[i][...], wb_ref[i])
    o_ref[...] = x_ref[...] + _dot(merged.astype(BF16), wo_ref[...])


def merge_out(x2, ya, yb, yc, yd, proj, w_branch, w_out, *, tm):
    t, d = x2.shape
    w_ = BRANCH_W
    yspec = pl.BlockSpec((tm, w_), lambda i: (i, 0))
    gspec = lambda n: pl.BlockSpec((tm, d), lambda i: (i, P_GATES // d + n))
    return pl.pallas_call(
        _merge_kernel,
        grid=(t // tm,),
        in_specs=[pl.BlockSpec((tm, d), lambda i: (i, 0)), yspec, yspec, yspec, yspec,
                  gspec(0), gspec(1), gspec(2), gspec(3),
                  pl.BlockSpec((N_BRANCH, w_, d), lambda i: (0, 0, 0)),
                  pl.BlockSpec((d, d), lambda i: (0, 0))],
        out_specs=pl.BlockSpec((tm, d), lambda i: (i, 0)),
        out_shape=jax.ShapeDtypeStruct((t, d), F32),
        compiler_params=_cparams(("parallel",)),
        name="merge_out",
    )(x2, ya, yb, yc, yd, proj, proj, proj, proj, w_branch, w_out)


def _cross_kernel(x_ref, g_ref, wq_ref, kv_ref, wo_ref, o_ref):
    x = x_ref[0]
    hn = _rms(x, g_ref[...]).astype(BF16)
    q = _dot(hn, wq_ref[...]) * (X_HEAD_DIM ** -0.5)
    kv = kv_ref[0]
    hd = X_HEADS * X_HEAD_DIM
    outs = []
    for h in range(X_HEADS):
        cs = slice(h * X_HEAD_DIM, (h + 1) * X_HEAD_DIM)
        lg = _dot_nt(q[:, cs].astype(BF16), kv[:, cs])
        p = jnp.exp(lg - jnp.max(lg, axis=-1, keepdims=True))
        p = p / jnp.sum(p, axis=-1, keepdims=True)
        outs.append(_dot(p.astype(BF16), kv[:, hd + h * X_HEAD_DIM:hd + (h + 1) * X_HEAD_DIM]))
    o = jnp.concatenate(outs, axis=1).astype(BF16)
    o_ref[0] = x + _dot(o, wo_ref[...])


def cross_attn(x3, g, w_q, kv, w_o, *, tm):
    b, s, d = x3.shape
    m = kv.shape[1]
    hd = X_HEADS * X_HEAD_DIM
    return pl.pallas_call(
        _cross_kernel,
        grid=(b, s // tm),
        in_specs=[pl.BlockSpec((1, tm, d), lambda bi, i: (bi, i, 0)),
                  pl.BlockSpec((1, d), lambda bi, i: (0, 0)),
                  pl.BlockSpec((d, hd), lambda bi, i: (0, 0)),
                  pl.BlockSpec((1, m, 2 * hd), lambda bi, i: (bi, 0, 0)),
                  pl.BlockSpec((hd, d), lambda bi, i: (0, 0))],
        out_specs=pl.BlockSpec((1, tm, d), lambda bi, i: (bi, i, 0)),
        out_shape=jax.ShapeDtypeStruct((b, s, d), F32),
        compiler_params=_cparams(("parallel", "arbitrary")),
        name="cross_attn",
    )(x3, g, w_q, kv, w_o)


def _router_kernel(x_ref, g_ref, wr_ref, br_ref, hn_ref, cmb_ref):
    hn = _rms(x_ref[...], g_ref[...])
    hn_ref[...] = hn.astype(BF16)
    lg = jnp.dot(hn, wr_ref[...], precision=lax.Precision.HIGHEST, preferred_element_type=F32) + br_ref[...]
    lane = lax.broadcasted_iota(I32, lg.shape, 1)
    lanef = lane.astype(F32)
    ninf = -jnp.inf
    big = float(LANES)
    first = lambda mask: jnp.min(jnp.where(mask, lanef, big), axis=-1, keepdims=True)

    isg = jnp.logical_and(lane >= N_EXPERTS, lane < N_EXPERTS + N_GROUPS)
    gl = jnp.where(isg, lg, ninf)
    gmax = jnp.max(gl, axis=-1, keepdims=True)
    gsel = first(gl == gmax) - float(N_EXPERTS)
    pg = 1.0 / jnp.sum(jnp.exp(gl - gmax), axis=-1, keepdims=True)

    ise = jnp.floor(lanef * (1.0 / EXPERTS_PER_GROUP)) == gsel
    el = jnp.where(ise, lg, ninf)
    e1 = jnp.max(el, axis=-1, keepdims=True)
    i1 = first(el == e1)
    el2 = jnp.where(lanef == i1, ninf, el)
    e2 = jnp.max(el2, axis=-1, keepdims=True)
    i2 = first(el2 == e2)
    d = jnp.exp(e2 - e1)
    w1 = 1.0 / (1.0 + d)
    w2 = d / (1.0 + d)
    cmb_ref[...] = jnp.where(lanef == i1, pg * w1, jnp.where(lanef == i2, pg * w2, 0.0))


def moe_router(x2, g, wr, br, *, tm):
    t, d = x2.shape
    return pl.pallas_call(
        _router_kernel,
        grid=(t // tm,),
        in_specs=[pl.BlockSpec((tm, d), lambda i: (i, 0)), pl.BlockSpec((1, d), lambda i: (0, 0)),
                  pl.BlockSpec((d, LANES), lambda i: (0, 0)), pl.BlockSpec((1, LANES), lambda i: (0, 0))],
        out_specs=[pl.BlockSpec((tm, d), lambda i: (i, 0)), pl.BlockSpec((tm, LANES), lambda i: (i, 0))],
        out_shape=[jax.ShapeDtypeStruct((t, d), BF16), jax.ShapeDtypeStruct((t, LANES), F32)],
        compiler_params=_cparams(("parallel",)),
        name="moe_router",
    )(x2, g, wr, br)


def _experts_kernel(x_ref, hn_ref, cmb_ref, wg_ref, wu_ref, wd_ref, gf_ref, o_ref, acc_ref, *, final_norm):
    e = pl.program_id(1)

    @pl.when(e == 0)
    def _():
        acc_ref[...] = x_ref[...]

    hn = hn_ref[...]
    g = _dot(hn, wg_ref[0])
    u = _dot(hn, wu_ref[0])
    cmb = cmb_ref[...]
    lane = lax.broadcasted_iota(I32, cmb.shape, 1)
    c = jnp.sum(jnp.where(lane == e, cmb, 0.0), axis=-1, keepdims=True)
    he = (g * _sigmoid(g)) * u * c
    acc_ref[...] += _dot(he.astype(BF16), wd_ref[0])

    @pl.when(e == pl.num_programs(1) - 1)
    def _():
        if final_norm:
            o_ref[...] = _rms(acc_ref[...], gf_ref[...])
        else:
            o_ref[...] = acc_ref[...]


def moe_experts(x2, hn, cmb, wg, wu, wd, g_final, *, tm, final_norm):
    t, d = x2.shape
    ne, _, f = wg.shape
    return pl.pallas_call(
        functools.partial(_experts_kernel, final_norm=final_norm),
        grid=(t // tm, ne),
        in_specs=[pl.BlockSpec((tm, d), lambda i, e: (i, 0)),
                  pl.BlockSpec((tm, d), lambda i, e: (i, 0)),
                  pl.BlockSpec((tm, LANES), lambda i, e: (i, 0)),
                  pl.BlockSpec((1, d, f), lambda i, e: (e, 0, 0)),
                  pl.BlockSpec((1, d, f), lambda i, e: (e, 0, 0)),
                  pl.BlockSpec((1, f, d), lambda i, e: (e, 0, 0)),
                  pl.BlockSpec((1, d), lambda i, e: (0, 0))],
        out_specs=pl.BlockSpec((tm, d), lambda i, e: (i, 0)),
        out_shape=jax.ShapeDtypeStruct((t, d), F32),
        scratch_shapes=[pltpu.VMEM((tm, d), F32)],
        compiler_params=_cparams(("parallel", "arbitrary")),
        name="moe_experts",
    )(x2, hn, cmb, wg, wu, wd, g_final)


def _permute_w_in(w):
    d = w.shape[0]
    o = 0
    seg = {}
    for name, width in (("c", 768), ("gla", 1024), ("ga", GLA_RANK), ("s5u", 256), ("dq", 256), ("dk", 64),
                        ("dv", 64), ("iq", 256), ("ik", IDX_DIM), ("iw", IDX_HEADS), ("gates", 4096)):
        seg[name] = w[:, o:o + width]
        o += width
    z = lambda n: jnp.zeros((d, n), w.dtype)
    cols = [seg["c"], seg["s5u"], seg["gla"], seg["dq"], seg["iq"], seg["dk"], seg["dv"],
            seg["ik"], seg["iw"], z(LANES - IDX_DIM - IDX_HEADS), seg["ga"], z(LANES - GLA_RANK),
            z(P_GATES - P_GA - LANES), seg["gates"]]
    out = jnp.concatenate(cols, axis=1).astype(BF16)
    assert out.shape[1] == P_TOTAL
    return out


def _s5_matrices(bb_re, bb_im, c_re, c_im):
    g, p, c = S5_GROUPS, S5_STATE, S5_GROUP
    n = g * p
    rows_g = jnp.arange(g * c) // c
    cols_g = jnp.arange(n) // p
    mask = (rows_g[:, None] == cols_g[None, :]).astype(F32)
    bm = jnp.concatenate([jnp.tile(bb_re, (g, 1)) * mask, jnp.tile(bb_im, (g, 1)) * mask], axis=1)
    ct = lambda a: jnp.tile(jnp.transpose(a, (0, 2, 1)).reshape(n, c), (1, g)) * mask.T
    cm = jnp.concatenate([ct(c_re), -ct(c_im)], axis=0)
    return bm.astype(BF16), cm.astype(BF16)


def _pick(s, pref):
    for c in pref:
        if s % c == 0:
            return c
    return s


def kernel(x, mem, positions, norm_mix, w_in, conv_w, conv_b, gla_a_up, gla_a_b, gla_norm, s5_lambda_re,
           s5_lambda_im, s5_log_dt, s5_b_re, s5_b_im, s5_c_re, s5_c_im, s5_d, s5_w_glu, s5_b_glu, w_branch,
           w_out, norm_cross, w_cq, w_ckv, w_co, norm_ffn, w_route_group, b_route_group, w_route_expert,
           b_route_expert, w_e_gate, w_e_up, w_e_down, norm_mem, norm_final):
    b, s, d = x.shape
    t = b * s
    m = mem.shape[1]
    depth = w_in.shape[0]
    topk = min(DSA_TOPK, s // 4)
    ts = _pick(s, (512, 256, 128))
    tm = _pick(t, (1024, 512, 256, 128))
    row = lambda a: a.reshape(1, -1)

    tabs = rope_tables(positions, ts=ts)
    x2 = x.reshape(t, d)
    mem2 = mem.reshape(b * m, d)
    for l in range(depth):
        proj = norm_matmul(x2, row(norm_mix[l]), _permute_w_in(w_in[l]), tm=tm, tn=1024)
        proj3 = proj.reshape(b, s, P_TOTAL)
        ya = conv_branch(proj3, conv_w[l], row(conv_b[l]), ts=ts)
        a_up_p = jnp.concatenate(
            [gla_a_up[l], jnp.zeros((LANES - GLA_RANK, gla_a_up.shape[2]), F32)], axis=0).astype(BF16)
        yb = gla_branch(proj3, a_up_p, row(gla_a_b[l]), row(gla_norm[l]), tt=ts)
        bb_re, bb_im, tab = s5_params(s5_lambda_re[l], s5_lambda_im[l], s5_log_dt[l], s5_b_re[l], s5_b_im[l])
        bm, cm = _s5_matrices(bb_re, bb_im, s5_c_re[l], s5_c_im[l])
        yc = s5_branch(proj3, bm, tab, cm, row(s5_d[l]), s5_w_glu[l].astype(BF16), row(s5_b_glu[l]), tt=ts)
        qs, qis, kr, vt4, kir, wt = dsa_prep(proj3, tabs, ts=ts)
        yd = dsa_attend(qs, qis, wt, kr, vt4, kir, qb=_pick(s, (2 * Q_BLOCK, Q_BLOCK)), kt=ts, topk=topk)
        w2 = lambda a: a.reshape(t, a.shape[-1])
        x2 = merge_out(x2, w2(ya), w2(yb), w2(yc), w2(yd), proj, w_branch[l].astype(BF16),
                       w_out[l].astype(BF16), tm=min(tm, 512))
        kv = norm_matmul(mem2, row(norm_mem), w_ckv[l].astype(BF16), tm=_pick(b * m, (1024, 512, 256)), tn=1024)
        x2 = cross_attn(x2.reshape(b, s, d), row(norm_cross[l]), w_cq[l].astype(BF16),
                        kv.reshape(b, m, -1), w_co[l].astype(BF16), tm=ts).reshape(t, d)
        pad = jnp.zeros((d, LANES - N_EXPERTS - N_GROUPS), F32)
        wr = jnp.concatenate([w_route_expert[l], w_route_group[l], pad], axis=1)
        br = jnp.concatenate([b_route_expert[l], b_route_group[l], pad[0]], axis=0).reshape(1, LANES)
        hn, cmb = moe_router(x2, row(norm_ffn[l]), wr, br, tm=min(tm, 512))
        x2 = moe_experts(x2, hn, cmb, w_e_gate[l].astype(BF16), w_e_up[l].astype(BF16),
                         w_e_down[l].astype(BF16), row(norm_final), tm=tm, final_norm=(l == depth - 1))
    return x2.reshape(b, s, d)
```

```python
import functools
import math

import numpy as np
import jax
import jax.numpy as jnp
from jax import lax
from jax.experimental import pallas as pl
from jax.experimental.pallas import tpu as pltpu

F32 = jnp.float32
BF16 = jnp.bfloat16
I32 = jnp.int32

D_MODEL = 1024
DEPTH = 2
EPS = 1e-6
N_BRANCH = 4
BRANCH_W = 256
CONV_W = 3
GLA_HEADS = 4
GLA_DK = 64
GLA_DV = 64
GLA_RANK = 16
GLA_TAU = 16.0
GLA_CHUNK = 64
S5_GROUP = 16
S5_GROUPS = BRANCH_W // S5_GROUP
S5_STATE = 64
DSA_HEADS = 4
DSA_HEAD_DIM = 64
IDX_HEADS = 8
IDX_DIM = 32
DSA_TOPK = 256
Q_BLOCK = 128
ROPE_THETA = 500000.0
ROPE_FRAC = 4
X_HEADS = 4
X_HEAD_DIM = 128
N_GROUPS = 4
EXPERTS_PER_GROUP = 4
N_EXPERTS = N_GROUPS * EXPERTS_PER_GROUP
D_FF_EXPERT = 512

LANES = 128
SUBLANES = 8
VMEM_LIMIT = 48 * 1024 * 1024

P_CONV = 0
P_S5U = 768
P_GLA = 1024
P_DQ = 2048
P_IQ = 2304
P_KV = 2560
P_IKW = 2688
P_GA = 2816
P_GATES = 3072
P_TOTAL = P_GATES + N_BRANCH * D_MODEL

INT_MIN = -2147483648
VT_ROWS = DSA_HEAD_DIM + 2 * SUBLANES


def _cparams(sem):
    return pltpu.CompilerParams(dimension_semantics=sem, vmem_limit_bytes=VMEM_LIMIT)


def _dot(a, b):
    return jnp.dot(a, b, preferred_element_type=F32)


def _dot_nt(a, b):
    return lax.dot_general(a, b, (((1,), (1,)), ((), ())), preferred_element_type=F32)


def _dot_tn(a, b):
    return lax.dot_general(a, b, (((0,), (0,)), ((), ())), preferred_element_type=F32)


def _split_dot(exact_bf16, x):
    hi = x.astype(BF16)
    lo = (x - hi.astype(F32)).astype(BF16)
    return _dot(exact_bf16, hi) + _dot(exact_bf16, lo)


def _split_dot_r(x, exact_bf16):
    hi = x.astype(BF16)
    lo = (x - hi.astype(F32)).astype(BF16)
    return _dot(hi, exact_bf16) + _dot(lo, exact_bf16)


def _rms(x, g):
    return x * lax.rsqrt(jnp.mean(x * x, axis=-1, keepdims=True) + EPS) * g


def _sigmoid(x):
    return 1.0 / (1.0 + jnp.exp(-x))


def _norm_matmul_kernel(x_ref, g_ref, w_ref, o_ref, hn_ref):
    @pl.when(pl.program_id(1) == 0)
    def _():
        hn_ref[...] = _rms(x_ref[...], g_ref[...]).astype(BF16)

    o_ref[...] = _dot(hn_ref[...], w_ref[...]).astype(o_ref.dtype)


def norm_matmul(x, g, w, *, tm, tn, out_dtype=BF16):
    t, d = x.shape
    n = w.shape[1]
    return pl.pallas_call(
        _norm_matmul_kernel,
        grid=(t // tm, n // tn),
        in_specs=[
            pl.BlockSpec((tm, d), lambda i, j: (i, 0)),
            pl.BlockSpec((1, d), lambda i, j: (0, 0)),
            pl.BlockSpec((d, tn), lambda i, j: (0, j)),
        ],
        out_specs=pl.BlockSpec((tm, tn), lambda i, j: (i, j)),
        out_shape=jax.ShapeDtypeStruct((t, n), out_dtype),
        scratch_shapes=[pltpu.VMEM((tm, d), BF16)],
        compiler_params=_cparams(("parallel", "arbitrary")),
        name="norm_matmul",
    )(x, g, w)


def _conv_kernel(cur_ref, prev_ref, w_ref, b_ref, o_ref):
    i = pl.program_id(1)
    w_ = BRANCH_W
    cur = cur_ref[0].astype(F32)
    u = cur[:, 2 * w_:3 * w_] * cur[:, 0:w_]
    pv = prev_ref[0].astype(F32)
    pu = pv[:, 2 * w_:3 * w_] * pv[:, 0:w_]
    pu = jnp.where(i > 0, pu, 0.0)
    row = lax.broadcasted_iota(I32, u.shape, 0)
    u1 = jnp.where(row == 0, pu[7:8], pltpu.roll(u, 1, 0))
    u2 = jnp.where(row == 0, pu[6:7], jnp.where(row == 1, pu[7:8], pltpu.roll(u, 2, 0)))
    w = w_ref[...]
    y = w[0:1] * u2 + w[1:2] * u1 + w[2:3] * u + b_ref[...]
    o_ref[0] = (cur[:, w_:2 * w_] * y).astype(o_ref.dtype)


def conv_branch(proj3, conv_w, conv_b, *, ts):
    b, s, _ = proj3.shape
    wc = 3 * BRANCH_W
    hb = ts // SUBLANES
    return pl.pallas_call(
        _conv_kernel,
        grid=(b, s // ts),
        in_specs=[
            pl.BlockSpec((1, ts, wc), lambda bi, i: (bi, i, P_CONV // wc)),
            pl.BlockSpec((1, SUBLANES, wc), lambda bi, i: (bi, jnp.maximum(i * hb - 1, 0), P_CONV // wc)),
            pl.BlockSpec((CONV_W, BRANCH_W), lambda bi, i: (0, 0)),
            pl.BlockSpec((1, BRANCH_W), lambda bi, i: (0, 0)),
        ],
        out_specs=pl.BlockSpec((1, ts, BRANCH_W), lambda bi, i: (bi, i, 0)),
        out_shape=jax.ShapeDtypeStruct((b, s, BRANCH_W), BF16),
        compiler_params=_cparams(("parallel", "arbitrary")),
        name="conv_branch",
    )(proj3, proj3, conv_w, conv_b)


def _gla_kernel(g_ref, a_ref, aup_ref, ab_ref, gn_ref, ltb_ref, ltf_ref, bob_ref, bd_ref, bdb_ref,
                o_ref, st_ref, *, tt):
    hw = GLA_HEADS * GLA_DK
    c = GLA_CHUNK

    @pl.when(pl.program_id(1) == 0)
    def _():
        st_ref[...] = jnp.zeros_like(st_ref)

    blk = g_ref[0]
    q = blk[:, 0:hw].astype(F32)
    k = blk[:, hw:2 * hw].astype(F32)
    vb = blk[:, 2 * hw:3 * hw]
    v = vb.astype(F32)
    r = blk[:, 3 * hw:4 * hw].astype(F32)

    pre = _dot(a_ref[0], aup_ref[...]) + ab_ref[...]
    la = (jnp.minimum(pre, 0.0) - jnp.log(1.0 + jnp.exp(-jnp.abs(pre)))) * (1.0 / GLA_TAU)
    cum = _split_dot(ltb_ref[...], la)
    tot = _split_dot(bob_ref[...], la)
    q_dec = q * (GLA_DK ** -0.5) * jnp.exp(cum)
    k_inv = (k * jnp.exp(-cum)).astype(BF16)
    k_end = (k * jnp.exp(tot - cum)).astype(BF16)
    qdb = q_dec.astype(BF16)

    lane = lax.broadcasted_iota(I32, (1, hw), 1)
    ltmask = ltf_ref[...] > 0.0
    o = jnp.zeros((tt, hw), F32)
    for h in range(GLA_HEADS):
        hm = (lane // GLA_DK) == h
        qh = jnp.where(hm, q_dec, 0.0).astype(BF16)
        att = jnp.where(ltmask, _dot_nt(qh, k_inv), 0.0)
        vh = jnp.where(hm, v, 0.0).astype(BF16)
        o = o + _dot(att.astype(BF16), vh)

    st = st_ref[...]
    bd = bd_ref[...]
    inter = []
    for n in range(tt // c):
        rows = slice(n * c, (n + 1) * c)
        inter.append(_dot_nt(qdb[rows], st.astype(BF16)))
        dec = jnp.exp(tot[n * c:n * c + 1, :])
        st = st * dec + _dot_tn(vb[rows], k_end[rows]) * bd
    st_ref[...] = st
    o = o + jnp.concatenate(inter, axis=0)

    msq = _split_dot_r(o * o, bdb_ref[...]) * (1.0 / GLA_DV)
    y = o * lax.rsqrt(msq + EPS) * gn_ref[...] * (r * _sigmoid(r))
    o_ref[0] = y.astype(o_ref.dtype)


def _gla_consts(tt):
    ri = np.arange(tt)[:, None]
    ci = np.arange(tt)[None, :]
    same = (ri // GLA_CHUNK) == (ci // GLA_CHUNK)
    lt = (same & (ci <= ri)).astype(np.float32)
    hw = GLA_HEADS * GLA_DK
    hi = np.arange(hw)
    bd = ((hi[:, None] // GLA_DK) == (hi[None, :] // GLA_DK)).astype(np.float32)
    return (jnp.asarray(lt, BF16), jnp.asarray(lt, F32), jnp.asarray(same.astype(np.float32), BF16),
            jnp.asarray(bd, F32), jnp.asarray(bd, BF16))


def gla_branch(proj3, a_up_p, a_b, g_norm, *, tt):
    b, s, _ = proj3.shape
    hw = GLA_HEADS * GLA_DK
    ltb, ltf, bob, bd, bdb = _gla_consts(tt)
    const = lambda shape: pl.BlockSpec(shape, lambda bi, i: (0,) * len(shape))
    return pl.pallas_call(
        functools.partial(_gla_kernel, tt=tt),
        grid=(b, s // tt),
        in_specs=[
            pl.BlockSpec((1, tt, 4 * hw), lambda bi, i: (bi, i, P_GLA // (4 * hw))),
            pl.BlockSpec((1, tt, LANES), lambda bi, i: (bi, i, P_GA // LANES)),
            const((LANES, hw)), const((1, hw)), const((1, hw)),
            const((tt, tt)), const((tt, tt)), const((tt, tt)), const((hw, hw)), const((hw, hw)),
        ],
        out_specs=pl.BlockSpec((1, tt, hw), lambda bi, i: (bi, i, 0)),
        out_shape=jax.ShapeDtypeStruct((b, s, hw), BF16),
        scratch_shapes=[pltpu.VMEM((hw, hw), F32)],
        compiler_params=_cparams(("parallel", "arbitrary")),
        name="gla_branch",
    )(proj3, proj3, a_up_p, a_b, g_norm, ltb, ltf, bob, bd, bdb)


def _s5_param_kernel(lre_ref, lim_ref, ldt_ref, bre_ref, bim_ref, bbre_ref, bbim_ref, tab_ref):
    lre = lre_ref[...]
    lim = lim_ref[...]
    dt = jnp.exp(ldt_ref[...])
    mag = jnp.exp(lre * dt)
    lbr = mag * jnp.cos(lim * dt)
    lbi = mag * jnp.sin(lim * dt)
    den = lre * lre + lim * lim
    fre = ((lbr - 1.0) * lre + lbi * lim) / den
    fim = (lbi * lre - (lbr - 1.0) * lim) / den
    bre = bre_ref[...]
    bim = bim_ref[...]
    bbre_ref[...] = fre * bre - fim * bim
    bbim_ref[...] = fre * bim + fim * bre

    pw = [None, (lbr, lbi)]
    for _ in range(2, SUBLANES + 1):
        pr, pi = pw[-1]
        pw.append((pr * lbr - pi * lbi, pr * lbi + pi * lbr))
    n = lre.shape[-1]
    row = lax.broadcasted_iota(I32, (SUBLANES, n), 0)
    zero = jnp.zeros((SUBLANES, n), F32)
    for idx, sft in enumerate((1, 2, 4)):
        tab_ref[2 * idx] = jnp.where(row >= sft, jnp.broadcast_to(pw[sft][0], (SUBLANES, n)), zero)
        tab_ref[2 * idx + 1] = jnp.where(row >= sft, jnp.broadcast_to(pw[sft][1], (SUBLANES, n)), zero)
    cr, ci = zero, zero
    for rr in range(SUBLANES):
        cr = jnp.where(row == rr, jnp.broadcast_to(pw[rr + 1][0], (SUBLANES, n)), cr)
        ci = jnp.where(row == rr, jnp.broadcast_to(pw[rr + 1][1], (SUBLANES, n)), ci)
    tab_ref[6] = cr
    tab_ref[7] = ci


def s5_params(lam_re, lam_im, log_dt, b_re, b_im):
    g, p = lam_re.shape
    n = g * p
    row = lambda a: a.reshape(1, n)
    ldt = jnp.broadcast_to(log_dt[:, None], (g, p))
    bt = lambda a: jnp.transpose(a, (2, 0, 1)).reshape(S5_GROUP, n)
    full = lambda shape: pl.BlockSpec(shape, lambda: (0,) * len(shape))
    return pl.pallas_call(
        _s5_param_kernel,
        in_specs=[full((1, n))] * 3 + [full((S5_GROUP, n))] * 2,
        out_specs=[full((S5_GROUP, n)), full((S5_GROUP, n)), full((8, SUBLANES, n))],
        out_shape=[jax.ShapeDtypeStruct((S5_GROUP, n), F32), jax.ShapeDtypeStruct((S5_GROUP, n), F32),
                   jax.ShapeDtypeStruct((8, SUBLANES, n), F32)],
        name="s5_params",
    )(row(lam_re), row(lam_im), row(ldt), bt(b_re), bt(b_im))


def _s5_kernel(u_ref, bm_ref, tab_ref, cm_ref, d_ref, wg_ref, bg_ref, o_ref, xs_ref, car_ref, *, tt, n):
    @pl.when(pl.program_id(1) == 0)
    def _():
        car_ref[...] = jnp.zeros_like(car_ref)

    ub = u_ref[0]
    xs_ref[...] = _dot(ub, bm_ref[...])

    def group(gi, carry):
        r0 = pl.multiple_of(gi * SUBLANES, SUBLANES)
        for j in range(n // LANES):
            cre = slice(j * LANES, (j + 1) * LANES)
            cim = slice(n + j * LANES, n + (j + 1) * LANES)
            re = xs_ref[pl.ds(r0, SUBLANES), cre]
            im = xs_ref[pl.ds(r0, SUBLANES), cim]
            for idx, sft in enumerate((1, 2, 4)):
                ar = tab_ref[2 * idx, :, cre]
                ai = tab_ref[2 * idx + 1, :, cre]
                sr = pltpu.roll(re, sft, 0)
                si = pltpu.roll(im, sft, 0)
                re, im = re + ar * sr - ai * si, im + ar * si + ai * sr
            pr = tab_ref[6, :, cre]
            pi = tab_ref[7, :, cre]
            cr = car_ref[0, :, cre]
            ci = car_ref[1, :, cre]
            re, im = re + pr * cr - pi * ci, im + pr * ci + pi * cr
            xs_ref[pl.ds(r0, SUBLANES), cre] = re
            xs_ref[pl.ds(r0, SUBLANES), cim] = im
            car_ref[0, :, cre] = jnp.broadcast_to(re[SUBLANES - 1:SUBLANES], (SUBLANES, LANES))
            car_ref[1, :, cre] = jnp.broadcast_to(im[SUBLANES - 1:SUBLANES], (SUBLANES, LANES))
        return carry

    lax.fori_loop(0, tt // SUBLANES, group, 0)

    y = _dot(xs_ref[...].astype(BF16), cm_ref[...]) + d_ref[...] * ub.astype(F32)
    y = 0.5 * y * (1.0 + jnp.tanh(math.sqrt(2.0 / math.pi) * (y + 0.044715 * (y * y * y))))
    z = _dot(y.astype(BF16), wg_ref[...]) + bg_ref[...]
    o_ref[0] = (y * _sigmoid(z)).astype(o_ref.dtype)


def s5_branch(proj3, bmat, tab, cmat, d_skip, w_glu, b_glu, *, tt):
    b, s, _ = proj3.shape
    w_ = BRANCH_W
    n = S5_GROUPS * S5_STATE
    const = lambda shape: pl.BlockSpec(shape, lambda bi, i: (0,) * len(shape))
    return pl.pallas_call(
        functools.partial(_s5_kernel, tt=tt, n=n),
        grid=(b, s // tt),
        in_specs=[
            pl.BlockSpec((1, tt, w_), lambda bi, i: (bi, i, P_S5U // w_)),
            const((w_, 2 * n)), const((8, SUBLANES, n)), const((2 * n, w_)),
            const((1, w_)), const((w_, w_)), const((1, w_)),
        ],
        out_specs=pl.BlockSpec((1, tt, w_), lambda bi, i: (bi, i, 0)),
        out_shape=jax.ShapeDtypeStruct((b, s, w_), BF16),
        scratch_shapes=[pltpu.VMEM((tt, 2 * n), F32), pltpu.VMEM((2, SUBLANES, n), F32)],
        compiler_params=_cparams(("parallel", "arbitrary")),
        name="s5_branch",
    )(proj3, bmat, tab, cmat, d_skip, w_glu, b_glu)


def _rope_freq_rows():
    rows = np.zeros((8, LANES), np.float32)
    for pat, dh in enumerate((DSA_HEAD_DIM, IDX_DIM)):
        rd = dh // ROPE_FRAC
        half = rd // 2
        inv = (np.float32(ROPE_THETA) ** (-np.arange(half, dtype=np.float32) * np.float32(2.0 / rd))).astype(np.float32)
        for l in range(LANES):
            i = l % dh
            if i < half:
                rows[3 * pat, l] = inv[i]
                rows[3 * pat + 1, l] = -1.0
            elif i < rd:
                rows[3 * pat, l] = inv[i - half]
                rows[3 * pat + 2, l] = 1.0
    return rows


def _rope_tab_kernel(pos_ref, fr_ref, o_ref):
    pos = pos_ref[0]
    fr = fr_ref[...]
    for pat in range(2):
        ang = pos * fr[3 * pat:3 * pat + 1]
        c = jnp.cos(ang)
        s = jnp.sin(ang)
        o_ref[0, 3 * pat] = c
        o_ref[0, 3 * pat + 1] = s * fr[3 * pat + 1:3 * pat + 2]
        o_ref[0, 3 * pat + 2] = s * fr[3 * pat + 2:3 * pat + 3]


def rope_tables(positions, *, ts):
    b, s = positions.shape
    pos = positions.astype(F32).reshape(b, s, 1)
    fr = jnp.asarray(_rope_freq_rows())
    return pl.pallas_call(
        _rope_tab_kernel,
        grid=(b, s // ts),
        in_specs=[pl.BlockSpec((1, ts, 1), lambda bi, i: (bi, i, 0)),
                  pl.BlockSpec((8, LANES), lambda bi, i: (0, 0))],
        out_specs=pl.BlockSpec((1, 6, ts, LANES), lambda bi, i: (bi, 0, i, 0)),
        out_shape=jax.ShapeDtypeStruct((b, 6, s, LANES), F32),
        compiler_params=_cparams(("parallel", "arbitrary")),
        name="rope_tables",
    )(pos, fr)


def _rope(t, c, sm, sp, half):
    n = t.shape[-1]
    return t * c + sm * pltpu.roll(t, n - half, 1) + sp * pltpu.roll(t, half, 1)


def _dsa_prep_kernel(dq_ref, iq_ref, kv_ref, ikw_ref, tab_ref,
                     qs_ref, qis_ref, kr_ref, vt_ref, kir_ref, wt_ref):
    two = lambda a: jnp.concatenate([a, a], axis=1)
    c1, sm1, sp1 = tab_ref[0, 0], tab_ref[0, 1], tab_ref[0, 2]
    c2, sm2, sp2 = tab_ref[0, 3], tab_ref[0, 4], tab_ref[0, 5]
    h1 = DSA_HEAD_DIM // ROPE_FRAC // 2
    h2 = IDX_DIM // ROPE_FRAC // 2
    lane = lax.broadcasted_iota(I32, (1, LANES), 1)

    q = _rope(dq_ref[0].astype(F32), two(c1), two(sm1), two(sp1), h1) * (DSA_HEAD_DIM ** -0.5 * math.log2(math.e))
    for h in range(DSA_HEADS):
        qs_ref[0, h] = q[:, h * DSA_HEAD_DIM:(h + 1) * DSA_HEAD_DIM].astype(BF16)
    qi = _rope(iq_ref[0].astype(F32), two(c2), two(sm2), two(sp2), h2)
    for h in range(IDX_HEADS):
        qis_ref[0, h] = qi[:, h * IDX_DIM:(h + 1) * IDX_DIM].astype(BF16)

    kv = kv_ref[0].astype(F32)
    isk = lane < DSA_HEAD_DIM
    kvr = _rope(kv, jnp.where(isk, c1, 1.0), jnp.where(isk, sm1, 0.0), jnp.where(isk, sp1, 0.0), h1)
    kr_ref[0] = kvr[:, 0:DSA_HEAD_DIM].astype(BF16)
    ones = jnp.ones((VT_ROWS - DSA_HEAD_DIM, kv.shape[0]), BF16)
    vt_ref[0, 0] = jnp.concatenate([kvr.T[DSA_HEAD_DIM:2 * DSA_HEAD_DIM].astype(BF16), ones], axis=0)

    ikw = ikw_ref[0].astype(F32)
    isi = lane < IDX_DIM
    ikr = _rope(ikw, jnp.where(isi, c2, 1.0), jnp.where(isi, sm2, 0.0), jnp.where(isi, sp2, 0.0), h2)
    kir_ref[0] = ikr[:, 0:IDX_DIM].astype(BF16)
    wt_ref[0] = ikr.T[IDX_DIM:IDX_DIM + IDX_HEADS] * ((IDX_HEADS ** -0.5) * (IDX_DIM ** -0.5))


def dsa_prep(proj3, tabs, *, ts):
    b, s, _ = proj3.shape
    qw = DSA_HEADS * DSA_HEAD_DIM
    iw = IDX_HEADS * IDX_DIM
    return pl.pallas_call(
        _dsa_prep_kernel,
        grid=(b, s // ts),
        in_specs=[
            pl.BlockSpec((1, ts, qw), lambda bi, i: (bi, i, P_DQ // qw)),
            pl.BlockSpec((1, ts, iw), lambda bi, i: (bi, i, P_IQ // iw)),
            pl.BlockSpec((1, ts, LANES), lambda bi, i: (bi, i, P_KV // LANES)),
            pl.BlockSpec((1, ts, LANES), lambda bi, i: (bi, i, P_IKW // LANES)),
            pl.BlockSpec((1, 6, ts, LANES), lambda bi, i: (bi, 0, i, 0)),
        ],
        out_specs=[
            pl.BlockSpec((1, DSA_HEADS, ts, DSA_HEAD_DIM), lambda bi, i: (bi, 0, i, 0)),
            pl.BlockSpec((1, IDX_HEADS, ts, IDX_DIM), lambda bi, i: (bi, 0, i, 0)),
            pl.BlockSpec((1, ts, DSA_HEAD_DIM), lambda bi, i: (bi, i, 0)),
            pl.BlockSpec((1, 1, VT_ROWS, ts), lambda bi, i: (bi, i, 0, 0)),
            pl.BlockSpec((1, ts, IDX_DIM), lambda bi, i: (bi, i, 0)),
            pl.BlockSpec((1, IDX_HEADS, ts), lambda bi, i: (bi, 0, i)),
        ],
        out_shape=[
            jax.ShapeDtypeStruct((b, DSA_HEADS, s, DSA_HEAD_DIM), BF16),
            jax.ShapeDtypeStruct((b, IDX_HEADS, s, IDX_DIM), BF16),
            jax.ShapeDtypeStruct((b, s, DSA_HEAD_DIM), BF16),
            jax.ShapeDtypeStruct((b, s // ts, VT_ROWS, ts), BF16),
            jax.ShapeDtypeStruct((b, s, IDX_DIM), BF16),
            jax.ShapeDtypeStruct((b, IDX_HEADS, s), F32),
        ],
        compiler_params=_cparams(("parallel", "arbitrary")),
        name="dsa_prep",
    )(proj3, proj3, proj3, proj3, tabs)


def _dsa_kernel(q_ref, qi_ref, w_ref, k_ref, vt_ref, ki_ref, lts_ref, o_ref, sc_ref, lg_ref, *, qb, kt, topk):
    i = pl.program_id(1)
    q0 = i * qb
    nkt = (q0 + qb + kt - 1) // kt
    kf = float(topk)
    sg = 8 * SUBLANES
    wrow = w_ref[0]
    tq = q0 + lax.broadcasted_iota(I32, (1, qb), 1)

    def to_key(x):
        bits = pltpu.bitcast(x, I32)
        return bits ^ ((bits >> 31) & 0x7FFFFFFF)

    def scores(t, gmax):
        k0 = pl.multiple_of(t * kt, kt)
        kit = ki_ref[0, pl.ds(k0, kt), :]
        acc = jnp.zeros((kt, qb), F32)
        for h in range(IDX_HEADS):
            acc = acc + jnp.maximum(_dot_nt(kit, qi_ref[0, h]), 0.0) * wrow[h:h + 1]
        acc = jnp.where(acc == 0.0, 0.0, acc)
        srow = k0 + lax.broadcasted_iota(I32, (kt, qb), 0)
        causal = srow <= tq
        sc_ref[t] = jnp.where(causal, to_key(acc), INT_MIN)
        accm = jnp.where(causal, acc, -jnp.inf)
        return jnp.maximum(gmax, jnp.max(accm.reshape(kt // topk, topk, qb), axis=0))

    gmax = lax.fori_loop(0, nkt, scores, jnp.full((topk, qb), -jnp.inf, F32))

    def count(pred):
        def body(t, c):
            m = jnp.where(pred(sc_ref[t]), 1.0, 0.0)
            return c + jnp.sum(m.reshape(kt // sg, sg, qb), axis=0)
        c = lax.fori_loop(0, nkt, body, jnp.zeros((sg, qb), F32))
        return jnp.sum(c, axis=0, keepdims=True)

    need = (tq + 1) > topk
    lo0 = to_key(jnp.min(gmax, axis=0, keepdims=True))
    hi0 = to_key(jnp.max(gmax, axis=0, keepdims=True))
    done0 = jnp.where(jnp.logical_and(need, lo0 < hi0), 0.0, 1.0)

    def bs_cond(c):
        it, _, _, _, done = c
        return jnp.logical_and(it < 34, jnp.sum(1.0 - done) > 0.0)

    def bs_step(lo, hi, thr, done, probe=None):
        mid = (lo >> 1) + (hi >> 1) + (((lo & 1) + (hi & 1) + 1) >> 1)
        if probe is not None:
            mid = jnp.where(jnp.logical_and(lo < probe, probe <= hi), probe, mid)
        cnt = count(lambda kk: kk >= mid)
        ge = cnt >= kf
        lo = jnp.where(ge, mid, lo)
        hi = jnp.where(ge, hi, mid - 1)
        fin = jnp.where(cnt == kf, 1.0, jnp.where(lo == hi, 1.0, 0.0))
        val = jnp.where(cnt == kf, mid, lo)
        thr = jnp.where(done == 0.0, val, thr)
        return lo, hi, thr, jnp.maximum(done, fin)

    def bs_body(c):
        it, lo, hi, thr, done = c
        lo, hi, thr, done = bs_step(*bs_step(lo, hi, thr, done))
        return it + 2, lo, hi, thr, done

    st = bs_step(*bs_step(lo0, hi0, lo0, done0, probe=1), probe=0)
    _, _, _, thr, _ = lax.while_loop(bs_cond, bs_body, (jnp.int32(2),) + st)
    thr = jnp.where(need, thr, INT_MIN + 1)

    cge = count(lambda kk: kk >= thr)
    tie = jnp.logical_and(need, cge > kf)

    @pl.when(jnp.sum(jnp.where(tie, 1.0, 0.0)) > 0.0)
    def _():
        cgt = count(lambda kk: kk > thr)
        room = kf - cgt

        def body(t, seen):
            kk = sc_ref[t]
            eq = jnp.logical_and(kk == thr, tie)
            eqf = jnp.where(eq, 1.0, 0.0)
            before = _dot(lts_ref[...], eqf.astype(BF16)) + seen
            sc_ref[t] = jnp.where(jnp.logical_and(eq, before >= room), INT_MIN, kk)
            return seen + jnp.sum(eqf, axis=0, keepdims=True)

        lax.fori_loop(0, nkt, body, jnp.zeros((1, qb), F32))

    nh = DSA_HEADS

    def logits(t, c):
        k0 = pl.multiple_of(t * kt, kt)
        kt_ = k_ref[0, pl.ds(k0, kt), :]
        bias = jnp.where(sc_ref[t] >= thr, 0.0, -jnp.inf)
        out = []
        for h in range(nh):
            lg = _dot_nt(kt_, q_ref[0, h]) + bias
            lg_ref[t, h] = lg
            out.append(jnp.maximum(c[h], jnp.max(lg.reshape(kt // sg, sg, qb), axis=0)))
        return tuple(out)

    mparts = lax.fori_loop(0, nkt, logits, (jnp.full((sg, qb), -jnp.inf, F32),) * nh)
    ms = []
    for h in range(nh):
        m = jnp.max(mparts[h], axis=0, keepdims=True)
        ms.append(jnp.where(m == -jnp.inf, 0.0, m))

    def attend(t, c):
        vt = vt_ref[0, t]
        out = []
        for h in range(nh):
            p = jnp.exp2((lg_ref[t, h] - ms[h]).astype(BF16))
            out.append(c[h] + _dot(vt, p))
        return tuple(out)

    res = lax.fori_loop(0, nkt, attend, (jnp.zeros((vt_ref.shape[2], qb), F32),) * nh)
    dh = DSA_HEAD_DIM
    for h in range(nh):
        ot = res[h][0:dh] / res[h][dh:dh + 1]
        o_ref[0, :, h * dh:(h + 1) * dh] = ot.T.astype(o_ref.dtype)


def dsa_attend(qs, qis, wt, kr, vt4, kir, *, qb, kt, topk):
    b, nh, s, dh = qs.shape
    nkt = s // kt
    ri = np.arange(kt)
    lts = jnp.asarray((ri[None, :] < ri[:, None]).astype(np.float32), BF16)
    return pl.pallas_call(
        functools.partial(_dsa_kernel, qb=qb, kt=kt, topk=topk),
        grid=(b, s // qb),
        in_specs=[
            pl.BlockSpec((1, nh, qb, dh), lambda bi, i: (bi, 0, i, 0)),
            pl.BlockSpec((1, IDX_HEADS, qb, IDX_DIM), lambda bi, i: (bi, 0, i, 0)),
            pl.BlockSpec((1, IDX_HEADS, qb), lambda bi, i: (bi, 0, i)),
            pl.BlockSpec((1, s, dh), lambda bi, i: (bi, 0, 0)),
            pl.BlockSpec((1, nkt, VT_ROWS, kt), lambda bi, i: (bi, 0, 0, 0)),
            pl.BlockSpec((1, s, IDX_DIM), lambda bi, i: (bi, 0, 0)),
            pl.BlockSpec((kt, kt), lambda bi, i: (0, 0)),
        ],
        out_specs=pl.BlockSpec((1, qb, nh * dh), lambda bi, i: (bi, i, 0)),
        out_shape=jax.ShapeDtypeStruct((b, s, nh * dh), BF16),
        scratch_shapes=[pltpu.VMEM((nkt, kt, qb), I32), pltpu.VMEM((nkt, nh, kt, qb), F32)],
        compiler_params=_cparams(("parallel", "arbitrary")),
        name="dsa_attend",
    )(qs, qis, wt, kr, vt4, kir, lts)


def _merge_kernel(x_ref, ya_ref, yb_ref, yc_ref, yd_ref, g0_ref, g1_ref, g2_ref, g3_ref, wb_ref, wo_ref, o_ref):
    ys = (ya_ref, yb_ref, yc_ref, yd_ref)
    gs = (g0_ref, g1_ref, g2_ref, g3_ref)
    merged = jnp.zeros(o_ref.shape, F32)
    for i in range(N_BRANCH):
        merged = merged + _sigmoid(gs[i][...].astype(F32)) * _dot(ys[i][...], wb_ref[i])
    o_ref[...] = x_ref[...] + _dot(merged.astype(BF16), wo_ref[...])


def merge_out(x2, ya, yb, yc, yd, proj, w_branch, w_out, *, tm):
    t, d = x2.shape
    w_ = BRANCH_W
    yspec = pl.BlockSpec((tm, w_), lambda i: (i, 0))
    gspec = lambda n: pl.BlockSpec((tm, d), lambda i: (i, P_GATES // d + n))
    return pl.pallas_call(
        _merge_kernel,
        grid=(t // tm,),
        in_specs=[pl.BlockSpec((tm, d), lambda i: (i, 0)), yspec, yspec, yspec, yspec,
                  gspec(0), gspec(1), gspec(2), gspec(3),
                  pl.BlockSpec((N_BRANCH, w_, d), lambda i: (0, 0, 0)),
                  pl.BlockSpec((d, d), lambda i: (0, 0))],
        out_specs=pl.BlockSpec((tm, d), lambda i: (i, 0)),
        out_shape=jax.ShapeDtypeStruct((t, d), F32),
        compiler_params=_cparams(("parallel",)),
        name="merge_out",
    )(x2, ya, yb, yc, yd, proj, proj, proj, proj, w_branch, w_out)


def _cross_kernel(x_ref, g_ref, wq_ref, kv_ref, wo_ref, o_ref):
    x = x_ref[0]
    hn = _rms(x, g_ref[...]).astype(BF16)
    q = _dot(hn, wq_ref[...]) * (X_HEAD_DIM ** -0.5)
    kv = kv_ref[0]
    hd = X_HEADS * X_HEAD_DIM
    outs = []
    for h in range(X_HEADS):
        cs = slice(h * X_HEAD_DIM, (h + 1) * X_HEAD_DIM)
        lg = _dot_nt(q[:, cs].astype(BF16), kv[:, cs])
        p = jnp.exp(lg - jnp.max(lg, axis=-1, keepdims=True))
        p = p / jnp.sum(p, axis=-1, keepdims=True)
        outs.append(_dot(p.astype(BF16), kv[:, hd + h * X_HEAD_DIM:hd + (h + 1) * X_HEAD_DIM]))
    o = jnp.concatenate(outs, axis=1).astype(BF16)
    o_ref[0] = x + _dot(o, wo_ref[...])


def cross_attn(x3, g, w_q, kv, w_o, *, tm):
    b, s, d = x3.shape
    m = kv.shape[1]
    hd = X_HEADS * X_HEAD_DIM
    return pl.pallas_call(
        _cross_kernel,
        grid=(b, s // tm),
        in_specs=[pl.BlockSpec((1, tm, d), lambda bi, i: (bi, i, 0)),
                  pl.BlockSpec((1, d), lambda bi, i: (0, 0)),
                  pl.BlockSpec((d, hd), lambda bi, i: (0, 0)),
                  pl.BlockSpec((1, m, 2 * hd), lambda bi, i: (bi, 0, 0)),
                  pl.BlockSpec((hd, d), lambda bi, i: (0, 0))],
        out_specs=pl.BlockSpec((1, tm, d), lambda bi, i: (bi, i, 0)),
        out_shape=jax.ShapeDtypeStruct((b, s, d), F32),
        compiler_params=_cparams(("parallel", "arbitrary")),
        name="cross_attn",
    )(x3, g, w_q, kv, w_o)


def _router_kernel(x_ref, g_ref, wr_ref, br_ref, hn_ref, cmb_ref):
    hn = _rms(x_ref[...], g_ref[...])
    hn_ref[...] = hn.astype(BF16)
    lg = jnp.dot(hn, wr_ref[...], precision=lax.Precision.HIGHEST, preferred_element_type=F32) + br_ref[...]
    lane = lax.broadcasted_iota(I32, lg.shape, 1)
    lanef = lane.astype(F32)
    ninf = -jnp.inf
    big = float(LANES)
    first = lambda mask: jnp.min(jnp.where(mask, lanef, big), axis=-1, keepdims=True)

    isg = jnp.logical_and(lane >= N_EXPERTS, lane < N_EXPERTS + N_GROUPS)
    gl = jnp.where(isg, lg, ninf)
    gmax = jnp.max(gl, axis=-1, keepdims=True)
    gsel = first(gl == gmax) - float(N_EXPERTS)
    pg = 1.0 / jnp.sum(jnp.exp(gl - gmax), axis=-1, keepdims=True)

    ise = jnp.floor(lanef * (1.0 / EXPERTS_PER_GROUP)) == gsel
    el = jnp.where(ise, lg, ninf)
    e1 = jnp.max(el, axis=-1, keepdims=True)
    i1 = first(el == e1)
    el2 = jnp.where(lanef == i1, ninf, el)
    e2 = jnp.max(el2, axis=-1, keepdims=True)
    i2 = first(el2 == e2)
    d = jnp.exp(e2 - e1)
    w1 = 1.0 / (1.0 + d)
    w2 = d / (1.0 + d)
    cmb = jnp.where(lanef == i1, pg * w1, jnp.where(lanef == i2, pg * w2, 0.0))
    cmb_ref[...] = jnp.where(lane == N_EXPERTS, gsel, cmb)


def moe_router(x2, g, wr, br, *, tm):
    t, d = x2.shape
    return pl.pallas_call(
        _router_kernel,
        grid=(t // tm,),
        in_specs=[pl.BlockSpec((tm, d), lambda i: (i, 0)), pl.BlockSpec((1, d), lambda i: (0, 0)),
                  pl.BlockSpec((d, LANES), lambda i: (0, 0)), pl.BlockSpec((1, LANES), lambda i: (0, 0))],
        out_specs=[pl.BlockSpec((tm, d), lambda i: (i, 0)), pl.BlockSpec((tm, LANES), lambda i: (i, 0))],
        out_shape=[jax.ShapeDtypeStruct((t, d), BF16), jax.ShapeDtypeStruct((t, LANES), F32)],
        compiler_params=_cparams(("parallel",)),
        name="moe_router",
    )(x2, g, wr, br)


MOE_PAD = 2 * SUBLANES
MOE_CH = 288
MOE_SB = 256


def _moe_rmax(tm):
    return -(-(tm + N_GROUPS * MOE_PAD + MOE_CH) // LANES) * LANES


def _moe_sort_kernel(hn_ref, cmb_ref, lts_ref, hs_ref, cs_ref, meta_ref, seg_ref, *, tm, rmax):
    cmb = cmb_ref[...]
    lanef = lax.broadcasted_iota(I32, (tm, LANES), 1).astype(F32)
    oh = jnp.where(lanef == cmb[:, N_EXPERTS:N_EXPERTS + 1], 1.0, 0.0)
    seen = jnp.zeros((1, LANES), F32)
    pres = []
    for sblk in range(tm // MOE_SB):
        ohs = oh[sblk * MOE_SB:(sblk + 1) * MOE_SB]
        pres.append(_dot(lts_ref[...], ohs.astype(BF16)) + seen)
        seen = seen + jnp.sum(ohs, axis=0, keepdims=True)
    pre = jnp.concatenate(pres, axis=0)
    cnt = jnp.broadcast_to(seen, (SUBLANES, LANES))
    padn = jnp.ceil(cnt * (1.0 / MOE_PAD)) * MOE_PAD
    lane8 = lax.broadcasted_iota(I32, (SUBLANES, LANES), 1)
    off = jnp.zeros((SUBLANES, LANES), F32)
    for k in range(1, N_GROUPS):
        off = off + jnp.where(lane8 >= k, pltpu.roll(padn, k, 1), 0.0)
    dest = jnp.sum(oh * (off[0:1] + pre), axis=-1, keepdims=True)
    destb = jnp.broadcast_to(dest, (tm, LANES))
    meta_ref[...] = destb
    row8 = lax.broadcasted_iota(I32, (SUBLANES, LANES), 0)
    seg_ref[0] = jnp.where(row8 == 0, off, jnp.where(row8 == 1, cnt, 0.0)).astype(I32)

    dest_row = destb.T[0:1]
    hn = hn_ref[...]
    chi = cmb.astype(BF16)
    clo = (cmb - chi.astype(F32)).astype(BF16)

    def blk(rb, carry):
        r0 = pl.multiple_of(rb * LANES, LANES)
        rows = (r0 + lax.broadcasted_iota(I32, (LANES, tm), 0)).astype(F32)
        p = jnp.where(rows == dest_row, 1.0, 0.0).astype(BF16)
        hs_ref[0, pl.ds(r0, LANES), :] = _dot(p, hn).astype(BF16)
        cs_ref[0, pl.ds(r0, LANES), :] = _dot(p, chi) + _dot(p, clo)
        return carry

    lax.fori_loop(0, rmax // LANES, blk, 0)


def moe_sort(hn, cmb, *, tm):
    t, d = hn.shape
    nt = t // tm
    rmax = _moe_rmax(tm)
    ri = np.arange(MOE_SB)
    lts = jnp.asarray((ri[None, :] < ri[:, None]).astype(np.float32), BF16)
    return pl.pallas_call(
        functools.partial(_moe_sort_kernel, tm=tm, rmax=rmax),
        grid=(nt,),
        in_specs=[pl.BlockSpec((tm, d), lambda i: (i, 0)), pl.BlockSpec((tm, LANES), lambda i: (i, 0)),
                  pl.BlockSpec((MOE_SB, MOE_SB), lambda i: (0, 0))],
        out_specs=[pl.BlockSpec((1, rmax, d), lambda i: (i, 0, 0)),
                   pl.BlockSpec((1, rmax, LANES), lambda i: (i, 0, 0)),
                   pl.BlockSpec((tm, LANES), lambda i: (i, 0)),
                   pl.BlockSpec((1, SUBLANES, LANES), lambda i: (i, 0, 0))],
        out_shape=[jax.ShapeDtypeStruct((nt, rmax, d), BF16), jax.ShapeDtypeStruct((nt, rmax, LANES), F32),
                   jax.ShapeDtypeStruct((t, LANES), F32), jax.ShapeDtypeStruct((nt, SUBLANES, LANES), I32)],
        compiler_params=_cparams(("parallel",)),
        name="moe_sort",
    )(hn, cmb, lts)


def _moe_expert_kernel(seg_ref, hs_ref, cs_ref, wg_ref, wu_ref, wd_ref, y_ref, acc_ref):
    i = pl.program_id(0)
    e = pl.program_id(1)

    @pl.when(e == 0)
    def _():
        acc_ref[...] = jnp.zeros_like(acc_ref)

    g = e // EXPERTS_PER_GROUP
    off = seg_ref[i * 2 * N_GROUPS + g]
    n = seg_ref[i * 2 * N_GROUPS + N_GROUPS + g]

    def chunk(j, carry):
        r0 = pl.multiple_of(off + j * MOE_CH, MOE_PAD)
        hsl = hs_ref[0, pl.ds(r0, MOE_CH), :]
        gg = _dot(hsl, wg_ref[0])
        uu = _dot(hsl, wu_ref[0])
        c = cs_ref[0, pl.ds(r0, MOE_CH), :]
        lane = lax.broadcasted_iota(I32, c.shape, 1)
        cc = jnp.sum(jnp.where(lane == e, c, 0.0), axis=-1, keepdims=True)
        he = (gg * _sigmoid(gg)) * uu * cc
        acc_ref[pl.ds(r0, MOE_CH), :] += _dot(he.astype(BF16), wd_ref[0])
        return carry

    lax.fori_loop(0, (n + MOE_CH - 1) // MOE_CH, chunk, 0)

    @pl.when(e == pl.num_programs(1) - 1)
    def _():
        y_ref[0] = acc_ref[...].astype(BF16)


def moe_experts(seg, hs, cs, wg, wu, wd):
    nt, rmax, d = hs.shape
    ne, _, f = wg.shape
    return pl.pallas_call(
        _moe_expert_kernel,
        grid_spec=pltpu.PrefetchScalarGridSpec(
            num_scalar_prefetch=1,
            grid=(nt, ne),
            in_specs=[pl.BlockSpec((1, rmax, d), lambda i, e, sref: (i, 0, 0)),
                      pl.BlockSpec((1, rmax, LANES), lambda i, e, sref: (i, 0, 0)),
                      pl.BlockSpec((1, d, f), lambda i, e, sref: (e, 0, 0)),
                      pl.BlockSpec((1, d, f), lambda i, e, sref: (e, 0, 0)),
                      pl.BlockSpec((1, f, d), lambda i, e, sref: (e, 0, 0))],
            out_specs=pl.BlockSpec((1, rmax, d), lambda i, e, sref: (i, 0, 0)),
            scratch_shapes=[pltpu.VMEM((rmax, d), F32)]),
        out_shape=jax.ShapeDtypeStruct((nt, rmax, d), BF16),
        compiler_params=_cparams(("parallel", "arbitrary")),
        name="moe_experts",
    )(seg, hs, cs, wg, wu, wd)


def _moe_combine_kernel(x_ref, meta_ref, y_ref, gf_ref, o_ref, *, rmax, final_norm):
    tb = x_ref.shape[0]
    lanef = lax.broadcasted_iota(I32, (tb, rmax), 1).astype(F32)
    pt = jnp.where(lanef == meta_ref[:, 0:1], 1.0, 0.0).astype(BF16)
    out = x_ref[...] + _dot(pt, y_ref[0])
    o_ref[...] = _rms(out, gf_ref[...]) if final_norm else out


def moe_combine(x2, meta, y, g_final, *, tm, tb, final_norm):
    t, d = x2.shape
    nt, rmax, _ = y.shape
    nb = tm // tb
    return pl.pallas_call(
        functools.partial(_moe_combine_kernel, rmax=rmax, final_norm=final_norm),
        grid=(nt, nb),
        in_specs=[pl.BlockSpec((tb, d), lambda i, j: (i * nb + j, 0)),
                  pl.BlockSpec((tb, LANES), lambda i, j: (i * nb + j, 0)),
                  pl.BlockSpec((1, rmax, d), lambda i, j: (i, 0, 0)),
                  pl.BlockSpec((1, d), lambda i, j: (0, 0))],
        out_specs=pl.BlockSpec((tb, d), lambda i, j: (i * nb + j, 0)),
        out_shape=jax.ShapeDtypeStruct((t, d), F32),
        compiler_params=_cparams(("parallel", "arbitrary")),
        name="moe_combine",
    )(x2, meta, y, g_final)


def _permute_w_in(w):
    d = w.shape[0]
    o = 0
    seg = {}
    for name, width in (("c", 768), ("gla", 1024), ("ga", GLA_RANK), ("s5u", 256), ("dq", 256), ("dk", 64),
                        ("dv", 64), ("iq", 256), ("ik", IDX_DIM), ("iw", IDX_HEADS), ("gates", 4096)):
        seg[name] = w[:, o:o + width]
        o += width
    z = lambda n: jnp.zeros((d, n), w.dtype)
    cols = [seg["c"], seg["s5u"], seg["gla"], seg["dq"], seg["iq"], seg["dk"], seg["dv"],
            seg["ik"], seg["iw"], z(LANES - IDX_DIM - IDX_HEADS), seg["ga"], z(LANES - GLA_RANK),
            z(P_GATES - P_GA - LANES), seg["gates"]]
    out = jnp.concatenate(cols, axis=1).astype(BF16)
    assert out.shape[1] == P_TOTAL
    return out


def _s5_matrices(bb_re, bb_im, c_re, c_im):
    g, p, c = S5_GROUPS, S5_STATE, S5_GROUP
    n = g * p
    rows_g = jnp.arange(g * c) // c
    cols_g = jnp.arange(n) // p
    mask = (rows_g[:, None] == cols_g[None, :]).astype(F32)
    bm = jnp.concatenate([jnp.tile(bb_re, (g, 1)) * mask, jnp.tile(bb_im, (g, 1)) * mask], axis=1)
    ct = lambda a: jnp.tile(jnp.transpose(a, (0, 2, 1)).reshape(n, c), (1, g)) * mask.T
    cm = jnp.concatenate([ct(c_re), -ct(c_im)], axis=0)
    return bm.astype(BF16), cm.astype(BF16)


def _pick(s, pref):
    for c in pref:
        if s % c == 0:
            return c
    return s


def kernel(x, mem, positions, norm_mix, w_in, conv_w, conv_b, gla_a_up, gla_a_b, gla_norm, s5_lambda_re,
           s5_lambda_im, s5_log_dt, s5_b_re, s5_b_im, s5_c_re, s5_c_im, s5_d, s5_w_glu, s5_b_glu, w_branch,
           w_out, norm_cross, w_cq, w_ckv, w_co, norm_ffn, w_route_group, b_route_group, w_route_expert,
           b_route_expert, w_e_gate, w_e_up, w_e_down, norm_mem, norm_final):
    b, s, d = x.shape
    t = b * s
    m = mem.shape[1]
    depth = w_in.shape[0]
    topk = min(DSA_TOPK, s // 4)
    ts = _pick(s, (512, 256, 128))
    tm = _pick(t, (1024, 512, 256, 128))
    row = lambda a: a.reshape(1, -1)

    tabs = rope_tables(positions, ts=ts)
    x2 = x.reshape(t, d)
    mem2 = mem.reshape(b * m, d)
    for l in range(depth):
        proj = norm_matmul(x2, row(norm_mix[l]), _permute_w_in(w_in[l]), tm=tm, tn=1024)
        proj3 = proj.reshape(b, s, P_TOTAL)
        ya = conv_branch(proj3, conv_w[l], row(conv_b[l]), ts=ts)
        a_up_p = jnp.concatenate(
            [gla_a_up[l], jnp.zeros((LANES - GLA_RANK, gla_a_up.shape[2]), F32)], axis=0).astype(BF16)
        yb = gla_branch(proj3, a_up_p, row(gla_a_b[l]), row(gla_norm[l]), tt=ts)
        bb_re, bb_im, tab = s5_params(s5_lambda_re[l], s5_lambda_im[l], s5_log_dt[l], s5_b_re[l], s5_b_im[l])
        bm, cm = _s5_matrices(bb_re, bb_im, s5_c_re[l], s5_c_im[l])
        yc = s5_branch(proj3, bm, tab, cm, row(s5_d[l]), s5_w_glu[l].astype(BF16), row(s5_b_glu[l]), tt=ts)
        qs, qis, kr, vt4, kir, wt = dsa_prep(proj3, tabs, ts=ts)
        yd = dsa_attend(qs, qis, wt, kr, vt4, kir, qb=_pick(s, (2 * Q_BLOCK, Q_BLOCK)), kt=ts, topk=topk)
        w2 = lambda a: a.reshape(t, a.shape[-1])
        x2 = merge_out(x2, w2(ya), w2(yb), w2(yc), w2(yd), proj, w_branch[l].astype(BF16),
                       w_out[l].astype(BF16), tm=min(tm, 512))
        kv = norm_matmul(mem2, row(norm_mem), w_ckv[l].astype(BF16), tm=_pick(b * m, (1024, 512, 256)), tn=1024)
        x2 = cross_attn(x2.reshape(b, s, d), row(norm_cross[l]), w_cq[l].astype(BF16),
                        kv.reshape(b, m, -1), w_co[l].astype(BF16), tm=ts).reshape(t, d)
        pad = jnp.zeros((d, LANES - N_EXPERTS - N_GROUPS), F32)
        wr = jnp.concatenate([w_route_expert[l], w_route_group[l], pad], axis=1)
        br = jnp.concatenate([b_route_expert[l], b_route_group[l], pad[0]], axis=0).reshape(1, LANES)
        hn, cmb = moe_router(x2, row(norm_ffn[l]), wr, br, tm=min(tm, 512))
        tmoe = _pick(t, (2048, 1024, 512))
        hs, cs, meta, seg = moe_sort(hn, cmb, tm=tmoe)
        seg1 = jnp.concatenate([seg[:, 0, :N_GROUPS], seg[:, 1, :N_GROUPS]], axis=1).reshape(-1)
        y = moe_experts(seg1, hs, cs, w_e_gate[l].astype(BF16), w_e_up[l].astype(BF16), w_e_down[l].astype(BF16))
        x2 = moe_combine(x2, meta, y, row(norm_final), tm=tmoe, tb=min(tmoe, 512), final_norm=(l == depth - 1))
    return x2.reshape(b, s, d)
```

```python
import functools
import math

import numpy as np
import jax
import jax.numpy as jnp
from jax import lax
from jax.experimental import pallas as pl
from jax.experimental.pallas import tpu as pltpu

F32 = jnp.float32
BF16 = jnp.bfloat16
I32 = jnp.int32

D_MODEL = 1024
DEPTH = 2
EPS = 1e-6
N_BRANCH = 4
BRANCH_W = 256
CONV_W = 3
GLA_HEADS = 4
GLA_DK = 64
GLA_DV = 64
GLA_RANK = 16
GLA_TAU = 16.0
GLA_CHUNK = 64
S5_GROUP = 16
S5_GROUPS = BRANCH_W // S5_GROUP
S5_STATE = 64
DSA_HEADS = 4
DSA_HEAD_DIM = 64
IDX_HEADS = 8
IDX_DIM = 32
DSA_TOPK = 256
Q_BLOCK = 128
ROPE_THETA = 500000.0
ROPE_FRAC = 4
X_HEADS = 4
X_HEAD_DIM = 128
N_GROUPS = 4
EXPERTS_PER_GROUP = 4
N_EXPERTS = N_GROUPS * EXPERTS_PER_GROUP
D_FF_EXPERT = 512

LANES = 128
SUBLANES = 8
VMEM_LIMIT = 48 * 1024 * 1024

P_CONV = 0
P_S5U = 768
P_GLA = 1024
P_DQ = 2048
P_IQ = 2304
P_KV = 2560
P_IKW = 2688
P_GA = 2816
P_GATES = 3072
P_TOTAL = P_GATES + N_BRANCH * D_MODEL

INT_MIN = -2147483648
NEG_INF_KEY = INT_MIN + 0x7FFFFF
VT_ROWS = DSA_HEAD_DIM + 2 * SUBLANES


def _cparams(sem):
    return pltpu.CompilerParams(dimension_semantics=sem, vmem_limit_bytes=VMEM_LIMIT)


def _dot(a, b):
    return jnp.dot(a, b, preferred_element_type=F32)


def _dot_nt(a, b):
    return lax.dot_general(a, b, (((1,), (1,)), ((), ())), preferred_element_type=F32)


def _dot_tn(a, b):
    return lax.dot_general(a, b, (((0,), (0,)), ((), ())), preferred_element_type=F32)


def _split_dot(exact_bf16, x):
    hi = x.astype(BF16)
    lo = (x - hi.astype(F32)).astype(BF16)
    return _dot(exact_bf16, hi) + _dot(exact_bf16, lo)


def _split_dot_r(x, exact_bf16):
    hi = x.astype(BF16)
    lo = (x - hi.astype(F32)).astype(BF16)
    return _dot(hi, exact_bf16) + _dot(lo, exact_bf16)


def _rms(x, g):
    return x * lax.rsqrt(jnp.mean(x * x, axis=-1, keepdims=True) + EPS) * g


def _sigmoid(x):
    return 1.0 / (1.0 + jnp.exp(-x))


def _norm_matmul_kernel(x_ref, g_ref, w_ref, o_ref, hn_ref):
    @pl.when(pl.program_id(1) == 0)
    def _():
        hn_ref[...] = _rms(x_ref[...], g_ref[...]).astype(BF16)

    o_ref[...] = _dot(hn_ref[...], w_ref[...]).astype(o_ref.dtype)


def norm_matmul(x, g, w, *, tm, tn, out_dtype=BF16):
    t, d = x.shape
    n = w.shape[1]
    return pl.pallas_call(
        _norm_matmul_kernel,
        grid=(t // tm, n // tn),
        in_specs=[
            pl.BlockSpec((tm, d), lambda i, j: (i, 0)),
            pl.BlockSpec((1, d), lambda i, j: (0, 0)),
            pl.BlockSpec((d, tn), lambda i, j: (0, j)),
        ],
        out_specs=pl.BlockSpec((tm, tn), lambda i, j: (i, j)),
        out_shape=jax.ShapeDtypeStruct((t, n), out_dtype),
        scratch_shapes=[pltpu.VMEM((tm, d), BF16)],
        compiler_params=_cparams(("parallel", "arbitrary")),
        name="norm_matmul",
    )(x, g, w)


def _conv_kernel(cur_ref, prev_ref, w_ref, b_ref, o_ref):
    i = pl.program_id(1)
    w_ = BRANCH_W
    cur = cur_ref[0].astype(F32)
    u = cur[:, 2 * w_:3 * w_] * cur[:, 0:w_]
    pv = prev_ref[0].astype(F32)
    pu = pv[:, 2 * w_:3 * w_] * pv[:, 0:w_]
    pu = jnp.where(i > 0, pu, 0.0)
    row = lax.broadcasted_iota(I32, u.shape, 0)
    u1 = jnp.where(row == 0, pu[7:8], pltpu.roll(u, 1, 0))
    u2 = jnp.where(row == 0, pu[6:7], jnp.where(row == 1, pu[7:8], pltpu.roll(u, 2, 0)))
    w = w_ref[...]
    y = w[0:1] * u2 + w[1:2] * u1 + w[2:3] * u + b_ref[...]
    o_ref[0] = (cur[:, w_:2 * w_] * y).astype(o_ref.dtype)


def conv_branch(proj3, conv_w, conv_b, *, ts):
    b, s, _ = proj3.shape
    wc = 3 * BRANCH_W
    hb = ts // SUBLANES
    return pl.pallas_call(
        _conv_kernel,
        grid=(b, s // ts),
        in_specs=[
            pl.BlockSpec((1, ts, wc), lambda bi, i: (bi, i, P_CONV // wc)),
            pl.BlockSpec((1, SUBLANES, wc), lambda bi, i: (bi, jnp.maximum(i * hb - 1, 0), P_CONV // wc)),
            pl.BlockSpec((CONV_W, BRANCH_W), lambda bi, i: (0, 0)),
            pl.BlockSpec((1, BRANCH_W), lambda bi, i: (0, 0)),
        ],
        out_specs=pl.BlockSpec((1, ts, BRANCH_W), lambda bi, i: (bi, i, 0)),
        out_shape=jax.ShapeDtypeStruct((b, s, BRANCH_W), BF16),
        compiler_params=_cparams(("parallel", "arbitrary")),
        name="conv_branch",
    )(proj3, proj3, conv_w, conv_b)


def _gla_kernel(g_ref, a_ref, aup_ref, ab_ref, gn_ref, ltb_ref, ltf_ref, bob_ref, bd_ref, bdb_ref,
                o_ref, st_ref, *, tt):
    hw = GLA_HEADS * GLA_DK
    c = GLA_CHUNK

    @pl.when(pl.program_id(1) == 0)
    def _():
        st_ref[...] = jnp.zeros_like(st_ref)

    blk = g_ref[0]
    q = blk[:, 0:hw].astype(F32)
    k = blk[:, hw:2 * hw].astype(F32)
    vb = blk[:, 2 * hw:3 * hw]
    v = vb.astype(F32)
    r = blk[:, 3 * hw:4 * hw].astype(F32)

    pre = _dot(a_ref[0], aup_ref[...]) + ab_ref[...]
    la = (jnp.minimum(pre, 0.0) - jnp.log(1.0 + jnp.exp(-jnp.abs(pre)))) * (1.0 / GLA_TAU)
    cum = _split_dot(ltb_ref[...], la)
    tot = _split_dot(bob_ref[...], la)
    q_dec = q * (GLA_DK ** -0.5) * jnp.exp(cum)
    k_inv = (k * jnp.exp(-cum)).astype(BF16)
    k_end = (k * jnp.exp(tot - cum)).astype(BF16)
    qdb = q_dec.astype(BF16)

    lane = lax.broadcasted_iota(I32, (1, hw), 1)
    ltmask = ltf_ref[...] > 0.0
    o = jnp.zeros((tt, hw), F32)
    for h in range(GLA_HEADS):
        hm = (lane // GLA_DK) == h
        qh = jnp.where(hm, q_dec, 0.0).astype(BF16)
        att = jnp.where(ltmask, _dot_nt(qh, k_inv), 0.0)
        vh = jnp.where(hm, v, 0.0).astype(BF16)
        o = o + _dot(att.astype(BF16), vh)

    st = st_ref[...]
    bd = bd_ref[...]
    inter = []
    for n in range(tt // c):
        rows = slice(n * c, (n + 1) * c)
        inter.append(_dot_nt(qdb[rows], st.astype(BF16)))
        dec = jnp.exp(tot[n * c:n * c + 1, :])
        st = st * dec + _dot_tn(vb[rows], k_end[rows]) * bd
    st_ref[...] = st
    o = o + jnp.concatenate(inter, axis=0)

    msq = _split_dot_r(o * o, bdb_ref[...]) * (1.0 / GLA_DV)
    y = o * lax.rsqrt(msq + EPS) * gn_ref[...] * (r * _sigmoid(r))
    o_ref[0] = y.astype(o_ref.dtype)


def _gla_consts(tt):
    ri = np.arange(tt)[:, None]
    ci = np.arange(tt)[None, :]
    same = (ri // GLA_CHUNK) == (ci // GLA_CHUNK)
    lt = (same & (ci <= ri)).astype(np.float32)
    hw = GLA_HEADS * GLA_DK
    hi = np.arange(hw)
    bd = ((hi[:, None] // GLA_DK) == (hi[None, :] // GLA_DK)).astype(np.float32)
    return (jnp.asarray(lt, BF16), jnp.asarray(lt, F32), jnp.asarray(same.astype(np.float32), BF16),
            jnp.asarray(bd, F32), jnp.asarray(bd, BF16))


def gla_branch(proj3, a_up_p, a_b, g_norm, *, tt):
    b, s, _ = proj3.shape
    hw = GLA_HEADS * GLA_DK
    ltb, ltf, bob, bd, bdb = _gla_consts(tt)
    const = lambda shape: pl.BlockSpec(shape, lambda bi, i: (0,) * len(shape))
    return pl.pallas_call(
        functools.partial(_gla_kernel, tt=tt),
        grid=(b, s // tt),
        in_specs=[
            pl.BlockSpec((1, tt, 4 * hw), lambda bi, i: (bi, i, P_GLA // (4 * hw))),
            pl.BlockSpec((1, tt, LANES), lambda bi, i: (bi, i, P_GA // LANES)),
            const((LANES, hw)), const((1, hw)), const((1, hw)),
            const((tt, tt)), const((tt, tt)), const((tt, tt)), const((hw, hw)), const((hw, hw)),
        ],
        out_specs=pl.BlockSpec((1, tt, hw), lambda bi, i: (bi, i, 0)),
        out_shape=jax.ShapeDtypeStruct((b, s, hw), BF16),
        scratch_shapes=[pltpu.VMEM((hw, hw), F32)],
        compiler_params=_cparams(("parallel", "arbitrary")),
        name="gla_branch",
    )(proj3, proj3, a_up_p, a_b, g_norm, ltb, ltf, bob, bd, bdb)


def _s5_param_kernel(lre_ref, lim_ref, ldt_ref, bre_ref, bim_ref, bbre_ref, bbim_ref, tab_ref):
    lre = lre_ref[...]
    lim = lim_ref[...]
    dt = jnp.exp(ldt_ref[...])
    mag = jnp.exp(lre * dt)
    lbr = mag * jnp.cos(lim * dt)
    lbi = mag * jnp.sin(lim * dt)
    den = lre * lre + lim * lim
    fre = ((lbr - 1.0) * lre + lbi * lim) / den
    fim = (lbi * lre - (lbr - 1.0) * lim) / den
    bre = bre_ref[...]
    bim = bim_ref[...]
    bbre_ref[...] = fre * bre - fim * bim
    bbim_ref[...] = fre * bim + fim * bre

    pw = [None, (lbr, lbi)]
    for _ in range(2, SUBLANES + 1):
        pr, pi = pw[-1]
        pw.append((pr * lbr - pi * lbi, pr * lbi + pi * lbr))
    n = lre.shape[-1]
    row = lax.broadcasted_iota(I32, (SUBLANES, n), 0)
    zero = jnp.zeros((SUBLANES, n), F32)
    for idx, sft in enumerate((1, 2, 4)):
        tab_ref[2 * idx] = jnp.where(row >= sft, jnp.broadcast_to(pw[sft][0], (SUBLANES, n)), zero)
        tab_ref[2 * idx + 1] = jnp.where(row >= sft, jnp.broadcast_to(pw[sft][1], (SUBLANES, n)), zero)
    cr, ci = zero, zero
    for rr in range(SUBLANES):
        cr = jnp.where(row == rr, jnp.broadcast_to(pw[rr + 1][0], (SUBLANES, n)), cr)
        ci = jnp.where(row == rr, jnp.broadcast_to(pw[rr + 1][1], (SUBLANES, n)), ci)
    tab_ref[6] = cr
    tab_ref[7] = ci


def s5_params(lam_re, lam_im, log_dt, b_re, b_im):
    g, p = lam_re.shape
    n = g * p
    row = lambda a: a.reshape(1, n)
    ldt = jnp.broadcast_to(log_dt[:, None], (g, p))
    bt = lambda a: jnp.transpose(a, (2, 0, 1)).reshape(S5_GROUP, n)
    full = lambda shape: pl.BlockSpec(shape, lambda: (0,) * len(shape))
    return pl.pallas_call(
        _s5_param_kernel,
        in_specs=[full((1, n))] * 3 + [full((S5_GROUP, n))] * 2,
        out_specs=[full((S5_GROUP, n)), full((S5_GROUP, n)), full((8, SUBLANES, n))],
        out_shape=[jax.ShapeDtypeStruct((S5_GROUP, n), F32), jax.ShapeDtypeStruct((S5_GROUP, n), F32),
                   jax.ShapeDtypeStruct((8, SUBLANES, n), F32)],
        name="s5_params",
    )(row(lam_re), row(lam_im), row(ldt), bt(b_re), bt(b_im))


def _s5_kernel(u_ref, bm_ref, tab_ref, cm_ref, d_ref, wg_ref, bg_ref, o_ref, xs_ref, car_ref, *, tt, n):
    @pl.when(pl.program_id(1) == 0)
    def _():
        car_ref[...] = jnp.zeros_like(car_ref)

    ub = u_ref[0]
    xs_ref[...] = _dot(ub, bm_ref[...])

    def group(gi, carry):
        r0 = pl.multiple_of(gi * SUBLANES, SUBLANES)
        for j in range(n // LANES):
            cre = slice(j * LANES, (j + 1) * LANES)
            cim = slice(n + j * LANES, n + (j + 1) * LANES)
            re = xs_ref[pl.ds(r0, SUBLANES), cre]
            im = xs_ref[pl.ds(r0, SUBLANES), cim]
            for idx, sft in enumerate((1, 2, 4)):
                ar = tab_ref[2 * idx, :, cre]
                ai = tab_ref[2 * idx + 1, :, cre]
                sr = pltpu.roll(re, sft, 0)
                si = pltpu.roll(im, sft, 0)
                re, im = re + ar * sr - ai * si, im + ar * si + ai * sr
            pr = tab_ref[6, :, cre]
            pi = tab_ref[7, :, cre]
            cr = car_ref[0, :, cre]
            ci = car_ref[1, :, cre]
            re, im = re + pr * cr - pi * ci, im + pr * ci + pi * cr
            xs_ref[pl.ds(r0, SUBLANES), cre] = re
            xs_ref[pl.ds(r0, SUBLANES), cim] = im
            car_ref[0, :, cre] = jnp.broadcast_to(re[SUBLANES - 1:SUBLANES], (SUBLANES, LANES))
            car_ref[1, :, cre] = jnp.broadcast_to(im[SUBLANES - 1:SUBLANES], (SUBLANES, LANES))
        return carry

    lax.fori_loop(0, tt // SUBLANES, group, 0)

    y = _dot(xs_ref[...].astype(BF16), cm_ref[...]) + d_ref[...] * ub.astype(F32)
    y = 0.5 * y * (1.0 + jnp.tanh(math.sqrt(2.0 / math.pi) * (y + 0.044715 * (y * y * y))))
    z = _dot(y.astype(BF16), wg_ref[...]) + bg_ref[...]
    o_ref[0] = (y * _sigmoid(z)).astype(o_ref.dtype)


def s5_branch(proj3, bmat, tab, cmat, d_skip, w_glu, b_glu, *, tt):
    b, s, _ = proj3.shape
    w_ = BRANCH_W
    n = S5_GROUPS * S5_STATE
    const = lambda shape: pl.BlockSpec(shape, lambda bi, i: (0,) * len(shape))
    return pl.pallas_call(
        functools.partial(_s5_kernel, tt=tt, n=n),
        grid=(b, s // tt),
        in_specs=[
            pl.BlockSpec((1, tt, w_), lambda bi, i: (bi, i, P_S5U // w_)),
            const((w_, 2 * n)), const((8, SUBLANES, n)), const((2 * n, w_)),
            const((1, w_)), const((w_, w_)), const((1, w_)),
        ],
        out_specs=pl.BlockSpec((1, tt, w_), lambda bi, i: (bi, i, 0)),
        out_shape=jax.ShapeDtypeStruct((b, s, w_), BF16),
        scratch_shapes=[pltpu.VMEM((tt, 2 * n), F32), pltpu.VMEM((2, SUBLANES, n), F32)],
        compiler_params=_cparams(("parallel", "arbitrary")),
        name="s5_branch",
    )(proj3, bmat, tab, cmat, d_skip, w_glu, b_glu)


def _rope_freq_rows():
    rows = np.zeros((8, LANES), np.float32)
    for pat, dh in enumerate((DSA_HEAD_DIM, IDX_DIM)):
        rd = dh // ROPE_FRAC
        half = rd // 2
        inv = (np.float32(ROPE_THETA) ** (-np.arange(half, dtype=np.float32) * np.float32(2.0 / rd))).astype(np.float32)
        for l in range(LANES):
            i = l % dh
            if i < half:
                rows[3 * pat, l] = inv[i]
                rows[3 * pat + 1, l] = -1.0
            elif i < rd:
                rows[3 * pat, l] = inv[i - half]
                rows[3 * pat + 2, l] = 1.0
    return rows


def _rope_tab_kernel(pos_ref, fr_ref, o_ref):
    pos = pos_ref[0]
    fr = fr_ref[...]
    for pat in range(2):
        ang = pos * fr[3 * pat:3 * pat + 1]
        c = jnp.cos(ang)
        s = jnp.sin(ang)
        o_ref[0, 3 * pat] = c
        o_ref[0, 3 * pat + 1] = s * fr[3 * pat + 1:3 * pat + 2]
        o_ref[0, 3 * pat + 2] = s * fr[3 * pat + 2:3 * pat + 3]


def rope_tables(positions, *, ts):
    b, s = positions.shape
    pos = positions.astype(F32).reshape(b, s, 1)
    fr = jnp.asarray(_rope_freq_rows())
    return pl.pallas_call(
        _rope_tab_kernel,
        grid=(b, s // ts),
        in_specs=[pl.BlockSpec((1, ts, 1), lambda bi, i: (bi, i, 0)),
                  pl.BlockSpec((8, LANES), lambda bi, i: (0, 0))],
        out_specs=pl.BlockSpec((1, 6, ts, LANES), lambda bi, i: (bi, 0, i, 0)),
        out_shape=jax.ShapeDtypeStruct((b, 6, s, LANES), F32),
        compiler_params=_cparams(("parallel", "arbitrary")),
        name="rope_tables",
    )(pos, fr)


def _rope(t, c, sm, sp, half):
    n = t.shape[-1]
    return t * c + sm * pltpu.roll(t, n - half, 1) + sp * pltpu.roll(t, half, 1)


def _dsa_prep_kernel(dq_ref, iq_ref, kv_ref, ikw_ref, tab_ref,
                     qs_ref, qis_ref, kr_ref, vt_ref, kir_ref, wt_ref):
    two = lambda a: jnp.concatenate([a, a], axis=1)
    c1, sm1, sp1 = tab_ref[0, 0], tab_ref[0, 1], tab_ref[0, 2]
    c2, sm2, sp2 = tab_ref[0, 3], tab_ref[0, 4], tab_ref[0, 5]
    h1 = DSA_HEAD_DIM // ROPE_FRAC // 2
    h2 = IDX_DIM // ROPE_FRAC // 2
    lane = lax.broadcasted_iota(I32, (1, LANES), 1)

    q = _rope(dq_ref[0].astype(F32), two(c1), two(sm1), two(sp1), h1) * (DSA_HEAD_DIM ** -0.5 * math.log2(math.e))
    for h in range(DSA_HEADS):
        qs_ref[0, h] = q[:, h * DSA_HEAD_DIM:(h + 1) * DSA_HEAD_DIM].astype(BF16)
    qi = _rope(iq_ref[0].astype(F32), two(c2), two(sm2), two(sp2), h2)
    for h in range(IDX_HEADS):
        qis_ref[0, h] = qi[:, h * IDX_DIM:(h + 1) * IDX_DIM].astype(BF16)

    kv = kv_ref[0].astype(F32)
    isk = lane < DSA_HEAD_DIM
    kvr = _rope(kv, jnp.where(isk, c1, 1.0), jnp.where(isk, sm1, 0.0), jnp.where(isk, sp1, 0.0), h1)
    kr_ref[0] = kvr[:, 0:DSA_HEAD_DIM].astype(BF16)
    ones = jnp.ones((VT_ROWS - DSA_HEAD_DIM, kv.shape[0]), BF16)
    vt_ref[0, 0] = jnp.concatenate([kvr.T[DSA_HEAD_DIM:2 * DSA_HEAD_DIM].astype(BF16), ones], axis=0)

    ikw = ikw_ref[0].astype(F32)
    isi = lane < IDX_DIM
    ikr = _rope(ikw, jnp.where(isi, c2, 1.0), jnp.where(isi, sm2, 0.0), jnp.where(isi, sp2, 0.0), h2)
    kir_ref[0] = ikr[:, 0:IDX_DIM].astype(BF16)
    wt_ref[0] = ikr.T[IDX_DIM:IDX_DIM + IDX_HEADS] * ((IDX_HEADS ** -0.5) * (IDX_DIM ** -0.5))


def dsa_prep(proj3, tabs, *, ts):
    b, s, _ = proj3.shape
    qw = DSA_HEADS * DSA_HEAD_DIM
    iw = IDX_HEADS * IDX_DIM
    return pl.pallas_call(
        _dsa_prep_kernel,
        grid=(b, s // ts),
        in_specs=[
            pl.BlockSpec((1, ts, qw), lambda bi, i: (bi, i, P_DQ // qw)),
            pl.BlockSpec((1, ts, iw), lambda bi, i: (bi, i, P_IQ // iw)),
            pl.BlockSpec((1, ts, LANES), lambda bi, i: (bi, i, P_KV // LANES)),
            pl.BlockSpec((1, ts, LANES), lambda bi, i: (bi, i, P_IKW // LANES)),
            pl.BlockSpec((1, 6, ts, LANES), lambda bi, i: (bi, 0, i, 0)),
        ],
        out_specs=[
            pl.BlockSpec((1, DSA_HEADS, ts, DSA_HEAD_DIM), lambda bi, i: (bi, 0, i, 0)),
            pl.BlockSpec((1, IDX_HEADS, ts, IDX_DIM), lambda bi, i: (bi, 0, i, 0)),
            pl.BlockSpec((1, ts, DSA_HEAD_DIM), lambda bi, i: (bi, i, 0)),
            pl.BlockSpec((1, 1, VT_ROWS, ts), lambda bi, i: (bi, i, 0, 0)),
            pl.BlockSpec((1, ts, IDX_DIM), lambda bi, i: (bi, i, 0)),
            pl.BlockSpec((1, IDX_HEADS, ts), lambda bi, i: (bi, 0, i)),
        ],
        out_shape=[
            jax.ShapeDtypeStruct((b, DSA_HEADS, s, DSA_HEAD_DIM), BF16),
            jax.ShapeDtypeStruct((b, IDX_HEADS, s, IDX_DIM), BF16),
            jax.ShapeDtypeStruct((b, s, DSA_HEAD_DIM), BF16),
            jax.ShapeDtypeStruct((b, s // ts, VT_ROWS, ts), BF16),
            jax.ShapeDtypeStruct((b, s, IDX_DIM), BF16),
            jax.ShapeDtypeStruct((b, IDX_HEADS, s), F32),
        ],
        compiler_params=_cparams(("parallel", "arbitrary")),
        name="dsa_prep",
    )(proj3, proj3, proj3, proj3, tabs)


def _dsa_kernel(q_ref, qi_ref, w_ref, k_ref, vt_ref, ki_ref, lts_ref, o_ref, sc_ref, lg_ref, *, qb, kt, topk):
    i = pl.program_id(1)
    q0 = i * qb
    nkt = (q0 + qb + kt - 1) // kt
    kf = float(topk)
    sg = 8 * SUBLANES
    wrow = w_ref[0]
    tq = q0 + lax.broadcasted_iota(I32, (1, qb), 1)

    def to_key(x):
        bits = pltpu.bitcast(x, I32)
        return bits ^ ((bits >> 31) & 0x7FFFFFFF)

    def scores(t, gmax):
        k0 = pl.multiple_of(t * kt, kt)
        kit = ki_ref[0, pl.ds(k0, kt), :]
        acc = jnp.zeros((kt, qb), F32)
        for h in range(IDX_HEADS):
            acc = acc + jnp.maximum(_dot_nt(kit, qi_ref[0, h]), 0.0) * wrow[h:h + 1]
        acc = jnp.where(acc == 0.0, 0.0, acc)
        srow = k0 + lax.broadcasted_iota(I32, (kt, qb), 0)
        causal = srow <= tq
        sc_ref[t] = jnp.where(causal, to_key(acc), INT_MIN)
        accm = jnp.where(causal, acc, -jnp.inf)
        return jnp.maximum(gmax, jnp.max(accm.reshape(kt // topk, topk, qb), axis=0))

    gmax = lax.fori_loop(0, nkt, scores, jnp.full((topk, qb), -jnp.inf, F32))

    def count(pred):
        def body(t, c):
            m = jnp.where(pred(sc_ref[t]), 1.0, 0.0)
            return c + jnp.sum(m.reshape(kt // sg, sg, qb), axis=0)
        c = lax.fori_loop(0, nkt, body, jnp.zeros((sg, qb), F32))
        return jnp.sum(c, axis=0, keepdims=True)

    need = (tq + 1) > topk
    lo0 = to_key(jnp.min(gmax, axis=0, keepdims=True))
    hi0 = to_key(jnp.max(gmax, axis=0, keepdims=True))
    done0 = jnp.where(jnp.logical_and(need, lo0 < hi0), 0.0, 1.0)

    def bs_cond(c):
        it, _, _, _, done, _ = c
        return jnp.logical_and(it < 34, jnp.sum(1.0 - done) > 0.0)

    def bs_step(lo, hi, thr, done, below, probe=None):
        mid = (lo >> 1) + (hi >> 1) + (((lo & 1) + (hi & 1) + 1) >> 1)
        if probe is not None:
            mid = jnp.where(jnp.logical_and(lo < probe, probe <= hi), probe, mid)
        cnt = count(lambda kk: kk >= mid)
        ge = cnt >= kf
        hit = cnt == kf
        near = cnt == kf - 1.0
        lo = jnp.where(ge, mid, lo)
        hi = jnp.where(ge, hi, mid - 1)
        fin = jnp.where(hit, 1.0, jnp.where(near, 1.0, jnp.where(lo == hi, 1.0, 0.0)))
        val = jnp.where(hit, mid, jnp.where(near, mid - 1, lo))
        live = done == 0.0
        thr = jnp.where(live, val, thr)
        below = jnp.where(live, jnp.where(near, 1.0, 0.0), below)
        return lo, hi, thr, jnp.maximum(done, fin), below

    def bs_body(c):
        it = c[0]
        return (it + 2,) + bs_step(*bs_step(*c[1:]))

    st = bs_step(*bs_step(lo0, hi0, lo0, done0, jnp.zeros((1, qb), F32), probe=1), probe=0)
    _, _, _, thr, _, below = lax.while_loop(bs_cond, bs_body, (jnp.int32(2),) + st)

    def below_max(t, c):
        kk = sc_ref[t]
        cand = jnp.where(kk <= thr, kk, NEG_INF_KEY)
        return jnp.maximum(c, jnp.max(cand.reshape(kt // sg, sg, qb), axis=0))

    bm = lax.fori_loop(0, nkt, below_max, jnp.full((sg, qb), NEG_INF_KEY, I32))
    bmax = to_key(jnp.max(pltpu.bitcast(bm ^ ((bm >> 31) & 0x7FFFFFFF), F32), axis=0, keepdims=True))
    thr = jnp.where(below > 0.0, bmax, thr)
    thr = jnp.where(need, thr, INT_MIN + 1)

    cge = count(lambda kk: kk >= thr)
    tie = jnp.logical_and(need, cge > kf)

    @pl.when(jnp.sum(jnp.where(tie, 1.0, 0.0)) > 0.0)
    def _():
        cgt = count(lambda kk: kk > thr)
        room = kf - cgt

        def body(t, seen):
            kk = sc_ref[t]
            eq = jnp.logical_and(kk == thr, tie)
            eqf = jnp.where(eq, 1.0, 0.0)
            before = _dot(lts_ref[...], eqf.astype(BF16)) + seen
            sc_ref[t] = jnp.where(jnp.logical_and(eq, before >= room), INT_MIN, kk)
            return seen + jnp.sum(eqf, axis=0, keepdims=True)

        lax.fori_loop(0, nkt, body, jnp.zeros((1, qb), F32))

    nh = DSA_HEADS

    def logits(t, c):
        k0 = pl.multiple_of(t * kt, kt)
        kt_ = k_ref[0, pl.ds(k0, kt), :]
        bias = jnp.where(sc_ref[t] >= thr, 0.0, -jnp.inf)
        out = []
        for h in range(nh):
            lg = _dot_nt(kt_, q_ref[0, h]) + bias
            lg_ref[t, h] = lg
            out.append(jnp.maximum(c[h], jnp.max(lg.reshape(kt // sg, sg, qb), axis=0)))
        return tuple(out)

    mparts = lax.fori_loop(0, nkt, logits, (jnp.full((sg, qb), -jnp.inf, F32),) * nh)
    ms = []
    for h in range(nh):
        m = jnp.max(mparts[h], axis=0, keepdims=True)
        ms.append(jnp.where(m == -jnp.inf, 0.0, m))

    def attend(t, c):
        vt = vt_ref[0, t]
        out = []
        for h in range(nh):
            p = jnp.exp2((lg_ref[t, h] - ms[h]).astype(BF16))
            out.append(c[h] + _dot(vt, p))
        return tuple(out)

    res = lax.fori_loop(0, nkt, attend, (jnp.zeros((vt_ref.shape[2], qb), F32),) * nh)
    dh = DSA_HEAD_DIM
    for h in range(nh):
        ot = res[h][0:dh] / res[h][dh:dh + 1]
        o_ref[0, :, h * dh:(h + 1) * dh] = ot.T.astype(o_ref.dtype)


def dsa_attend(qs, qis, wt, kr, vt4, kir, *, qb, kt, topk):
    b, nh, s, dh = qs.shape
    nkt = s // kt
    ri = np.arange(kt)
    lts = jnp.asarray((ri[None, :] < ri[:, None]).astype(np.float32), BF16)
    return pl.pallas_call(
        functools.partial(_dsa_kernel, qb=qb, kt=kt, topk=topk),
        grid=(b, s // qb),
        in_specs=[
            pl.BlockSpec((1, nh, qb, dh), lambda bi, i: (bi, 0, i, 0)),
            pl.BlockSpec((1, IDX_HEADS, qb, IDX_DIM), lambda bi, i: (bi, 0, i, 0)),
            pl.BlockSpec((1, IDX_HEADS, qb), lambda bi, i: (bi, 0, i)),
            pl.BlockSpec((1, s, dh), lambda bi, i: (bi, 0, 0)),
            pl.BlockSpec((1, nkt, VT_ROWS, kt), lambda bi, i: (bi, 0, 0, 0)),
            pl.BlockSpec((1, s, IDX_DIM), lambda bi, i: (bi, 0, 0)),
            pl.BlockSpec((kt, kt), lambda bi, i: (0, 0)),
        ],
        out_specs=pl.BlockSpec((1, qb, nh * dh), lambda bi, i: (bi, i, 0)),
        out_shape=jax.ShapeDtypeStruct((b, s, nh * dh), BF16),
        scratch_shapes=[pltpu.VMEM((nkt, kt, qb), I32), pltpu.VMEM((nkt, nh, kt, qb), F32)],
        compiler_params=_cparams(("parallel", "arbitrary")),
        name="dsa_attend",
    )(qs, qis, wt, kr, vt4, kir, lts)


def _merge_kernel(x_ref, ya_ref, yb_ref, yc_ref, yd_ref, g0_ref, g1_ref, g2_ref, g3_ref, wb_ref, wo_ref, o_ref):
    ys = (ya_ref, yb_ref, yc_ref, yd_ref)
    gs = (g0_ref, g1_ref, g2_ref, g3_ref)
    merged = jnp.zeros(o_ref.shape, F32)
    for i in range(N_BRANCH):
        merged = merged + _sigmoid(gs[i][...].astype(F32)) * _dot(ys[i][...], wb_ref[i])
    o_ref[...] = x_ref[...] + _dot(merged.astype(BF16), wo_ref[...])


def merge_out(x2, ya, yb, yc, yd, proj, w_branch, w_out, *, tm):
    t, d = x2.shape
    w_ = BRANCH_W
    yspec = pl.BlockSpec((tm, w_), lambda i: (i, 0))
    gspec = lambda n: pl.BlockSpec((tm, d), lambda i: (i, P_GATES // d + n))
    return pl.pallas_call(
        _merge_kernel,
        grid=(t // tm,),
        in_specs=[pl.BlockSpec((tm, d), lambda i: (i, 0)), yspec, yspec, yspec, yspec,
                  gspec(0), gspec(1), gspec(2), gspec(3),
                  pl.BlockSpec((N_BRANCH, w_, d), lambda i: (0, 0, 0)),
                  pl.BlockSpec((d, d), lambda i: (0, 0))],
        out_specs=pl.BlockSpec((tm, d), lambda i: (i, 0)),
        out_shape=jax.ShapeDtypeStruct((t, d), F32),
        compiler_params=_cparams(("parallel",)),
        name="merge_out",
    )(x2, ya, yb, yc, yd, proj, proj, proj, proj, w_branch, w_out)


def _cross_kernel(x_ref, g_ref, wq_ref, kv_ref, wo_ref, o_ref):
    x = x_ref[0]
    hn = _rms(x, g_ref[...]).astype(BF16)
    q = _dot(hn, wq_ref[...]) * (X_HEAD_DIM ** -0.5)
    kv = kv_ref[0]
    hd = X_HEADS * X_HEAD_DIM
    outs = []
    for h in range(X_HEADS):
        cs = slice(h * X_HEAD_DIM, (h + 1) * X_HEAD_DIM)
        lg = _dot_nt(q[:, cs].astype(BF16), kv[:, cs])
        p = jnp.exp(lg - jnp.max(lg, axis=-1, keepdims=True))
        p = p / jnp.sum(p, axis=-1, keepdims=True)
        outs.append(_dot(p.astype(BF16), kv[:, hd + h * X_HEAD_DIM:hd + (h + 1) * X_HEAD_DIM]))
    o = jnp.concatenate(outs, axis=1).astype(BF16)
    o_ref[0] = x + _dot(o, wo_ref[...])


def cross_attn(x3, g, w_q, kv, w_o, *, tm):
    b, s, d = x3.shape
    m = kv.shape[1]
    hd = X_HEADS * X_HEAD_DIM
    return pl.pallas_call(
        _cross_kernel,
        grid=(b, s // tm),
        in_specs=[pl.BlockSpec((1, tm, d), lambda bi, i: (bi, i, 0)),
                  pl.BlockSpec((1, d), lambda bi, i: (0, 0)),
                  pl.BlockSpec((d, hd), lambda bi, i: (0, 0)),
                  pl.BlockSpec((1, m, 2 * hd), lambda bi, i: (bi, 0, 0)),
                  pl.BlockSpec((hd, d), lambda bi, i: (0, 0))],
        out_specs=pl.BlockSpec((1, tm, d), lambda bi, i: (bi, i, 0)),
        out_shape=jax.ShapeDtypeStruct((b, s, d), F32),
        compiler_params=_cparams(("parallel", "arbitrary")),
        name="cross_attn",
    )(x3, g, w_q, kv, w_o)


def _router_kernel(x_ref, g_ref, wr_ref, br_ref, hn_ref, cmb_ref):
    hn = _rms(x_ref[...], g_ref[...])
    hn_ref[...] = hn.astype(BF16)
    lg = jnp.dot(hn, wr_ref[...], precision=lax.Precision.HIGHEST, preferred_element_type=F32) + br_ref[...]
    lane = lax.broadcasted_iota(I32, lg.shape, 1)
    lanef = lane.astype(F32)
    ninf = -jnp.inf
    big = float(LANES)
    first = lambda mask: jnp.min(jnp.where(mask, lanef, big), axis=-1, keepdims=True)

    isg = jnp.logical_and(lane >= N_EXPERTS, lane < N_EXPERTS + N_GROUPS)
    gl = jnp.where(isg, lg, ninf)
    gmax = jnp.max(gl, axis=-1, keepdims=True)
    gsel = first(gl == gmax) - float(N_EXPERTS)
    pg = 1.0 / jnp.sum(jnp.exp(gl - gmax), axis=-1, keepdims=True)

    ise = jnp.floor(lanef * (1.0 / EXPERTS_PER_GROUP)) == gsel
    el = jnp.where(ise, lg, ninf)
    e1 = jnp.max(el, axis=-1, keepdims=True)
    i1 = first(el == e1)
    el2 = jnp.where(lanef == i1, ninf, el)
    e2 = jnp.max(el2, axis=-1, keepdims=True)
    i2 = first(el2 == e2)
    d = jnp.exp(e2 - e1)
    w1 = 1.0 / (1.0 + d)
    w2 = d / (1.0 + d)
    cmb = jnp.where(lanef == i1, pg * w1, jnp.where(lanef == i2, pg * w2, 0.0))
    cmb_ref[...] = jnp.where(lane == N_EXPERTS, gsel, cmb)


def moe_router(x2, g, wr, br, *, tm):
    t, d = x2.shape
    return pl.pallas_call(
        _router_kernel,
        grid=(t // tm,),
        in_specs=[pl.BlockSpec((tm, d), lambda i: (i, 0)), pl.BlockSpec((1, d), lambda i: (0, 0)),
                  pl.BlockSpec((d, LANES), lambda i: (0, 0)), pl.BlockSpec((1, LANES), lambda i: (0, 0))],
        out_specs=[pl.BlockSpec((tm, d), lambda i: (i, 0)), pl.BlockSpec((tm, LANES), lambda i: (i, 0))],
        out_shape=[jax.ShapeDtypeStruct((t, d), BF16), jax.ShapeDtypeStruct((t, LANES), F32)],
        compiler_params=_cparams(("parallel",)),
        name="moe_router",
    )(x2, g, wr, br)


MOE_PAD = 2 * SUBLANES
MOE_CH = 288
MOE_SB = 256


def _moe_rmax(tm):
    return -(-(tm + N_GROUPS * MOE_PAD + MOE_CH) // LANES) * LANES


def _moe_sort_kernel(hn_ref, cmb_ref, lts_ref, hs_ref, cs_ref, meta_ref, seg_ref, *, tm, rmax):
    cmb = cmb_ref[...]
    lanef = lax.broadcasted_iota(I32, (tm, LANES), 1).astype(F32)
    oh = jnp.where(lanef == cmb[:, N_EXPERTS:N_EXPERTS + 1], 1.0, 0.0)
    seen = jnp.zeros((1, LANES), F32)
    pres = []
    for sblk in range(tm // MOE_SB):
        ohs = oh[sblk * MOE_SB:(sblk + 1) * MOE_SB]
        pres.append(_dot(lts_ref[...], ohs.astype(BF16)) + seen)
        seen = seen + jnp.sum(ohs, axis=0, keepdims=True)
    pre = jnp.concatenate(pres, axis=0)
    cnt = jnp.broadcast_to(seen, (SUBLANES, LANES))
    padn = jnp.ceil(cnt * (1.0 / MOE_PAD)) * MOE_PAD
    lane8 = lax.broadcasted_iota(I32, (SUBLANES, LANES), 1)
    off = jnp.zeros((SUBLANES, LANES), F32)
    for k in range(1, N_GROUPS):
        off = off + jnp.where(lane8 >= k, pltpu.roll(padn, k, 1), 0.0)
    dest = jnp.sum(oh * (off[0:1] + pre), axis=-1, keepdims=True)
    destb = jnp.broadcast_to(dest, (tm, LANES))
    meta_ref[...] = destb
    row8 = lax.broadcasted_iota(I32, (SUBLANES, LANES), 0)
    seg_ref[0] = jnp.where(row8 == 0, off, jnp.where(row8 == 1, cnt, 0.0)).astype(I32)

    dest_row = destb.T[0:1]
    hn = hn_ref[...]
    chi = cmb.astype(BF16)
    clo = (cmb - chi.astype(F32)).astype(BF16)

    def blk(rb, carry):
        r0 = pl.multiple_of(rb * LANES, LANES)
        rows = (r0 + lax.broadcasted_iota(I32, (LANES, tm), 0)).astype(F32)
        p = jnp.where(rows == dest_row, 1.0, 0.0).astype(BF16)
        hs_ref[0, pl.ds(r0, LANES), :] = _dot(p, hn).astype(BF16)
        cs_ref[0, pl.ds(r0, LANES), :] = _dot(p, chi) + _dot(p, clo)
        return carry

    lax.fori_loop(0, rmax // LANES, blk, 0)


def moe_sort(hn, cmb, *, tm):
    t, d = hn.shape
    nt = t // tm
    rmax = _moe_rmax(tm)
    ri = np.arange(MOE_SB)
    lts = jnp.asarray((ri[None, :] < ri[:, None]).astype(np.float32), BF16)
    return pl.pallas_call(
        functools.partial(_moe_sort_kernel, tm=tm, rmax=rmax),
        grid=(nt,),
        in_specs=[pl.BlockSpec((tm, d), lambda i: (i, 0)), pl.BlockSpec((tm, LANES), lambda i: (i, 0)),
                  pl.BlockSpec((MOE_SB, MOE_SB), lambda i: (0, 0))],
        out_specs=[pl.BlockSpec((1, rmax, d), lambda i: (i, 0, 0)),
                   pl.BlockSpec((1, rmax, LANES), lambda i: (i, 0, 0)),
                   pl.BlockSpec((tm, LANES), lambda i: (i, 0)),
                   pl.BlockSpec((1, SUBLANES, LANES), lambda i: (i, 0, 0))],
        out_shape=[jax.ShapeDtypeStruct((nt, rmax, d), BF16), jax.ShapeDtypeStruct((nt, rmax, LANES), F32),
                   jax.ShapeDtypeStruct((t, LANES), F32), jax.ShapeDtypeStruct((nt, SUBLANES, LANES), I32)],
        compiler_params=_cparams(("parallel",)),
        name="moe_sort",
    )(hn, cmb, lts)


def _moe_expert_kernel(seg_ref, hs_ref, cs_ref, wg_ref, wu_ref, wd_ref, y_ref, acc_ref, *, ns):
    i = pl.program_id(0)
    e = pl.program_id(1)

    @pl.when(e == 0)
    def _():
        acc_ref[...] = jnp.zeros_like(acc_ref)

    g = e // EXPERTS_PER_GROUP
    for s in range(ns):
        base = (i * ns + s) * 2 * N_GROUPS
        off = seg_ref[base + g]
        n = seg_ref[base + N_GROUPS + g]

        def chunk(j, carry, s=s, off=off):
            r0 = pl.multiple_of(off + j * MOE_CH, MOE_PAD)
            hsl = hs_ref[s, pl.ds(r0, MOE_CH), :]
            gg = _dot(hsl, wg_ref[0])
            uu = _dot(hsl, wu_ref[0])
            c = cs_ref[s, pl.ds(r0, MOE_CH), :]
            lane = lax.broadcasted_iota(I32, c.shape, 1)
            cc = jnp.sum(jnp.where(lane == e, c, 0.0), axis=-1, keepdims=True)
            he = (gg * _sigmoid(gg)) * uu * cc
            acc_ref[s, pl.ds(r0, MOE_CH), :] += _dot(he.astype(BF16), wd_ref[0])
            return carry

        lax.fori_loop(0, (n + MOE_CH - 1) // MOE_CH, chunk, 0)

    @pl.when(e == pl.num_programs(1) - 1)
    def _():
        y_ref[...] = acc_ref[...].astype(BF16)


def moe_experts(seg, hs, cs, wg, wu, wd, *, ns):
    nt, rmax, d = hs.shape
    ne, _, f = wg.shape
    return pl.pallas_call(
        functools.partial(_moe_expert_kernel, ns=ns),
        grid_spec=pltpu.PrefetchScalarGridSpec(
            num_scalar_prefetch=1,
            grid=(nt // ns, ne),
            in_specs=[pl.BlockSpec((ns, rmax, d), lambda i, e, sref: (i, 0, 0)),
                      pl.BlockSpec((ns, rmax, LANES), lambda i, e, sref: (i, 0, 0)),
                      pl.BlockSpec((1, d, f), lambda i, e, sref: (e, 0, 0)),
                      pl.BlockSpec((1, d, f), lambda i, e, sref: (e, 0, 0)),
                      pl.BlockSpec((1, f, d), lambda i, e, sref: (e, 0, 0))],
            out_specs=pl.BlockSpec((ns, rmax, d), lambda i, e, sref: (i, 0, 0)),
            scratch_shapes=[pltpu.VMEM((ns, rmax, d), F32)]),
        out_shape=jax.ShapeDtypeStruct((nt, rmax, d), BF16),
        compiler_params=_cparams(("parallel", "arbitrary")),
        name="moe_experts",
    )(seg, hs, cs, wg, wu, wd)


def _moe_combine_kernel(x_ref, meta_ref, y_ref, gf_ref, o_ref, *, rmax, final_norm):
    tb = x_ref.shape[0]
    lanef = lax.broadcasted_iota(I32, (tb, rmax), 1).astype(F32)
    pt = jnp.where(lanef == meta_ref[:, 0:1], 1.0, 0.0).astype(BF16)
    out = x_ref[...] + _dot(pt, y_ref[0])
    o_ref[...] = _rms(out, gf_ref[...]) if final_norm else out


def moe_combine(x2, meta, y, g_final, *, tm, tb, final_norm):
    t, d = x2.shape
    nt, rmax, _ = y.shape
    nb = tm // tb
    return pl.pallas_call(
        functools.partial(_moe_combine_kernel, rmax=rmax, final_norm=final_norm),
        grid=(nt, nb),
        in_specs=[pl.BlockSpec((tb, d), lambda i, j: (i * nb + j, 0)),
                  pl.BlockSpec((tb, LANES), lambda i, j: (i * nb + j, 0)),
                  pl.BlockSpec((1, rmax, d), lambda i, j: (i, 0, 0)),
                  pl.BlockSpec((1, d), lambda i, j: (0, 0))],
        out_specs=pl.BlockSpec((tb, d), lambda i, j: (i * nb + j, 0)),
        out_shape=jax.ShapeDtypeStruct((t, d), F32),
        compiler_params=_cparams(("parallel", "arbitrary")),
        name="moe_combine",
    )(x2, meta, y, g_final)


def _permute_w_in(w):
    d = w.shape[0]
    o = 0
    seg = {}
    for name, width in (("c", 768), ("gla", 1024), ("ga", GLA_RANK), ("s5u", 256), ("dq", 256), ("dk", 64),
                        ("dv", 64), ("iq", 256), ("ik", IDX_DIM), ("iw", IDX_HEADS), ("gates", 4096)):
        seg[name] = w[:, o:o + width]
        o += width
    z = lambda n: jnp.zeros((d, n), w.dtype)
    cols = [seg["c"], seg["s5u"], seg["gla"], seg["dq"], seg["iq"], seg["dk"], seg["dv"],
            seg["ik"], seg["iw"], z(LANES - IDX_DIM - IDX_HEADS), seg["ga"], z(LANES - GLA_RANK),
            z(P_GATES - P_GA - LANES), seg["gates"]]
    out = jnp.concatenate(cols, axis=1).astype(BF16)
    assert out.shape[1] == P_TOTAL
    return out


def _s5_matrices(bb_re, bb_im, c_re, c_im):
    g, p, c = S5_GROUPS, S5_STATE, S5_GROUP
    n = g * p
    rows_g = jnp.arange(g * c) // c
    cols_g = jnp.arange(n) // p
    mask = (rows_g[:, None] == cols_g[None, :]).astype(F32)
    bm = jnp.concatenate([jnp.tile(bb_re, (g, 1)) * mask, jnp.tile(bb_im, (g, 1)) * mask], axis=1)
    ct = lambda a: jnp.tile(jnp.transpose(a, (0, 2, 1)).reshape(n, c), (1, g)) * mask.T
    cm = jnp.concatenate([ct(c_re), -ct(c_im)], axis=0)
    return bm.astype(BF16), cm.astype(BF16)


def _pick(s, pref):
    for c in pref:
        if s % c == 0:
            return c
    return s


def kernel(x, mem, positions, norm_mix, w_in, conv_w, conv_b, gla_a_up, gla_a_b, gla_norm, s5_lambda_re,
           s5_lambda_im, s5_log_dt, s5_b_re, s5_b_im, s5_c_re, s5_c_im, s5_d, s5_w_glu, s5_b_glu, w_branch,
           w_out, norm_cross, w_cq, w_ckv, w_co, norm_ffn, w_route_group, b_route_group, w_route_expert,
           b_route_expert, w_e_gate, w_e_up, w_e_down, norm_mem, norm_final):
    b, s, d = x.shape
    t = b * s
    m = mem.shape[1]
    depth = w_in.shape[0]
    topk = min(DSA_TOPK, s // 4)
    ts = _pick(s, (512, 256, 128))
    tm = _pick(t, (1024, 512, 256, 128))
    row = lambda a: a.reshape(1, -1)

    tabs = rope_tables(positions, ts=ts)
    x2 = x.reshape(t, d)
    mem2 = mem.reshape(b * m, d)
    for l in range(depth):
        proj = norm_matmul(x2, row(norm_mix[l]), _permute_w_in(w_in[l]), tm=_pick(t, (2048, 1024, 512)), tn=1024)
        proj3 = proj.reshape(b, s, P_TOTAL)
        ya = conv_branch(proj3, conv_w[l], row(conv_b[l]), ts=ts)
        a_up_p = jnp.concatenate(
            [gla_a_up[l], jnp.zeros((LANES - GLA_RANK, gla_a_up.shape[2]), F32)], axis=0).astype(BF16)
        yb = gla_branch(proj3, a_up_p, row(gla_a_b[l]), row(gla_norm[l]), tt=ts)
        bb_re, bb_im, tab = s5_params(s5_lambda_re[l], s5_lambda_im[l], s5_log_dt[l], s5_b_re[l], s5_b_im[l])
        bm, cm = _s5_matrices(bb_re, bb_im, s5_c_re[l], s5_c_im[l])
        yc = s5_branch(proj3, bm, tab, cm, row(s5_d[l]), s5_w_glu[l].astype(BF16), row(s5_b_glu[l]), tt=ts)
        qs, qis, kr, vt4, kir, wt = dsa_prep(proj3, tabs, ts=ts)
        yd = dsa_attend(qs, qis, wt, kr, vt4, kir, qb=_pick(s, (2 * Q_BLOCK, Q_BLOCK)), kt=ts, topk=topk)
        w2 = lambda a: a.reshape(t, a.shape[-1])
        x2 = merge_out(x2, w2(ya), w2(yb), w2(yc), w2(yd), proj, w_branch[l].astype(BF16),
                       w_out[l].astype(BF16), tm=min(tm, 512))
        kv = norm_matmul(mem2, row(norm_mem), w_ckv[l].astype(BF16), tm=_pick(b * m, (1024, 512, 256)), tn=1024)
        x2 = cross_attn(x2.reshape(b, s, d), row(norm_cross[l]), w_cq[l].astype(BF16),
                        kv.reshape(b, m, -1), w_co[l].astype(BF16), tm=ts).reshape(t, d)
        pad = jnp.zeros((d, LANES - N_EXPERTS - N_GROUPS), F32)
        wr = jnp.concatenate([w_route_expert[l], w_route_group[l], pad], axis=1)
        br = jnp.concatenate([b_route_expert[l], b_route_group[l], pad[0]], axis=0).reshape(1, LANES)
        hn, cmb = moe_router(x2, row(norm_ffn[l]), wr, br, tm=min(tm, 512))
        tmoe = _pick(t, (1024, 512))
        hs, cs, meta, seg = moe_sort(hn, cmb, tm=tmoe)
        seg1 = jnp.concatenate([seg[:, 0, :N_GROUPS], seg[:, 1, :N_GROUPS]], axis=1).reshape(-1)
        y = moe_experts(seg1, hs, cs, w_e_gate[l].astype(BF16), w_e_up[l].astype(BF16), w_e_down[l].astype(BF16),
                        ns=2 if (t // tmoe) % 2 == 0 else 1)
        x2 = moe_combine(x2, meta, y, row(norm_final), tm=tmoe, tb=min(tmoe, 512), final_norm=(l == depth - 1))
    return x2.reshape(b, s, d)
```

```python
import functools
import math

import numpy as np
import jax
import jax.numpy as jnp
from jax import lax
from jax.experimental import pallas as pl
from jax.experimental.pallas import tpu as pltpu

F32 = jnp.float32
BF16 = jnp.bfloat16
I32 = jnp.int32

D_MODEL = 1024
DEPTH = 2
EPS = 1e-6
N_BRANCH = 4
BRANCH_W = 256
CONV_W = 3
GLA_HEADS = 4
GLA_DK = 64
GLA_DV = 64
GLA_RANK = 16
GLA_TAU = 16.0
GLA_CHUNK = 64
S5_GROUP = 16
S5_GROUPS = BRANCH_W // S5_GROUP
S5_STATE = 64
DSA_HEADS = 4
DSA_HEAD_DIM = 64
IDX_HEADS = 8
IDX_DIM = 32
DSA_TOPK = 256
Q_BLOCK = 128
ROPE_THETA = 500000.0
ROPE_FRAC = 4
X_HEADS = 4
X_HEAD_DIM = 128
N_GROUPS = 4
EXPERTS_PER_GROUP = 4
N_EXPERTS = N_GROUPS * EXPERTS_PER_GROUP
D_FF_EXPERT = 512

LANES = 128
SUBLANES = 8
VMEM_LIMIT = 48 * 1024 * 1024

P_CONV = 0
P_S5U = 768
P_GLA = 1024
P_DQ = 2048
P_IQ = 2304
P_KV = 2560
P_IKW = 2688
P_GA = 2816
P_GATES = 3072
P_TOTAL = P_GATES + N_BRANCH * D_MODEL

INT_MIN = -2147483648
NEG_INF_KEY = INT_MIN + 0x7FFFFF
VT_ROWS = DSA_HEAD_DIM + 2 * SUBLANES


def _cparams(sem):
    return pltpu.CompilerParams(dimension_semantics=sem, vmem_limit_bytes=VMEM_LIMIT)


def _dot(a, b):
    return jnp.dot(a, b, preferred_element_type=F32)


def _dot_nt(a, b):
    return lax.dot_general(a, b, (((1,), (1,)), ((), ())), preferred_element_type=F32)


def _dot_tn(a, b):
    return lax.dot_general(a, b, (((0,), (0,)), ((), ())), preferred_element_type=F32)


def _split_dot(exact_bf16, x):
    hi = x.astype(BF16)
    lo = (x - hi.astype(F32)).astype(BF16)
    return _dot(exact_bf16, hi) + _dot(exact_bf16, lo)


def _split_dot_r(x, exact_bf16):
    hi = x.astype(BF16)
    lo = (x - hi.astype(F32)).astype(BF16)
    return _dot(hi, exact_bf16) + _dot(lo, exact_bf16)


def _rms(x, g):
    return x * lax.rsqrt(jnp.mean(x * x, axis=-1, keepdims=True) + EPS) * g


def _sigmoid(x):
    return 1.0 / (1.0 + jnp.exp(-x))


def _norm_matmul_kernel(x_ref, g_ref, w_ref, o_ref, hn_ref):
    @pl.when(pl.program_id(1) == 0)
    def _():
        hn_ref[...] = _rms(x_ref[...], g_ref[...]).astype(BF16)

    o_ref[...] = _dot(hn_ref[...], w_ref[...]).astype(o_ref.dtype)


def norm_matmul(x, g, w, *, tm, tn, out_dtype=BF16):
    t, d = x.shape
    n = w.shape[1]
    return pl.pallas_call(
        _norm_matmul_kernel,
        grid=(t // tm, n // tn),
        in_specs=[
            pl.BlockSpec((tm, d), lambda i, j: (i, 0)),
            pl.BlockSpec((1, d), lambda i, j: (0, 0)),
            pl.BlockSpec((d, tn), lambda i, j: (0, j)),
        ],
        out_specs=pl.BlockSpec((tm, tn), lambda i, j: (i, j)),
        out_shape=jax.ShapeDtypeStruct((t, n), out_dtype),
        scratch_shapes=[pltpu.VMEM((tm, d), BF16)],
        compiler_params=_cparams(("parallel", "arbitrary")),
        name="norm_matmul",
    )(x, g, w)


def _conv_kernel(cur_ref, prev_ref, w_ref, b_ref, o_ref):
    i = pl.program_id(1)
    w_ = BRANCH_W
    cur = cur_ref[0].astype(F32)
    u = cur[:, 2 * w_:3 * w_] * cur[:, 0:w_]
    pv = prev_ref[0].astype(F32)
    pu = pv[:, 2 * w_:3 * w_] * pv[:, 0:w_]
    pu = jnp.where(i > 0, pu, 0.0)
    row = lax.broadcasted_iota(I32, u.shape, 0)
    u1 = jnp.where(row == 0, pu[7:8], pltpu.roll(u, 1, 0))
    u2 = jnp.where(row == 0, pu[6:7], jnp.where(row == 1, pu[7:8], pltpu.roll(u, 2, 0)))
    w = w_ref[...]
    y = w[0:1] * u2 + w[1:2] * u1 + w[2:3] * u + b_ref[...]
    o_ref[0] = (cur[:, w_:2 * w_] * y).astype(o_ref.dtype)


def conv_branch(proj3, conv_w, conv_b, *, ts):
    b, s, _ = proj3.shape
    wc = 3 * BRANCH_W
    hb = ts // SUBLANES
    return pl.pallas_call(
        _conv_kernel,
        grid=(b, s // ts),
        in_specs=[
            pl.BlockSpec((1, ts, wc), lambda bi, i: (bi, i, P_CONV // wc)),
            pl.BlockSpec((1, SUBLANES, wc), lambda bi, i: (bi, jnp.maximum(i * hb - 1, 0), P_CONV // wc)),
            pl.BlockSpec((CONV_W, BRANCH_W), lambda bi, i: (0, 0)),
            pl.BlockSpec((1, BRANCH_W), lambda bi, i: (0, 0)),
        ],
        out_specs=pl.BlockSpec((1, ts, BRANCH_W), lambda bi, i: (bi, i, 0)),
        out_shape=jax.ShapeDtypeStruct((b, s, BRANCH_W), BF16),
        compiler_params=_cparams(("parallel", "arbitrary")),
        name="conv_branch",
    )(proj3, proj3, conv_w, conv_b)


def _gla_kernel(g_ref, a_ref, aup_ref, ab_ref, gn_ref, ltb_ref, ltf_ref, bob_ref, bd_ref, bdb_ref,
                o_ref, st_ref, *, tt):
    hw = GLA_HEADS * GLA_DK
    c = GLA_CHUNK

    @pl.when(pl.program_id(1) == 0)
    def _():
        st_ref[...] = jnp.zeros_like(st_ref)

    blk = g_ref[0]
    q = blk[:, 0:hw].astype(F32)
    k = blk[:, hw:2 * hw].astype(F32)
    vb = blk[:, 2 * hw:3 * hw]
    v = vb.astype(F32)
    r = blk[:, 3 * hw:4 * hw].astype(F32)

    pre = _dot(a_ref[0], aup_ref[...]) + ab_ref[...]
    la = (jnp.minimum(pre, 0.0) - jnp.log(1.0 + jnp.exp(-jnp.abs(pre)))) * (1.0 / GLA_TAU)
    cum = _split_dot(ltb_ref[...], la)
    tot = _split_dot(bob_ref[...], la)
    q_dec = q * (GLA_DK ** -0.5) * jnp.exp(cum)
    k_inv = (k * jnp.exp(-cum)).astype(BF16)
    k_end = (k * jnp.exp(tot - cum)).astype(BF16)
    qdb = q_dec.astype(BF16)

    lane = lax.broadcasted_iota(I32, (1, hw), 1)
    ltmask = ltf_ref[...] > 0.0
    o = jnp.zeros((tt, hw), F32)
    for h in range(GLA_HEADS):
        hm = (lane // GLA_DK) == h
        qh = jnp.where(hm, q_dec, 0.0).astype(BF16)
        att = jnp.where(ltmask, _dot_nt(qh, k_inv), 0.0)
        vh = jnp.where(hm, v, 0.0).astype(BF16)
        o = o + _dot(att.astype(BF16), vh)

    st = st_ref[...]
    bd = bd_ref[...]
    inter = []
    for n in range(tt // c):
        rows = slice(n * c, (n + 1) * c)
        inter.append(_dot_nt(qdb[rows], st.astype(BF16)))
        dec = jnp.exp(tot[n * c:n * c + 1, :])
        st = st * dec + _dot_tn(vb[rows], k_end[rows]) * bd
    st_ref[...] = st
    o = o + jnp.concatenate(inter, axis=0)

    msq = _split_dot_r(o * o, bdb_ref[...]) * (1.0 / GLA_DV)
    y = o * lax.rsqrt(msq + EPS) * gn_ref[...] * (r * _sigmoid(r))
    o_ref[0] = y.astype(o_ref.dtype)


def _gla_consts(tt):
    ri = np.arange(tt)[:, None]
    ci = np.arange(tt)[None, :]
    same = (ri // GLA_CHUNK) == (ci // GLA_CHUNK)
    lt = (same & (ci <= ri)).astype(np.float32)
    hw = GLA_HEADS * GLA_DK
    hi = np.arange(hw)
    bd = ((hi[:, None] // GLA_DK) == (hi[None, :] // GLA_DK)).astype(np.float32)
    return (jnp.asarray(lt, BF16), jnp.asarray(lt, F32), jnp.asarray(same.astype(np.float32), BF16),
            jnp.asarray(bd, F32), jnp.asarray(bd, BF16))


def gla_branch(proj3, a_up_p, a_b, g_norm, *, tt):
    b, s, _ = proj3.shape
    hw = GLA_HEADS * GLA_DK
    ltb, ltf, bob, bd, bdb = _gla_consts(tt)
    const = lambda shape: pl.BlockSpec(shape, lambda bi, i: (0,) * len(shape))
    return pl.pallas_call(
        functools.partial(_gla_kernel, tt=tt),
        grid=(b, s // tt),
        in_specs=[
            pl.BlockSpec((1, tt, 4 * hw), lambda bi, i: (bi, i, P_GLA // (4 * hw))),
            pl.BlockSpec((1, tt, LANES), lambda bi, i: (bi, i, P_GA // LANES)),
            const((LANES, hw)), const((1, hw)), const((1, hw)),
            const((tt, tt)), const((tt, tt)), const((tt, tt)), const((hw, hw)), const((hw, hw)),
        ],
        out_specs=pl.BlockSpec((1, tt, hw), lambda bi, i: (bi, i, 0)),
        out_shape=jax.ShapeDtypeStruct((b, s, hw), BF16),
        scratch_shapes=[pltpu.VMEM((hw, hw), F32)],
        compiler_params=_cparams(("parallel", "arbitrary")),
        name="gla_branch",
    )(proj3, proj3, a_up_p, a_b, g_norm, ltb, ltf, bob, bd, bdb)


def _s5_param_kernel(lre_ref, lim_ref, ldt_ref, bre_ref, bim_ref, bbre_ref, bbim_ref, tab_ref):
    lre = lre_ref[...]
    lim = lim_ref[...]
    dt = jnp.exp(ldt_ref[...])
    mag = jnp.exp(lre * dt)
    lbr = mag * jnp.cos(lim * dt)
    lbi = mag * jnp.sin(lim * dt)
    den = lre * lre + lim * lim
    fre = ((lbr - 1.0) * lre + lbi * lim) / den
    fim = (lbi * lre - (lbr - 1.0) * lim) / den
    bre = bre_ref[...]
    bim = bim_ref[...]
    bbre_ref[...] = fre * bre - fim * bim
    bbim_ref[...] = fre * bim + fim * bre

    pw = [None, (lbr, lbi)]
    for _ in range(2, SUBLANES + 1):
        pr, pi = pw[-1]
        pw.append((pr * lbr - pi * lbi, pr * lbi + pi * lbr))
    n = lre.shape[-1]
    row = lax.broadcasted_iota(I32, (SUBLANES, n), 0)
    zero = jnp.zeros((SUBLANES, n), F32)
    for idx, sft in enumerate((1, 2, 4)):
        tab_ref[2 * idx] = jnp.where(row >= sft, jnp.broadcast_to(pw[sft][0], (SUBLANES, n)), zero)
        tab_ref[2 * idx + 1] = jnp.where(row >= sft, jnp.broadcast_to(pw[sft][1], (SUBLANES, n)), zero)
    cr, ci = zero, zero
    for rr in range(SUBLANES):
        cr = jnp.where(row == rr, jnp.broadcast_to(pw[rr + 1][0], (SUBLANES, n)), cr)
        ci = jnp.where(row == rr, jnp.broadcast_to(pw[rr + 1][1], (SUBLANES, n)), ci)
    tab_ref[6] = cr
    tab_ref[7] = ci


def s5_params(lam_re, lam_im, log_dt, b_re, b_im):
    g, p = lam_re.shape
    n = g * p
    row = lambda a: a.reshape(1, n)
    ldt = jnp.broadcast_to(log_dt[:, None], (g, p))
    bt = lambda a: jnp.transpose(a, (2, 0, 1)).reshape(S5_GROUP, n)
    full = lambda shape: pl.BlockSpec(shape, lambda: (0,) * len(shape))
    return pl.pallas_call(
        _s5_param_kernel,
        in_specs=[full((1, n))] * 3 + [full((S5_GROUP, n))] * 2,
        out_specs=[full((S5_GROUP, n)), full((S5_GROUP, n)), full((8, SUBLANES, n))],
        out_shape=[jax.ShapeDtypeStruct((S5_GROUP, n), F32), jax.ShapeDtypeStruct((S5_GROUP, n), F32),
                   jax.ShapeDtypeStruct((8, SUBLANES, n), F32)],
        name="s5_params",
    )(row(lam_re), row(lam_im), row(ldt), bt(b_re), bt(b_im))


def _s5_kernel(u_ref, bm_ref, tab_ref, cm_ref, d_ref, wg_ref, bg_ref, o_ref, xs_ref, car_ref, *, tt, n):
    @pl.when(pl.program_id(1) == 0)
    def _():
        car_ref[...] = jnp.zeros_like(car_ref)

    ub = u_ref[0]
    xs_ref[...] = _dot(ub, bm_ref[...])

    def group(gi, carry):
        r0 = pl.multiple_of(gi * SUBLANES, SUBLANES)
        for j in range(n // LANES):
            cre = slice(j * LANES, (j + 1) * LANES)
            cim = slice(n + j * LANES, n + (j + 1) * LANES)
            re = xs_ref[pl.ds(r0, SUBLANES), cre]
            im = xs_ref[pl.ds(r0, SUBLANES), cim]
            for idx, sft in enumerate((1, 2, 4)):
                ar = tab_ref[2 * idx, :, cre]
                ai = tab_ref[2 * idx + 1, :, cre]
                sr = pltpu.roll(re, sft, 0)
                si = pltpu.roll(im, sft, 0)
                re, im = re + ar * sr - ai * si, im + ar * si + ai * sr
            pr = tab_ref[6, :, cre]
            pi = tab_ref[7, :, cre]
            cr = car_ref[0, :, cre]
            ci = car_ref[1, :, cre]
            re, im = re + pr * cr - pi * ci, im + pr * ci + pi * cr
            xs_ref[pl.ds(r0, SUBLANES), cre] = re
            xs_ref[pl.ds(r0, SUBLANES), cim] = im
            car_ref[0, :, cre] = jnp.broadcast_to(re[SUBLANES - 1:SUBLANES], (SUBLANES, LANES))
            car_ref[1, :, cre] = jnp.broadcast_to(im[SUBLANES - 1:SUBLANES], (SUBLANES, LANES))
        return carry

    lax.fori_loop(0, tt // SUBLANES, group, 0)

    y = _dot(xs_ref[...].astype(BF16), cm_ref[...]) + d_ref[...] * ub.astype(F32)
    y = 0.5 * y * (1.0 + jnp.tanh(math.sqrt(2.0 / math.pi) * (y + 0.044715 * (y * y * y))))
    z = _dot(y.astype(BF16), wg_ref[...]) + bg_ref[...]
    o_ref[0] = (y * _sigmoid(z)).astype(o_ref.dtype)


def s5_branch(proj3, bmat, tab, cmat, d_skip, w_glu, b_glu, *, tt):
    b, s, _ = proj3.shape
    w_ = BRANCH_W
    n = S5_GROUPS * S5_STATE
    const = lambda shape: pl.BlockSpec(shape, lambda bi, i: (0,) * len(shape))
    return pl.pallas_call(
        functools.partial(_s5_kernel, tt=tt, n=n),
        grid=(b, s // tt),
        in_specs=[
            pl.BlockSpec((1, tt, w_), lambda bi, i: (bi, i, P_S5U // w_)),
            const((w_, 2 * n)), const((8, SUBLANES, n)), const((2 * n, w_)),
            const((1, w_)), const((w_, w_)), const((1, w_)),
        ],
        out_specs=pl.BlockSpec((1, tt, w_), lambda bi, i: (bi, i, 0)),
        out_shape=jax.ShapeDtypeStruct((b, s, w_), BF16),
        scratch_shapes=[pltpu.VMEM((tt, 2 * n), F32), pltpu.VMEM((2, SUBLANES, n), F32)],
        compiler_params=_cparams(("parallel", "arbitrary")),
        name="s5_branch",
    )(proj3, bmat, tab, cmat, d_skip, w_glu, b_glu)


def _rope_freq_rows():
    rows = np.zeros((8, LANES), np.float32)
    for pat, dh in enumerate((DSA_HEAD_DIM, IDX_DIM)):
        rd = dh // ROPE_FRAC
        half = rd // 2
        inv = (np.float32(ROPE_THETA) ** (-np.arange(half, dtype=np.float32) * np.float32(2.0 / rd))).astype(np.float32)
        for l in range(LANES):
            i = l % dh
            if i < half:
                rows[3 * pat, l] = inv[i]
                rows[3 * pat + 1, l] = -1.0
            elif i < rd:
                rows[3 * pat, l] = inv[i - half]
                rows[3 * pat + 2, l] = 1.0
    return rows


def _rope_tab_kernel(pos_ref, fr_ref, o_ref):
    pos = pos_ref[0]
    fr = fr_ref[...]
    for pat in range(2):
        ang = pos * fr[3 * pat:3 * pat + 1]
        c = jnp.cos(ang)
        s = jnp.sin(ang)
        o_ref[0, 3 * pat] = c
        o_ref[0, 3 * pat + 1] = s * fr[3 * pat + 1:3 * pat + 2]
        o_ref[0, 3 * pat + 2] = s * fr[3 * pat + 2:3 * pat + 3]


def rope_tables(positions, *, ts):
    b, s = positions.shape
    pos = positions.astype(F32).reshape(b, s, 1)
    fr = jnp.asarray(_rope_freq_rows())
    return pl.pallas_call(
        _rope_tab_kernel,
        grid=(b, s // ts),
        in_specs=[pl.BlockSpec((1, ts, 1), lambda bi, i: (bi, i, 0)),
                  pl.BlockSpec((8, LANES), lambda bi, i: (0, 0))],
        out_specs=pl.BlockSpec((1, 6, ts, LANES), lambda bi, i: (bi, 0, i, 0)),
        out_shape=jax.ShapeDtypeStruct((b, 6, s, LANES), F32),
        compiler_params=_cparams(("parallel", "arbitrary")),
        name="rope_tables",
    )(pos, fr)


def _rope(t, c, sm, sp, half):
    n = t.shape[-1]
    return t * c + sm * pltpu.roll(t, n - half, 1) + sp * pltpu.roll(t, half, 1)


def _dsa_prep_kernel(dq_ref, iq_ref, kv_ref, ikw_ref, tab_ref,
                     qs_ref, qis_ref, kr_ref, vt_ref, kir_ref, wt_ref):
    two = lambda a: jnp.concatenate([a, a], axis=1)
    c1, sm1, sp1 = tab_ref[0, 0], tab_ref[0, 1], tab_ref[0, 2]
    c2, sm2, sp2 = tab_ref[0, 3], tab_ref[0, 4], tab_ref[0, 5]
    h1 = DSA_HEAD_DIM // ROPE_FRAC // 2
    h2 = IDX_DIM // ROPE_FRAC // 2
    lane = lax.broadcasted_iota(I32, (1, LANES), 1)

    q = _rope(dq_ref[0].astype(F32), two(c1), two(sm1), two(sp1), h1) * (DSA_HEAD_DIM ** -0.5 * math.log2(math.e))
    for h in range(DSA_HEADS):
        qs_ref[0, h] = q[:, h * DSA_HEAD_DIM:(h + 1) * DSA_HEAD_DIM].astype(BF16)
    qi = _rope(iq_ref[0].astype(F32), two(c2), two(sm2), two(sp2), h2)
    for h in range(IDX_HEADS):
        qis_ref[0, h] = qi[:, h * IDX_DIM:(h + 1) * IDX_DIM].astype(BF16)

    kv = kv_ref[0].astype(F32)
    isk = lane < DSA_HEAD_DIM
    kvr = _rope(kv, jnp.where(isk, c1, 1.0), jnp.where(isk, sm1, 0.0), jnp.where(isk, sp1, 0.0), h1)
    kr_ref[0] = kvr[:, 0:DSA_HEAD_DIM].astype(BF16)
    ones = jnp.ones((VT_ROWS - DSA_HEAD_DIM, kv.shape[0]), BF16)
    vt_ref[0, 0] = jnp.concatenate([kvr.T[DSA_HEAD_DIM:2 * DSA_HEAD_DIM].astype(BF16), ones], axis=0)

    ikw = ikw_ref[0].astype(F32)
    isi = lane < IDX_DIM
    ikr = _rope(ikw, jnp.where(isi, c2, 1.0), jnp.where(isi, sm2, 0.0), jnp.where(isi, sp2, 0.0), h2)
    kir_ref[0] = ikr[:, 0:IDX_DIM].astype(BF16)
    wt_ref[0] = ikr.T[IDX_DIM:IDX_DIM + IDX_HEADS] * ((IDX_HEADS ** -0.5) * (IDX_DIM ** -0.5))


def dsa_prep(proj3, tabs, *, ts):
    b, s, _ = proj3.shape
    qw = DSA_HEADS * DSA_HEAD_DIM
    iw = IDX_HEADS * IDX_DIM
    return pl.pallas_call(
        _dsa_prep_kernel,
        grid=(b, s // ts),
        in_specs=[
            pl.BlockSpec((1, ts, qw), lambda bi, i: (bi, i, P_DQ // qw)),
            pl.BlockSpec((1, ts, iw), lambda bi, i: (bi, i, P_IQ // iw)),
            pl.BlockSpec((1, ts, LANES), lambda bi, i: (bi, i, P_KV // LANES)),
            pl.BlockSpec((1, ts, LANES), lambda bi, i: (bi, i, P_IKW // LANES)),
            pl.BlockSpec((1, 6, ts, LANES), lambda bi, i: (bi, 0, i, 0)),
        ],
        out_specs=[
            pl.BlockSpec((1, DSA_HEADS, ts, DSA_HEAD_DIM), lambda bi, i: (bi, 0, i, 0)),
            pl.BlockSpec((1, IDX_HEADS, ts, IDX_DIM), lambda bi, i: (bi, 0, i, 0)),
            pl.BlockSpec((1, ts, DSA_HEAD_DIM), lambda bi, i: (bi, i, 0)),
            pl.BlockSpec((1, 1, VT_ROWS, ts), lambda bi, i: (bi, i, 0, 0)),
            pl.BlockSpec((1, ts, IDX_DIM), lambda bi, i: (bi, i, 0)),
            pl.BlockSpec((1, IDX_HEADS, ts), lambda bi, i: (bi, 0, i)),
        ],
        out_shape=[
            jax.ShapeDtypeStruct((b, DSA_HEADS, s, DSA_HEAD_DIM), BF16),
            jax.ShapeDtypeStruct((b, IDX_HEADS, s, IDX_DIM), BF16),
            jax.ShapeDtypeStruct((b, s, DSA_HEAD_DIM), BF16),
            jax.ShapeDtypeStruct((b, s // ts, VT_ROWS, ts), BF16),
            jax.ShapeDtypeStruct((b, s, IDX_DIM), BF16),
            jax.ShapeDtypeStruct((b, IDX_HEADS, s), F32),
        ],
        compiler_params=_cparams(("parallel", "arbitrary")),
        name="dsa_prep",
    )(proj3, proj3, proj3, proj3, tabs)


def _dsa_kernel(q_ref, qi_ref, w_ref, k_ref, vt_ref, ki_ref, lts_ref, o_ref, sc_ref, lg_ref, *, qb, kt, topk):
    i = pl.program_id(1)
    q0 = i * qb
    nkt = (q0 + qb + kt - 1) // kt
    kf = float(topk)
    sg = 8 * SUBLANES
    wrow = w_ref[0]
    tq = q0 + lax.broadcasted_iota(I32, (1, qb), 1)

    def to_key(x):
        bits = pltpu.bitcast(x, I32)
        return bits ^ ((bits >> 31) & 0x7FFFFFFF)

    def scores(t, gmax):
        k0 = pl.multiple_of(t * kt, kt)
        sb = LANES
        gm = [gmax[j * sb:(j + 1) * sb] for j in range(topk // sb)]
        for r in range(kt // sb):
            kit = ki_ref[0, pl.ds(k0 + r * sb, sb), :]
            acc = jnp.zeros((sb, qb), F32)
            for h in range(IDX_HEADS):
                acc = acc + jnp.maximum(_dot_nt(kit, qi_ref[0, h]), 0.0) * wrow[h:h + 1]
            acc = jnp.where(acc == 0.0, 0.0, acc)
            causal = (k0 + r * sb + lax.broadcasted_iota(I32, (sb, qb), 0)) <= tq
            sc_ref[t, r * sb:(r + 1) * sb, :] = jnp.where(causal, to_key(acc), INT_MIN)
            j = ((r * sb) % topk) // sb
            gm[j] = jnp.maximum(gm[j], jnp.where(causal, acc, -jnp.inf))
        return jnp.concatenate(gm, axis=0)

    gmax = lax.fori_loop(0, nkt, scores, jnp.full((topk, qb), -jnp.inf, F32))

    def count(pred):
        def body(t, c):
            m = jnp.where(pred(sc_ref[t]), 1.0, 0.0)
            return c + jnp.sum(m.reshape(kt // sg, sg, qb), axis=0)
        c = lax.fori_loop(0, nkt, body, jnp.zeros((sg, qb), F32))
        return jnp.sum(c, axis=0, keepdims=True)

    need = (tq + 1) > topk
    lo0 = to_key(jnp.min(gmax, axis=0, keepdims=True))
    hi0 = to_key(jnp.max(gmax, axis=0, keepdims=True))
    done0 = jnp.where(jnp.logical_and(need, lo0 < hi0), 0.0, 1.0)

    def bs_cond(c):
        it, _, _, _, done, _ = c
        return jnp.logical_and(it < 34, jnp.sum(1.0 - done) > 0.0)

    def bs_step(lo, hi, thr, done, below, probe=None):
        mid = (lo >> 1) + (hi >> 1) + (((lo & 1) + (hi & 1) + 1) >> 1)
        if probe is not None:
            mid = jnp.where(jnp.logical_and(lo < probe, probe <= hi), probe, mid)
        cnt = count(lambda kk: kk >= mid)
        ge = cnt >= kf
        hit = cnt == kf
        near = cnt == kf - 1.0
        lo = jnp.where(ge, mid, lo)
        hi = jnp.where(ge, hi, mid - 1)
        fin = jnp.where(hit, 1.0, jnp.where(near, 1.0, jnp.where(lo == hi, 1.0, 0.0)))
        val = jnp.where(hit, mid, jnp.where(near, mid - 1, lo))
        live = done == 0.0
        thr = jnp.where(live, val, thr)
        below = jnp.where(live, jnp.where(near, 1.0, 0.0), below)
        return lo, hi, thr, jnp.maximum(done, fin), below

    def bs_body(c):
        it = c[0]
        return (it + 2,) + bs_step(*bs_step(*c[1:]))

    st = bs_step(*bs_step(lo0, hi0, lo0, done0, jnp.zeros((1, qb), F32), probe=1), probe=0)
    _, _, _, thr, _, below = lax.while_loop(bs_cond, bs_body, (jnp.int32(2),) + st)

    def below_max(t, c):
        kk = sc_ref[t]
        cand = jnp.where(kk <= thr, kk, NEG_INF_KEY)
        return jnp.maximum(c, jnp.max(cand.reshape(kt // sg, sg, qb), axis=0))

    bm = lax.fori_loop(0, nkt, below_max, jnp.full((sg, qb), NEG_INF_KEY, I32))
    bmax = to_key(jnp.max(pltpu.bitcast(bm ^ ((bm >> 31) & 0x7FFFFFFF), F32), axis=0, keepdims=True))
    thr = jnp.where(below > 0.0, bmax, thr)
    thr = jnp.where(need, thr, INT_MIN + 1)

    cge = count(lambda kk: kk >= thr)
    tie = jnp.logical_and(need, cge > kf)

    @pl.when(jnp.sum(jnp.where(tie, 1.0, 0.0)) > 0.0)
    def _():
        cgt = count(lambda kk: kk > thr)
        room = kf - cgt

        def body(t, seen):
            kk = sc_ref[t]
            eq = jnp.logical_and(kk == thr, tie)
            eqf = jnp.where(eq, 1.0, 0.0)
            before = _dot(lts_ref[...], eqf.astype(BF16)) + seen
            sc_ref[t] = jnp.where(jnp.logical_and(eq, before >= room), INT_MIN, kk)
            return seen + jnp.sum(eqf, axis=0, keepdims=True)

        lax.fori_loop(0, nkt, body, jnp.zeros((1, qb), F32))

    nh = DSA_HEADS

    def logits(t, c):
        k0 = pl.multiple_of(t * kt, kt)
        kt_ = k_ref[0, pl.ds(k0, kt), :]
        bias = jnp.where(sc_ref[t] >= thr, 0.0, -jnp.inf)
        out = []
        for h in range(nh):
            lg = _dot_nt(kt_, q_ref[0, h]) + bias
            lg_ref[t, h] = lg
            out.append(jnp.maximum(c[h], jnp.max(lg.reshape(kt // sg, sg, qb), axis=0)))
        return tuple(out)

    mparts = lax.fori_loop(0, nkt, logits, (jnp.full((sg, qb), -jnp.inf, F32),) * nh)
    ms = []
    for h in range(nh):
        m = jnp.max(mparts[h], axis=0, keepdims=True)
        ms.append(jnp.where(m == -jnp.inf, 0.0, m))

    def attend(t, c):
        vt = vt_ref[0, t]
        out = []
        for h in range(nh):
            p = jnp.exp2((lg_ref[t, h] - ms[h]).astype(BF16))
            out.append(c[h] + _dot(vt, p))
        return tuple(out)

    res = lax.fori_loop(0, nkt, attend, (jnp.zeros((vt_ref.shape[2], qb), F32),) * nh)
    dh = DSA_HEAD_DIM
    for h in range(nh):
        ot = res[h][0:dh] / res[h][dh:dh + 1]
        o_ref[0, :, h * dh:(h + 1) * dh] = ot.T.astype(o_ref.dtype)


def dsa_attend(qs, qis, wt, kr, vt4, kir, *, qb, kt, topk):
    b, nh, s, dh = qs.shape
    nkt = s // kt
    assert topk % LANES == 0 and kt % topk == 0 and s % kt == 0 and s % qb == 0
    ri = np.arange(kt)
    lts = jnp.asarray((ri[None, :] < ri[:, None]).astype(np.float32), BF16)
    return pl.pallas_call(
        functools.partial(_dsa_kernel, qb=qb, kt=kt, topk=topk),
        grid=(b, s // qb),
        in_specs=[
            pl.BlockSpec((1, nh, qb, dh), lambda bi, i: (bi, 0, i, 0)),
            pl.BlockSpec((1, IDX_HEADS, qb, IDX_DIM), lambda bi, i: (bi, 0, i, 0)),
            pl.BlockSpec((1, IDX_HEADS, qb), lambda bi, i: (bi, 0, i)),
            pl.BlockSpec((1, s, dh), lambda bi, i: (bi, 0, 0)),
            pl.BlockSpec((1, nkt, VT_ROWS, kt), lambda bi, i: (bi, 0, 0, 0)),
            pl.BlockSpec((1, s, IDX_DIM), lambda bi, i: (bi, 0, 0)),
            pl.BlockSpec((kt, kt), lambda bi, i: (0, 0)),
        ],
        out_specs=pl.BlockSpec((1, qb, nh * dh), lambda bi, i: (bi, i, 0)),
        out_shape=jax.ShapeDtypeStruct((b, s, nh * dh), BF16),
        scratch_shapes=[pltpu.VMEM((nkt, kt, qb), I32), pltpu.VMEM((nkt, nh, kt, qb), F32)],
        compiler_params=_cparams(("parallel", "arbitrary")),
        name="dsa_attend",
    )(qs, qis, wt, kr, vt4, kir, lts)


def _merge_kernel(x_ref, ya_ref, yb_ref, yc_ref, yd_ref, g0_ref, g1_ref, g2_ref, g3_ref, wb_ref, wo_ref, o_ref):
    ys = (ya_ref, yb_ref, yc_ref, yd_ref)
    gs = (g0_ref, g1_ref, g2_ref, g3_ref)
    merged = jnp.zeros(o_ref.shape, F32)
    for i in range(N_BRANCH):
        merged = merged + _sigmoid(gs[i][...].astype(F32)) * _dot(ys[i][...], wb_ref[i])
    o_ref[...] = x_ref[...] + _dot(merged.astype(BF16), wo_ref[...])


def merge_out(x2, ya, yb, yc, yd, proj, w_branch, w_out, *, tm):
    t, d = x2.shape
    w_ = BRANCH_W
    yspec = pl.BlockSpec((tm, w_), lambda i: (i, 0))
    gspec = lambda n: pl.BlockSpec((tm, d), lambda i: (i, P_GATES // d + n))
    return pl.pallas_call(
        _merge_kernel,
        grid=(t // tm,),
        in_specs=[pl.BlockSpec((tm, d), lambda i: (i, 0)), yspec, yspec, yspec, yspec,
                  gspec(0), gspec(1), gspec(2), gspec(3),
                  pl.BlockSpec((N_BRANCH, w_, d), lambda i: (0, 0, 0)),
                  pl.BlockSpec((d, d), lambda i: (0, 0))],
        out_specs=pl.BlockSpec((tm, d), lambda i: (i, 0)),
        out_shape=jax.ShapeDtypeStruct((t, d), F32),
        compiler_params=_cparams(("parallel",)),
        name="merge_out",
    )(x2, ya, yb, yc, yd, proj, proj, proj, proj, w_branch, w_out)


def _cross_kernel(x_ref, g_ref, wq_ref, kv_ref, wo_ref, o_ref):
    x = x_ref[0]
    hn = _rms(x, g_ref[...]).astype(BF16)
    q = _dot(hn, wq_ref[...]) * (X_HEAD_DIM ** -0.5)
    kv = kv_ref[0]
    hd = X_HEADS * X_HEAD_DIM
    outs = []
    for h in range(X_HEADS):
        cs = slice(h * X_HEAD_DIM, (h + 1) * X_HEAD_DIM)
        lg = _dot_nt(q[:, cs].astype(BF16), kv[:, cs])
        p = jnp.exp(lg - jnp.max(lg, axis=-1, keepdims=True))
        p = p / jnp.sum(p, axis=-1, keepdims=True)
        outs.append(_dot(p.astype(BF16), kv[:, hd + h * X_HEAD_DIM:hd + (h + 1) * X_HEAD_DIM]))
    o = jnp.concatenate(outs, axis=1).astype(BF16)
    o_ref[0] = x + _dot(o, wo_ref[...])


def cross_attn(x3, g, w_q, kv, w_o, *, tm):
    b, s, d = x3.shape
    m = kv.shape[1]
    hd = X_HEADS * X_HEAD_DIM
    return pl.pallas_call(
        _cross_kernel,
        grid=(b, s // tm),
        in_specs=[pl.BlockSpec((1, tm, d), lambda bi, i: (bi, i, 0)),
                  pl.BlockSpec((1, d), lambda bi, i: (0, 0)),
                  pl.BlockSpec((d, hd), lambda bi, i: (0, 0)),
                  pl.BlockSpec((1, m, 2 * hd), lambda bi, i: (bi, 0, 0)),
                  pl.BlockSpec((hd, d), lambda bi, i: (0, 0))],
        out_specs=pl.BlockSpec((1, tm, d), lambda bi, i: (bi, i, 0)),
        out_shape=jax.ShapeDtypeStruct((b, s, d), F32),
        compiler_params=_cparams(("parallel", "arbitrary")),
        name="cross_attn",
    )(x3, g, w_q, kv, w_o)


def _router_kernel(x_ref, g_ref, wr_ref, br_ref, hn_ref, cmb_ref):
    hn = _rms(x_ref[...], g_ref[...])
    hn_ref[...] = hn.astype(BF16)
    lg = jnp.dot(hn, wr_ref[...], precision=lax.Precision.HIGHEST, preferred_element_type=F32) + br_ref[...]
    lane = lax.broadcasted_iota(I32, lg.shape, 1)
    lanef = lane.astype(F32)
    ninf = -jnp.inf
    big = float(LANES)
    first = lambda mask: jnp.min(jnp.where(mask, lanef, big), axis=-1, keepdims=True)

    isg = jnp.logical_and(lane >= N_EXPERTS, lane < N_EXPERTS + N_GROUPS)
    gl = jnp.where(isg, lg, ninf)
    gmax = jnp.max(gl, axis=-1, keepdims=True)
    gsel = first(gl == gmax) - float(N_EXPERTS)
    pg = 1.0 / jnp.sum(jnp.exp(gl - gmax), axis=-1, keepdims=True)

    ise = jnp.floor(lanef * (1.0 / EXPERTS_PER_GROUP)) == gsel
    el = jnp.where(ise, lg, ninf)
    e1 = jnp.max(el, axis=-1, keepdims=True)
    i1 = first(el == e1)
    el2 = jnp.where(lanef == i1, ninf, el)
    e2 = jnp.max(el2, axis=-1, keepdims=True)
    i2 = first(el2 == e2)
    d = jnp.exp(e2 - e1)
    w1 = 1.0 / (1.0 + d)
    w2 = d / (1.0 + d)
    cmb = jnp.where(lanef == i1, pg * w1, jnp.where(lanef == i2, pg * w2, 0.0))
    cmb_ref[...] = jnp.where(lane == N_EXPERTS, gsel, cmb)


def moe_router(x2, g, wr, br, *, tm):
    t, d = x2.shape
    return pl.pallas_call(
        _router_kernel,
        grid=(t // tm,),
        in_specs=[pl.BlockSpec((tm, d), lambda i: (i, 0)), pl.BlockSpec((1, d), lambda i: (0, 0)),
                  pl.BlockSpec((d, LANES), lambda i: (0, 0)), pl.BlockSpec((1, LANES), lambda i: (0, 0))],
        out_specs=[pl.BlockSpec((tm, d), lambda i: (i, 0)), pl.BlockSpec((tm, LANES), lambda i: (i, 0))],
        out_shape=[jax.ShapeDtypeStruct((t, d), BF16), jax.ShapeDtypeStruct((t, LANES), F32)],
        compiler_params=_cparams(("parallel",)),
        name="moe_router",
    )(x2, g, wr, br)


MOE_PAD = 2 * SUBLANES
MOE_CH = 320
MOE_SB = 256


def _moe_rmax(tm):
    return -(-(tm + N_GROUPS * MOE_PAD + MOE_CH) // LANES) * LANES


def _moe_sort_kernel(hn_ref, cmb_ref, lts_ref, hs_ref, cs_ref, meta_ref, seg_ref, *, tm, rmax):
    cmb = cmb_ref[...]
    lanef = lax.broadcasted_iota(I32, (tm, LANES), 1).astype(F32)
    oh = jnp.where(lanef == cmb[:, N_EXPERTS:N_EXPERTS + 1], 1.0, 0.0)
    seen = jnp.zeros((1, LANES), F32)
    pres = []
    for sblk in range(tm // MOE_SB):
        ohs = oh[sblk * MOE_SB:(sblk + 1) * MOE_SB]
        pres.append(_dot(lts_ref[...], ohs.astype(BF16)) + seen)
        seen = seen + jnp.sum(ohs, axis=0, keepdims=True)
    pre = jnp.concatenate(pres, axis=0)
    cnt = jnp.broadcast_to(seen, (SUBLANES, LANES))
    padn = jnp.ceil(cnt * (1.0 / MOE_PAD)) * MOE_PAD
    lane8 = lax.broadcasted_iota(I32, (SUBLANES, LANES), 1)
    off = jnp.zeros((SUBLANES, LANES), F32)
    for k in range(1, N_GROUPS):
        off = off + jnp.where(lane8 >= k, pltpu.roll(padn, k, 1), 0.0)
    dest = jnp.sum(oh * (off[0:1] + pre), axis=-1, keepdims=True)
    destb = jnp.broadcast_to(dest, (tm, LANES))
    meta_ref[...] = destb
    row8 = lax.broadcasted_iota(I32, (SUBLANES, LANES), 0)
    seg_ref[0] = jnp.where(row8 == 0, off, jnp.where(row8 == 1, cnt, 0.0)).astype(I32)

    dest_row = destb.T[0:1]
    hn = hn_ref[...]
    chi = cmb.astype(BF16)
    clo = (cmb - chi.astype(F32)).astype(BF16)

    def blk(rb, carry):
        r0 = pl.multiple_of(rb * LANES, LANES)
        rows = (r0 + lax.broadcasted_iota(I32, (LANES, tm), 0)).astype(F32)
        p = jnp.where(rows == dest_row, 1.0, 0.0).astype(BF16)
        hs_ref[0, pl.ds(r0, LANES), :] = _dot(p, hn).astype(BF16)
        cs_ref[0, pl.ds(r0, LANES), :] = _dot(p, chi) + _dot(p, clo)
        return carry

    lax.fori_loop(0, rmax // LANES, blk, 0)


def moe_sort(hn, cmb, *, tm):
    t, d = hn.shape
    nt = t // tm
    rmax = _moe_rmax(tm)
    ri = np.arange(MOE_SB)
    lts = jnp.asarray((ri[None, :] < ri[:, None]).astype(np.float32), BF16)
    return pl.pallas_call(
        functools.partial(_moe_sort_kernel, tm=tm, rmax=rmax),
        grid=(nt,),
        in_specs=[pl.BlockSpec((tm, d), lambda i: (i, 0)), pl.BlockSpec((tm, LANES), lambda i: (i, 0)),
                  pl.BlockSpec((MOE_SB, MOE_SB), lambda i: (0, 0))],
        out_specs=[pl.BlockSpec((1, rmax, d), lambda i: (i, 0, 0)),
                   pl.BlockSpec((1, rmax, LANES), lambda i: (i, 0, 0)),
                   pl.BlockSpec((tm, LANES), lambda i: (i, 0)),
                   pl.BlockSpec((1, SUBLANES, LANES), lambda i: (i, 0, 0))],
        out_shape=[jax.ShapeDtypeStruct((nt, rmax, d), BF16), jax.ShapeDtypeStruct((nt, rmax, LANES), F32),
                   jax.ShapeDtypeStruct((t, LANES), F32), jax.ShapeDtypeStruct((nt, SUBLANES, LANES), I32)],
        compiler_params=_cparams(("parallel",)),
        name="moe_sort",
    )(hn, cmb, lts)


def _moe_expert_kernel(seg_ref, hs_ref, cs_ref, wg_ref, wu_ref, wd_ref, y_ref, acc_ref, *, ns):
    i = pl.program_id(0)
    e = pl.program_id(1)

    @pl.when(e == 0)
    def _():
        acc_ref[...] = jnp.zeros_like(acc_ref)

    g = e // EXPERTS_PER_GROUP
    for s in range(ns):
        base = (i * ns + s) * 2 * N_GROUPS
        off = seg_ref[base + g]
        n = seg_ref[base + N_GROUPS + g]

        def chunk(j, carry, s=s, off=off):
            r0 = pl.multiple_of(off + j * MOE_CH, MOE_PAD)
            hsl = hs_ref[s, pl.ds(r0, MOE_CH), :]
            gg = _dot(hsl, wg_ref[0])
            uu = _dot(hsl, wu_ref[0])
            c = cs_ref[s, pl.ds(r0, MOE_CH), :]
            lane = lax.broadcasted_iota(I32, c.shape, 1)
            cc = jnp.sum(jnp.where(lane == e, c, 0.0), axis=-1, keepdims=True)
            he = (gg * _sigmoid(gg)) * uu * cc
            acc_ref[s, pl.ds(r0, MOE_CH), :] += _dot(he.astype(BF16), wd_ref[0])
            return carry

        lax.fori_loop(0, (n + MOE_CH - 1) // MOE_CH, chunk, 0)

    @pl.when(e == pl.num_programs(1) - 1)
    def _():
        y_ref[...] = acc_ref[...].astype(BF16)


def moe_experts(seg, hs, cs, wg, wu, wd, *, ns):
    nt, rmax, d = hs.shape
    ne, _, f = wg.shape
    return pl.pallas_call(
        functools.partial(_moe_expert_kernel, ns=ns),
        grid_spec=pltpu.PrefetchScalarGridSpec(
            num_scalar_prefetch=1,
            grid=(nt // ns, ne),
            in_specs=[pl.BlockSpec((ns, rmax, d), lambda i, e, sref: (i, 0, 0)),
                      pl.BlockSpec((ns, rmax, LANES), lambda i, e, sref: (i, 0, 0)),
                      pl.BlockSpec((1, d, f), lambda i, e, sref: (e, 0, 0)),
                      pl.BlockSpec((1, d, f), lambda i, e, sref: (e, 0, 0)),
                      pl.BlockSpec((1, f, d), lambda i, e, sref: (e, 0, 0))],
            out_specs=pl.BlockSpec((ns, rmax, d), lambda i, e, sref: (i, 0, 0)),
            scratch_shapes=[pltpu.VMEM((ns, rmax, d), F32)]),
        out_shape=jax.ShapeDtypeStruct((nt, rmax, d), BF16),
        compiler_params=_cparams(("parallel", "arbitrary")),
        name="moe_experts",
    )(seg, hs, cs, wg, wu, wd)


def _moe_combine_kernel(x_ref, meta_ref, y_ref, gf_ref, o_ref, *, rmax, final_norm):
    tb = x_ref.shape[0]
    lanef = lax.broadcasted_iota(I32, (tb, rmax), 1).astype(F32)
    pt = jnp.where(lanef == meta_ref[:, 0:1], 1.0, 0.0).astype(BF16)
    out = x_ref[...] + _dot(pt, y_ref[0])
    o_ref[...] = _rms(out, gf_ref[...]) if final_norm else out


def moe_combine(x2, meta, y, g_final, *, tm, tb, final_norm):
    t, d = x2.shape
    nt, rmax, _ = y.shape
    nb = tm // tb
    return pl.pallas_call(
        functools.partial(_moe_combine_kernel, rmax=rmax, final_norm=final_norm),
        grid=(nt, nb),
        in_specs=[pl.BlockSpec((tb, d), lambda i, j: (i * nb + j, 0)),
                  pl.BlockSpec((tb, LANES), lambda i, j: (i * nb + j, 0)),
                  pl.BlockSpec((1, rmax, d), lambda i, j: (i, 0, 0)),
                  pl.BlockSpec((1, d), lambda i, j: (0, 0))],
        out_specs=pl.BlockSpec((tb, d), lambda i, j: (i * nb + j, 0)),
        out_shape=jax.ShapeDtypeStruct((t, d), F32),
        compiler_params=_cparams(("parallel", "arbitrary")),
        name="moe_combine",
    )(x2, meta, y, g_final)


def _permute_w_in(w):
    d = w.shape[0]
    o = 0
    seg = {}
    for name, width in (("c", 768), ("gla", 1024), ("ga", GLA_RANK), ("s5u", 256), ("dq", 256), ("dk", 64),
                        ("dv", 64), ("iq", 256), ("ik", IDX_DIM), ("iw", IDX_HEADS), ("gates", 4096)):
        seg[name] = w[:, o:o + width]
        o += width
    z = lambda n: jnp.zeros((d, n), w.dtype)
    cols = [seg["c"], seg["s5u"], seg["gla"], seg["dq"], seg["iq"], seg["dk"], seg["dv"],
            seg["ik"], seg["iw"], z(LANES - IDX_DIM - IDX_HEADS), seg["ga"], z(LANES - GLA_RANK),
            z(P_GATES - P_GA - LANES), seg["gates"]]
    out = jnp.concatenate(cols, axis=1).astype(BF16)
    assert out.shape[1] == P_TOTAL
    return out


def _s5_matrices(bb_re, bb_im, c_re, c_im):
    g, p, c = S5_GROUPS, S5_STATE, S5_GROUP
    n = g * p
    rows_g = jnp.arange(g * c) // c
    cols_g = jnp.arange(n) // p
    mask = (rows_g[:, None] == cols_g[None, :]).astype(F32)
    bm = jnp.concatenate([jnp.tile(bb_re, (g, 1)) * mask, jnp.tile(bb_im, (g, 1)) * mask], axis=1)
    ct = lambda a: jnp.tile(jnp.transpose(a, (0, 2, 1)).reshape(n, c), (1, g)) * mask.T
    cm = jnp.concatenate([ct(c_re), -ct(c_im)], axis=0)
    return bm.astype(BF16), cm.astype(BF16)


def _pick(s, pref):
    for c in pref:
        if s % c == 0:
            return c
    return s


def kernel(x, mem, positions, norm_mix, w_in, conv_w, conv_b, gla_a_up, gla_a_b, gla_norm, s5_lambda_re,
           s5_lambda_im, s5_log_dt, s5_b_re, s5_b_im, s5_c_re, s5_c_im, s5_d, s5_w_glu, s5_b_glu, w_branch,
           w_out, norm_cross, w_cq, w_ckv, w_co, norm_ffn, w_route_group, b_route_group, w_route_expert,
           b_route_expert, w_e_gate, w_e_up, w_e_down, norm_mem, norm_final):
    b, s, d = x.shape
    t = b * s
    m = mem.shape[1]
    depth = w_in.shape[0]
    topk = min(DSA_TOPK, s // 4)
    ts = _pick(s, (512, 256, 128))
    tm = _pick(t, (1024, 512, 256, 128))
    row = lambda a: a.reshape(1, -1)

    tabs = rope_tables(positions, ts=ts)
    x2 = x.reshape(t, d)
    mem2 = mem.reshape(b * m, d)
    for l in range(depth):
        proj = norm_matmul(x2, row(norm_mix[l]), _permute_w_in(w_in[l]), tm=_pick(t, (2048, 1024, 512)), tn=1024)
        proj3 = proj.reshape(b, s, P_TOTAL)
        ya = conv_branch(proj3, conv_w[l], row(conv_b[l]), ts=ts)
        a_up_p = jnp.concatenate(
            [gla_a_up[l], jnp.zeros((LANES - GLA_RANK, gla_a_up.shape[2]), F32)], axis=0).astype(BF16)
        yb = gla_branch(proj3, a_up_p, row(gla_a_b[l]), row(gla_norm[l]), tt=ts)
        bb_re, bb_im, tab = s5_params(s5_lambda_re[l], s5_lambda_im[l], s5_log_dt[l], s5_b_re[l], s5_b_im[l])
        bm, cm = _s5_matrices(bb_re, bb_im, s5_c_re[l], s5_c_im[l])
        yc = s5_branch(proj3, bm, tab, cm, row(s5_d[l]), s5_w_glu[l].astype(BF16), row(s5_b_glu[l]), tt=ts)
        qs, qis, kr, vt4, kir, wt = dsa_prep(proj3, tabs, ts=ts)
        yd = dsa_attend(qs, qis, wt, kr, vt4, kir, qb=_pick(s, (2 * Q_BLOCK, Q_BLOCK)), kt=ts, topk=topk)
        w2 = lambda a: a.reshape(t, a.shape[-1])
        x2 = merge_out(x2, w2(ya), w2(yb), w2(yc), w2(yd), proj, w_branch[l].astype(BF16),
                       w_out[l].astype(BF16), tm=min(tm, 512))
        kv = norm_matmul(mem2, row(norm_mem), w_ckv[l].astype(BF16), tm=_pick(b * m, (1024, 512, 256)), tn=1024)
        x2 = cross_attn(x2.reshape(b, s, d), row(norm_cross[l]), w_cq[l].astype(BF16),
                        kv.reshape(b, m, -1), w_co[l].astype(BF16), tm=ts).reshape(t, d)
        pad = jnp.zeros((d, LANES - N_EXPERTS - N_GROUPS), F32)
        wr = jnp.concatenate([w_route_expert[l], w_route_group[l], pad], axis=1)
        br = jnp.concatenate([b_route_expert[l], b_route_group[l], pad[0]], axis=0).reshape(1, LANES)
        hn, cmb = moe_router(x2, row(norm_ffn[l]), wr, br, tm=min(tm, 512))
        tmoe = _pick(t, (1024, 512))
        hs, cs, meta, seg = moe_sort(hn, cmb, tm=tmoe)
        seg1 = jnp.concatenate([seg[:, 0, :N_GROUPS], seg[:, 1, :N_GROUPS]], axis=1).reshape(-1)
        y = moe_experts(seg1, hs, cs, w_e_gate[l].astype(BF16), w_e_up[l].astype(BF16), w_e_down[l].astype(BF16),
                        ns=2 if (t // tmoe) % 2 == 0 else 1)
        x2 = moe_combine(x2, meta, y, row(norm_final), tm=tmoe, tb=min(tmoe, 512), final_norm=(l == depth - 1))
    return x2.reshape(b, s, d)
```

```python
import functools
import math

import numpy as np
import jax
import jax.numpy as jnp
from jax import lax
from jax.experimental import pallas as pl
from jax.experimental.pallas import tpu as pltpu

F32 = jnp.float32
BF16 = jnp.bfloat16
I32 = jnp.int32

D_MODEL = 1024
DEPTH = 2
EPS = 1e-6
N_BRANCH = 4
BRANCH_W = 256
CONV_W = 3
GLA_HEADS = 4
GLA_DK = 64
GLA_DV = 64
GLA_RANK = 16
GLA_TAU = 16.0
GLA_CHUNK = 64
S5_GROUP = 16
S5_GROUPS = BRANCH_W // S5_GROUP
S5_STATE = 64
DSA_HEADS = 4
DSA_HEAD_DIM = 64
IDX_HEADS = 8
IDX_DIM = 32
DSA_TOPK = 256
Q_BLOCK = 128
ROPE_THETA = 500000.0
ROPE_FRAC = 4
X_HEADS = 4
X_HEAD_DIM = 128
N_GROUPS = 4
EXPERTS_PER_GROUP = 4
N_EXPERTS = N_GROUPS * EXPERTS_PER_GROUP
D_FF_EXPERT = 512

LANES = 128
SUBLANES = 8
VMEM_LIMIT = 48 * 1024 * 1024

P_CONV = 0
P_S5U = 768
P_GLA = 1024
P_DQ = 2048
P_IQ = 2304
P_KV = 2560
P_IKW = 2688
P_GA = 2816
P_GATES = 3072
P_TOTAL = P_GATES + N_BRANCH * D_MODEL

INT_MIN = -2147483648
NEG_INF_KEY = INT_MIN + 0x7FFFFF
VT_ROWS = DSA_HEAD_DIM + 2 * SUBLANES


def _cparams(sem):
    return pltpu.CompilerParams(dimension_semantics=sem, vmem_limit_bytes=VMEM_LIMIT)


def _dot(a, b):
    return jnp.dot(a, b, preferred_element_type=F32)


def _dot_nt(a, b):
    return lax.dot_general(a, b, (((1,), (1,)), ((), ())), preferred_element_type=F32)


def _dot_tn(a, b):
    return lax.dot_general(a, b, (((0,), (0,)), ((), ())), preferred_element_type=F32)


def _split_dot(exact_bf16, x):
    hi = x.astype(BF16)
    lo = (x - hi.astype(F32)).astype(BF16)
    return _dot(exact_bf16, hi) + _dot(exact_bf16, lo)


def _split_dot_r(x, exact_bf16):
    hi = x.astype(BF16)
    lo = (x - hi.astype(F32)).astype(BF16)
    return _dot(hi, exact_bf16) + _dot(lo, exact_bf16)


def _rms(x, g):
    return x * lax.rsqrt(jnp.mean(x * x, axis=-1, keepdims=True) + EPS) * g


def _sigmoid(x):
    return 1.0 / (1.0 + jnp.exp(-x))


def _norm_matmul_kernel(x_ref, g_ref, w_ref, o_ref, hn_ref):
    @pl.when(pl.program_id(1) == 0)
    def _():
        hn_ref[...] = _rms(x_ref[...], g_ref[...]).astype(BF16)

    o_ref[...] = _dot(hn_ref[...], w_ref[...]).astype(o_ref.dtype)


def norm_matmul(x, g, w, *, tm, tn, out_dtype=BF16):
    t, d = x.shape
    n = w.shape[1]
    return pl.pallas_call(
        _norm_matmul_kernel,
        grid=(t // tm, n // tn),
        in_specs=[
            pl.BlockSpec((tm, d), lambda i, j: (i, 0)),
            pl.BlockSpec((1, d), lambda i, j: (0, 0)),
            pl.BlockSpec((d, tn), lambda i, j: (0, j)),
        ],
        out_specs=pl.BlockSpec((tm, tn), lambda i, j: (i, j)),
        out_shape=jax.ShapeDtypeStruct((t, n), out_dtype),
        scratch_shapes=[pltpu.VMEM((tm, d), BF16)],
        compiler_params=_cparams(("parallel", "arbitrary")),
        name="norm_matmul",
    )(x, g, w)


def _conv_kernel(cur_ref, prev_ref, w_ref, b_ref, o_ref):
    i = pl.program_id(1)
    w_ = BRANCH_W
    cur = cur_ref[0].astype(F32)
    u = cur[:, 2 * w_:3 * w_] * cur[:, 0:w_]
    pv = prev_ref[0].astype(F32)
    pu = pv[:, 2 * w_:3 * w_] * pv[:, 0:w_]
    pu = jnp.where(i > 0, pu, 0.0)
    row = lax.broadcasted_iota(I32, u.shape, 0)
    u1 = jnp.where(row == 0, pu[7:8], pltpu.roll(u, 1, 0))
    u2 = jnp.where(row == 0, pu[6:7], jnp.where(row == 1, pu[7:8], pltpu.roll(u, 2, 0)))
    w = w_ref[...]
    y = w[0:1] * u2 + w[1:2] * u1 + w[2:3] * u + b_ref[...]
    o_ref[0] = (cur[:, w_:2 * w_] * y).astype(o_ref.dtype)


def conv_branch(proj3, conv_w, conv_b, *, ts):
    b, s, _ = proj3.shape
    wc = 3 * BRANCH_W
    hb = ts // SUBLANES
    return pl.pallas_call(
        _conv_kernel,
        grid=(b, s // ts),
        in_specs=[
            pl.BlockSpec((1, ts, wc), lambda bi, i: (bi, i, P_CONV // wc)),
            pl.BlockSpec((1, SUBLANES, wc), lambda bi, i: (bi, jnp.maximum(i * hb - 1, 0), P_CONV // wc)),
            pl.BlockSpec((CONV_W, BRANCH_W), lambda bi, i: (0, 0)),
            pl.BlockSpec((1, BRANCH_W), lambda bi, i: (0, 0)),
        ],
        out_specs=pl.BlockSpec((1, ts, BRANCH_W), lambda bi, i: (bi, i, 0)),
        out_shape=jax.ShapeDtypeStruct((b, s, BRANCH_W), BF16),
        compiler_params=_cparams(("parallel", "arbitrary")),
        name="conv_branch",
    )(proj3, proj3, conv_w, conv_b)


def _gla_kernel(g_ref, a_ref, aup_ref, ab_ref, gn_ref, ltb_ref, ltf_ref, bob_ref, bd_ref, bdb_ref,
                o_ref, st_ref, *, tt):
    hw = GLA_HEADS * GLA_DK
    c = GLA_CHUNK

    @pl.when(pl.program_id(1) == 0)
    def _():
        st_ref[...] = jnp.zeros_like(st_ref)

    blk = g_ref[0]
    q = blk[:, 0:hw].astype(F32)
    k = blk[:, hw:2 * hw].astype(F32)
    vb = blk[:, 2 * hw:3 * hw]
    v = vb.astype(F32)
    r = blk[:, 3 * hw:4 * hw].astype(F32)

    pre = _dot(a_ref[0], aup_ref[...]) + ab_ref[...]
    la = (jnp.minimum(pre, 0.0) - jnp.log(1.0 + jnp.exp(-jnp.abs(pre)))) * (1.0 / GLA_TAU)
    cum = _split_dot(ltb_ref[...], la)
    tot = _split_dot(bob_ref[...], la)
    q_dec = q * (GLA_DK ** -0.5) * jnp.exp(cum)
    k_inv = (k * jnp.exp(-cum)).astype(BF16)
    k_end = (k * jnp.exp(tot - cum)).astype(BF16)
    qdb = q_dec.astype(BF16)

    lane = lax.broadcasted_iota(I32, (1, hw), 1)
    ltmask = ltf_ref[...] > 0.0
    o = jnp.zeros((tt, hw), F32)
    for h in range(GLA_HEADS):
        hm = (lane // GLA_DK) == h
        qh = jnp.where(hm, q_dec, 0.0).astype(BF16)
        att = jnp.where(ltmask, _dot_nt(qh, k_inv), 0.0)
        vh = jnp.where(hm, v, 0.0).astype(BF16)
        o = o + _dot(att.astype(BF16), vh)

    st = st_ref[...]
    bd = bd_ref[...]
    inter = []
    for n in range(tt // c):
        rows = slice(n * c, (n + 1) * c)
        inter.append(_dot_nt(qdb[rows], st.astype(BF16)))
        dec = jnp.exp(tot[n * c:n * c + 1, :])
        st = st * dec + _dot_tn(vb[rows], k_end[rows]) * bd
    st_ref[...] = st
    o = o + jnp.concatenate(inter, axis=0)

    msq = _split_dot_r(o * o, bdb_ref[...]) * (1.0 / GLA_DV)
    y = o * lax.rsqrt(msq + EPS) * gn_ref[...] * (r * _sigmoid(r))
    o_ref[0] = y.astype(o_ref.dtype)


def _gla_consts(tt):
    ri = np.arange(tt)[:, None]
    ci = np.arange(tt)[None, :]
    same = (ri // GLA_CHUNK) == (ci // GLA_CHUNK)
    lt = (same & (ci <= ri)).astype(np.float32)
    hw = GLA_HEADS * GLA_DK
    hi = np.arange(hw)
    bd = ((hi[:, None] // GLA_DK) == (hi[None, :] // GLA_DK)).astype(np.float32)
    return (jnp.asarray(lt, BF16), jnp.asarray(lt, F32), jnp.asarray(same.astype(np.float32), BF16),
            jnp.asarray(bd, F32), jnp.asarray(bd, BF16))


def gla_branch(proj3, a_up_p, a_b, g_norm, *, tt):
    b, s, _ = proj3.shape
    hw = GLA_HEADS * GLA_DK
    ltb, ltf, bob, bd, bdb = _gla_consts(tt)
    const = lambda shape: pl.BlockSpec(shape, lambda bi, i: (0,) * len(shape))
    return pl.pallas_call(
        functools.partial(_gla_kernel, tt=tt),
        grid=(b, s // tt),
        in_specs=[
            pl.BlockSpec((1, tt, 4 * hw), lambda bi, i: (bi, i, P_GLA // (4 * hw))),
            pl.BlockSpec((1, tt, LANES), lambda bi, i: (bi, i, P_GA // LANES)),
            const((LANES, hw)), const((1, hw)), const((1, hw)),
            const((tt, tt)), const((tt, tt)), const((tt, tt)), const((hw, hw)), const((hw, hw)),
        ],
        out_specs=pl.BlockSpec((1, tt, hw), lambda bi, i: (bi, i, 0)),
        out_shape=jax.ShapeDtypeStruct((b, s, hw), BF16),
        scratch_shapes=[pltpu.VMEM((hw, hw), F32)],
        compiler_params=_cparams(("parallel", "arbitrary")),
        name="gla_branch",
    )(proj3, proj3, a_up_p, a_b, g_norm, ltb, ltf, bob, bd, bdb)


def _s5_param_kernel(lre_ref, lim_ref, ldt_ref, bre_ref, bim_ref, bbre_ref, bbim_ref, tab_ref):
    lre = lre_ref[...]
    lim = lim_ref[...]
    dt = jnp.exp(ldt_ref[...])
    mag = jnp.exp(lre * dt)
    lbr = mag * jnp.cos(lim * dt)
    lbi = mag * jnp.sin(lim * dt)
    den = lre * lre + lim * lim
    fre = ((lbr - 1.0) * lre + lbi * lim) / den
    fim = (lbi * lre - (lbr - 1.0) * lim) / den
    bre = bre_ref[...]
    bim = bim_ref[...]
    bbre_ref[...] = fre * bre - fim * bim
    bbim_ref[...] = fre * bim + fim * bre

    pw = [None, (lbr, lbi)]
    for _ in range(2, SUBLANES + 1):
        pr, pi = pw[-1]
        pw.append((pr * lbr - pi * lbi, pr * lbi + pi * lbr))
    n = lre.shape[-1]
    row = lax.broadcasted_iota(I32, (SUBLANES, n), 0)
    zero = jnp.zeros((SUBLANES, n), F32)
    for idx, sft in enumerate((1, 2, 4)):
        tab_ref[2 * idx] = jnp.where(row >= sft, jnp.broadcast_to(pw[sft][0], (SUBLANES, n)), zero)
        tab_ref[2 * idx + 1] = jnp.where(row >= sft, jnp.broadcast_to(pw[sft][1], (SUBLANES, n)), zero)
    cr, ci = zero, zero
    for rr in range(SUBLANES):
        cr = jnp.where(row == rr, jnp.broadcast_to(pw[rr + 1][0], (SUBLANES, n)), cr)
        ci = jnp.where(row == rr, jnp.broadcast_to(pw[rr + 1][1], (SUBLANES, n)), ci)
    tab_ref[6] = cr
    tab_ref[7] = ci


def s5_params(lam_re, lam_im, log_dt, b_re, b_im):
    g, p = lam_re.shape
    n = g * p
    row = lambda a: a.reshape(1, n)
    ldt = jnp.broadcast_to(log_dt[:, None], (g, p))
    bt = lambda a: jnp.transpose(a, (2, 0, 1)).reshape(S5_GROUP, n)
    full = lambda shape: pl.BlockSpec(shape, lambda: (0,) * len(shape))
    return pl.pallas_call(
        _s5_param_kernel,
        in_specs=[full((1, n))] * 3 + [full((S5_GROUP, n))] * 2,
        out_specs=[full((S5_GROUP, n)), full((S5_GROUP, n)), full((8, SUBLANES, n))],
        out_shape=[jax.ShapeDtypeStruct((S5_GROUP, n), F32), jax.ShapeDtypeStruct((S5_GROUP, n), F32),
                   jax.ShapeDtypeStruct((8, SUBLANES, n), F32)],
        name="s5_params",
    )(row(lam_re), row(lam_im), row(ldt), bt(b_re), bt(b_im))


def _s5_kernel(u_ref, bm_ref, tab_ref, cm_ref, d_ref, wg_ref, bg_ref, o_ref, xs_ref, car_ref, *, tt, n):
    @pl.when(pl.program_id(1) == 0)
    def _():
        car_ref[...] = jnp.zeros_like(car_ref)

    ub = u_ref[0]
    xs_ref[...] = _dot(ub, bm_ref[...])

    def group(gi, carry):
        r0 = pl.multiple_of(gi * SUBLANES, SUBLANES)
        for j in range(n // LANES):
            cre = slice(j * LANES, (j + 1) * LANES)
            cim = slice(n + j * LANES, n + (j + 1) * LANES)
            re = xs_ref[pl.ds(r0, SUBLANES), cre]
            im = xs_ref[pl.ds(r0, SUBLANES), cim]
            for idx, sft in enumerate((1, 2, 4)):
                ar = tab_ref[2 * idx, :, cre]
                ai = tab_ref[2 * idx + 1, :, cre]
                sr = pltpu.roll(re, sft, 0)
                si = pltpu.roll(im, sft, 0)
                re, im = re + ar * sr - ai * si, im + ar * si + ai * sr
            pr = tab_ref[6, :, cre]
            pi = tab_ref[7, :, cre]
            cr = car_ref[0, :, cre]
            ci = car_ref[1, :, cre]
            re, im = re + pr * cr - pi * ci, im + pr * ci + pi * cr
            xs_ref[pl.ds(r0, SUBLANES), cre] = re
            xs_ref[pl.ds(r0, SUBLANES), cim] = im
            car_ref[0, :, cre] = jnp.broadcast_to(re[SUBLANES - 1:SUBLANES], (SUBLANES, LANES))
            car_ref[1, :, cre] = jnp.broadcast_to(im[SUBLANES - 1:SUBLANES], (SUBLANES, LANES))
        return carry

    lax.fori_loop(0, tt // SUBLANES, group, 0)

    y = _dot(xs_ref[...].astype(BF16), cm_ref[...]) + d_ref[...] * ub.astype(F32)
    y = 0.5 * y * (1.0 + jnp.tanh(math.sqrt(2.0 / math.pi) * (y + 0.044715 * (y * y * y))))
    z = _dot(y.astype(BF16), wg_ref[...]) + bg_ref[...]
    o_ref[0] = (y * _sigmoid(z)).astype(o_ref.dtype)


def s5_branch(proj3, bmat, tab, cmat, d_skip, w_glu, b_glu, *, tt):
    b, s, _ = proj3.shape
    w_ = BRANCH_W
    n = S5_GROUPS * S5_STATE
    const = lambda shape: pl.BlockSpec(shape, lambda bi, i: (0,) * len(shape))
    return pl.pallas_call(
        functools.partial(_s5_kernel, tt=tt, n=n),
        grid=(b, s // tt),
        in_specs=[
            pl.BlockSpec((1, tt, w_), lambda bi, i: (bi, i, P_S5U // w_)),
            const((w_, 2 * n)), const((8, SUBLANES, n)), const((2 * n, w_)),
            const((1, w_)), const((w_, w_)), const((1, w_)),
        ],
        out_specs=pl.BlockSpec((1, tt, w_), lambda bi, i: (bi, i, 0)),
        out_shape=jax.ShapeDtypeStruct((b, s, w_), BF16),
        scratch_shapes=[pltpu.VMEM((tt, 2 * n), F32), pltpu.VMEM((2, SUBLANES, n), F32)],
        compiler_params=_cparams(("parallel", "arbitrary")),
        name="s5_branch",
    )(proj3, bmat, tab, cmat, d_skip, w_glu, b_glu)


def _rope_freq_rows():
    rows = np.zeros((8, LANES), np.float32)
    for pat, dh in enumerate((DSA_HEAD_DIM, IDX_DIM)):
        rd = dh // ROPE_FRAC
        half = rd // 2
        inv = (np.float32(ROPE_THETA) ** (-np.arange(half, dtype=np.float32) * np.float32(2.0 / rd))).astype(np.float32)
        for l in range(LANES):
            i = l % dh
            if i < half:
                rows[3 * pat, l] = inv[i]
                rows[3 * pat + 1, l] = -1.0
            elif i < rd:
                rows[3 * pat, l] = inv[i - half]
                rows[3 * pat + 2, l] = 1.0
    return rows


def _rope_tab_kernel(pos_ref, fr_ref, o_ref):
    pos = pos_ref[0]
    fr = fr_ref[...]
    for pat in range(2):
        ang = pos * fr[3 * pat:3 * pat + 1]
        c = jnp.cos(ang)
        s = jnp.sin(ang)
        o_ref[0, 3 * pat] = c
        o_ref[0, 3 * pat + 1] = s * fr[3 * pat + 1:3 * pat + 2]
        o_ref[0, 3 * pat + 2] = s * fr[3 * pat + 2:3 * pat + 3]


def rope_tables(positions, *, ts):
    b, s = positions.shape
    pos = positions.astype(F32).reshape(b, s, 1)
    fr = jnp.asarray(_rope_freq_rows())
    return pl.pallas_call(
        _rope_tab_kernel,
        grid=(b, s // ts),
        in_specs=[pl.BlockSpec((1, ts, 1), lambda bi, i: (bi, i, 0)),
                  pl.BlockSpec((8, LANES), lambda bi, i: (0, 0))],
        out_specs=pl.BlockSpec((1, 6, ts, LANES), lambda bi, i: (bi, 0, i, 0)),
        out_shape=jax.ShapeDtypeStruct((b, 6, s, LANES), F32),
        compiler_params=_cparams(("parallel", "arbitrary")),
        name="rope_tables",
    )(pos, fr)


def _rope(t, c, sm, sp, half):
    n = t.shape[-1]
    return t * c + sm * pltpu.roll(t, n - half, 1) + sp * pltpu.roll(t, half, 1)


def _dsa_prep_kernel(dq_ref, iq_ref, kv_ref, ikw_ref, tab_ref,
                     qs_ref, qis_ref, kr_ref, vt_ref, kir_ref, wt_ref):
    two = lambda a: jnp.concatenate([a, a], axis=1)
    c1, sm1, sp1 = tab_ref[0, 0], tab_ref[0, 1], tab_ref[0, 2]
    c2, sm2, sp2 = tab_ref[0, 3], tab_ref[0, 4], tab_ref[0, 5]
    h1 = DSA_HEAD_DIM // ROPE_FRAC // 2
    h2 = IDX_DIM // ROPE_FRAC // 2
    lane = lax.broadcasted_iota(I32, (1, LANES), 1)

    q = _rope(dq_ref[0].astype(F32), two(c1), two(sm1), two(sp1), h1) * (DSA_HEAD_DIM ** -0.5 * math.log2(math.e))
    for h in range(DSA_HEADS):
        qs_ref[0, h] = q[:, h * DSA_HEAD_DIM:(h + 1) * DSA_HEAD_DIM].astype(BF16)
    qi = _rope(iq_ref[0].astype(F32), two(c2), two(sm2), two(sp2), h2)
    for h in range(IDX_HEADS):
        qis_ref[0, h] = qi[:, h * IDX_DIM:(h + 1) * IDX_DIM].astype(BF16)

    kv = kv_ref[0].astype(F32)
    isk = lane < DSA_HEAD_DIM
    kvr = _rope(kv, jnp.where(isk, c1, 1.0), jnp.where(isk, sm1, 0.0), jnp.where(isk, sp1, 0.0), h1)
    kr_ref[0] = kvr[:, 0:DSA_HEAD_DIM].astype(BF16)
    ones = jnp.ones((VT_ROWS - DSA_HEAD_DIM, kv.shape[0]), BF16)
    vt_ref[0, 0] = jnp.concatenate([kvr.T[DSA_HEAD_DIM:2 * DSA_HEAD_DIM].astype(BF16), ones], axis=0)

    ikw = ikw_ref[0].astype(F32)
    isi = lane < IDX_DIM
    ikr = _rope(ikw, jnp.where(isi, c2, 1.0), jnp.where(isi, sm2, 0.0), jnp.where(isi, sp2, 0.0), h2)
    kir_ref[0] = ikr[:, 0:IDX_DIM].astype(BF16)
    wt_ref[0] = ikr.T[IDX_DIM:IDX_DIM + IDX_HEADS] * ((IDX_HEADS ** -0.5) * (IDX_DIM ** -0.5))


def dsa_prep(proj3, tabs, *, ts):
    b, s, _ = proj3.shape
    qw = DSA_HEADS * DSA_HEAD_DIM
    iw = IDX_HEADS * IDX_DIM
    return pl.pallas_call(
        _dsa_prep_kernel,
        grid=(b, s // ts),
        in_specs=[
            pl.BlockSpec((1, ts, qw), lambda bi, i: (bi, i, P_DQ // qw)),
            pl.BlockSpec((1, ts, iw), lambda bi, i: (bi, i, P_IQ // iw)),
            pl.BlockSpec((1, ts, LANES), lambda bi, i: (bi, i, P_KV // LANES)),
            pl.BlockSpec((1, ts, LANES), lambda bi, i: (bi, i, P_IKW // LANES)),
            pl.BlockSpec((1, 6, ts, LANES), lambda bi, i: (bi, 0, i, 0)),
        ],
        out_specs=[
            pl.BlockSpec((1, DSA_HEADS, ts, DSA_HEAD_DIM), lambda bi, i: (bi, 0, i, 0)),
            pl.BlockSpec((1, IDX_HEADS, ts, IDX_DIM), lambda bi, i: (bi, 0, i, 0)),
            pl.BlockSpec((1, ts, DSA_HEAD_DIM), lambda bi, i: (bi, i, 0)),
            pl.BlockSpec((1, 1, VT_ROWS, ts), lambda bi, i: (bi, i, 0, 0)),
            pl.BlockSpec((1, ts, IDX_DIM), lambda bi, i: (bi, i, 0)),
            pl.BlockSpec((1, IDX_HEADS, ts), lambda bi, i: (bi, 0, i)),
        ],
        out_shape=[
            jax.ShapeDtypeStruct((b, DSA_HEADS, s, DSA_HEAD_DIM), BF16),
            jax.ShapeDtypeStruct((b, IDX_HEADS, s, IDX_DIM), BF16),
            jax.ShapeDtypeStruct((b, s, DSA_HEAD_DIM), BF16),
            jax.ShapeDtypeStruct((b, s // ts, VT_ROWS, ts), BF16),
            jax.ShapeDtypeStruct((b, s, IDX_DIM), BF16),
            jax.ShapeDtypeStruct((b, IDX_HEADS, s), F32),
        ],
        compiler_params=_cparams(("parallel", "arbitrary")),
        name="dsa_prep",
    )(proj3, proj3, proj3, proj3, tabs)


def _dsa_kernel(q_ref, qi_ref, w_ref, k_ref, vt_ref, ki_ref, lts_ref, o_ref, sc_ref, lg_ref, *, qb, kt, topk):
    i = pl.program_id(1)
    q0 = i * qb
    nkt = (q0 + qb + kt - 1) // kt
    kf = float(topk)
    sg = 8 * SUBLANES
    wrow = w_ref[0]
    tq = q0 + lax.broadcasted_iota(I32, (1, qb), 1)

    def to_key(x):
        bits = pltpu.bitcast(x, I32)
        return bits ^ ((bits >> 31) & 0x7FFFFFFF)

    def scores(t, gmax):
        k0 = pl.multiple_of(t * kt, kt)
        sb = LANES
        gm = [gmax[j * sb:(j + 1) * sb] for j in range(topk // sb)]
        for r in range(kt // sb):
            kit = ki_ref[0, pl.ds(k0 + r * sb, sb), :]
            acc = jnp.zeros((sb, qb), F32)
            for h in range(IDX_HEADS):
                acc = acc + jnp.maximum(_dot_nt(kit, qi_ref[0, h]), 0.0) * wrow[h:h + 1]
            acc = jnp.where(acc == 0.0, 0.0, acc)
            causal = (k0 + r * sb + lax.broadcasted_iota(I32, (sb, qb), 0)) <= tq
            sc_ref[t, r * sb:(r + 1) * sb, :] = jnp.where(causal, to_key(acc), INT_MIN)
            j = ((r * sb) % topk) // sb
            gm[j] = jnp.maximum(gm[j], jnp.where(causal, acc, -jnp.inf))
        return jnp.concatenate(gm, axis=0)

    gmax = lax.fori_loop(0, nkt, scores, jnp.full((topk, qb), -jnp.inf, F32))

    def count(pred):
        def body(t, c):
            m = jnp.where(pred(sc_ref[t]), 1.0, 0.0)
            return c + jnp.sum(m.reshape(kt // sg, sg, qb), axis=0)
        c = lax.fori_loop(0, nkt, body, jnp.zeros((sg, qb), F32))
        return jnp.sum(c, axis=0, keepdims=True)

    need = (tq + 1) > topk
    lo0 = to_key(jnp.min(gmax, axis=0, keepdims=True))
    hi0 = to_key(jnp.max(gmax, axis=0, keepdims=True))
    done0 = jnp.where(jnp.logical_and(need, lo0 < hi0), 0.0, 1.0)

    def bs_cond(c):
        it, _, _, _, done, _ = c
        return jnp.logical_and(it < 34, jnp.sum(1.0 - done) > 0.0)

    def bs_step(lo, hi, thr, done, below, probe=None):
        mid = (lo >> 1) + (hi >> 1) + (((lo & 1) + (hi & 1) + 1) >> 1)
        if probe is not None:
            mid = jnp.where(jnp.logical_and(lo < probe, probe <= hi), probe, mid)
        cnt = count(lambda kk: kk >= mid)
        ge = cnt >= kf
        hit = cnt == kf
        near = cnt == kf - 1.0
        lo = jnp.where(ge, mid, lo)
        hi = jnp.where(ge, hi, mid - 1)
        fin = jnp.where(hit, 1.0, jnp.where(near, 1.0, jnp.where(lo == hi, 1.0, 0.0)))
        val = jnp.where(hit, mid, jnp.where(near, mid - 1, lo))
        live = done == 0.0
        thr = jnp.where(live, val, thr)
        below = jnp.where(live, jnp.where(near, 1.0, 0.0), below)
        return lo, hi, thr, jnp.maximum(done, fin), below

    def bs_body(c):
        it = c[0]
        return (it + 2,) + bs_step(*bs_step(*c[1:]))

    st = bs_step(*bs_step(lo0, hi0, lo0, done0, jnp.zeros((1, qb), F32), probe=1), probe=0)
    _, _, _, thr, _, below = lax.while_loop(bs_cond, bs_body, (jnp.int32(2),) + st)

    def below_max(t, c):
        kk = sc_ref[t]
        cand = jnp.where(kk <= thr, kk, NEG_INF_KEY)
        return jnp.maximum(c, jnp.max(cand.reshape(kt // sg, sg, qb), axis=0))

    bm = lax.fori_loop(0, nkt, below_max, jnp.full((sg, qb), NEG_INF_KEY, I32))
    bmax = to_key(jnp.max(pltpu.bitcast(bm ^ ((bm >> 31) & 0x7FFFFFFF), F32), axis=0, keepdims=True))
    thr = jnp.where(below > 0.0, bmax, thr)
    thr = jnp.where(need, thr, INT_MIN + 1)

    cge = count(lambda kk: kk >= thr)
    tie = jnp.logical_and(need, cge > kf)

    @pl.when(jnp.sum(jnp.where(tie, 1.0, 0.0)) > 0.0)
    def _():
        cgt = count(lambda kk: kk > thr)
        room = kf - cgt

        def body(t, seen):
            kk = sc_ref[t]
            eq = jnp.logical_and(kk == thr, tie)
            eqf = jnp.where(eq, 1.0, 0.0)
            before = _dot(lts_ref[...], eqf.astype(BF16)) + seen
            sc_ref[t] = jnp.where(jnp.logical_and(eq, before >= room), INT_MIN, kk)
            return seen + jnp.sum(eqf, axis=0, keepdims=True)

        lax.fori_loop(0, nkt, body, jnp.zeros((1, qb), F32))

    nh = DSA_HEADS

    def logits(t, c):
        k0 = pl.multiple_of(t * kt, kt)
        kt_ = k_ref[0, pl.ds(k0, kt), :]
        bias = jnp.where(sc_ref[t] >= thr, 0.0, -jnp.inf)
        out = []
        for h in range(nh):
            lg = _dot_nt(kt_, q_ref[0, h]) + bias
            lg_ref[t, h] = lg
            out.append(jnp.maximum(c[h], jnp.max(lg.reshape(kt // sg, sg, qb), axis=0)))
        return tuple(out)

    mparts = lax.fori_loop(0, nkt, logits, (jnp.full((sg, qb), -jnp.inf, F32),) * nh)
    ms = []
    for h in range(nh):
        m = jnp.max(mparts[h], axis=0, keepdims=True)
        ms.append(jnp.where(m == -jnp.inf, 0.0, m))

    def attend(t, c):
        vt = vt_ref[0, t]
        out = []
        for h in range(nh):
            p = jnp.exp2((lg_ref[t, h] - ms[h]).astype(BF16))
            out.append(c[h] + _dot(vt, p))
        return tuple(out)

    res = lax.fori_loop(0, nkt, attend, (jnp.zeros((vt_ref.shape[2], qb), F32),) * nh)
    dh = DSA_HEAD_DIM
    for h in range(nh):
        ot = res[h][0:dh] / res[h][dh:dh + 1]
        o_ref[0, :, h * dh:(h + 1) * dh] = ot.T.astype(o_ref.dtype)


def dsa_attend(qs, qis, wt, kr, vt4, kir, *, qb, kt, topk):
    b, nh, s, dh = qs.shape
    nkt = s // kt
    assert topk % LANES == 0 and kt % topk == 0 and s % kt == 0 and s % qb == 0
    ri = np.arange(kt)
    lts = jnp.asarray((ri[None, :] < ri[:, None]).astype(np.float32), BF16)
    return pl.pallas_call(
        functools.partial(_dsa_kernel, qb=qb, kt=kt, topk=topk),
        grid=(b, s // qb),
        in_specs=[
            pl.BlockSpec((1, nh, qb, dh), lambda bi, i: (bi, 0, i, 0)),
            pl.BlockSpec((1, IDX_HEADS, qb, IDX_DIM), lambda bi, i: (bi, 0, i, 0)),
            pl.BlockSpec((1, IDX_HEADS, qb), lambda bi, i: (bi, 0, i)),
            pl.BlockSpec((1, s, dh), lambda bi, i: (bi, 0, 0)),
            pl.BlockSpec((1, nkt, VT_ROWS, kt), lambda bi, i: (bi, 0, 0, 0)),
            pl.BlockSpec((1, s, IDX_DIM), lambda bi, i: (bi, 0, 0)),
            pl.BlockSpec((kt, kt), lambda bi, i: (0, 0)),
        ],
        out_specs=pl.BlockSpec((1, qb, nh * dh), lambda bi, i: (bi, i, 0)),
        out_shape=jax.ShapeDtypeStruct((b, s, nh * dh), BF16),
        scratch_shapes=[pltpu.VMEM((nkt, kt, qb), I32), pltpu.VMEM((nkt, nh, kt, qb), F32)],
        compiler_params=_cparams(("parallel", "arbitrary")),
        name="dsa_attend",
    )(qs, qis, wt, kr, vt4, kir, lts)


def _merge_kernel(x_ref, ya_ref, yb_ref, yc_ref, yd_ref, g0_ref, g1_ref, g2_ref, g3_ref, wb_ref, wo_ref, o_ref):
    ys = (ya_ref, yb_ref, yc_ref, yd_ref)
    gs = (g0_ref, g1_ref, g2_ref, g3_ref)
    merged = jnp.zeros(o_ref.shape, F32)
    for i in range(N_BRANCH):
        merged = merged + _sigmoid(gs[i][...].astype(F32)) * _dot(ys[i][...], wb_ref[i])
    o_ref[...] = x_ref[...] + _dot(merged.astype(BF16), wo_ref[...])


def merge_out(x2, ya, yb, yc, yd, proj, w_branch, w_out, *, tm):
    t, d = x2.shape
    w_ = BRANCH_W
    yspec = pl.BlockSpec((tm, w_), lambda i: (i, 0))
    gspec = lambda n: pl.BlockSpec((tm, d), lambda i: (i, P_GATES // d + n))
    return pl.pallas_call(
        _merge_kernel,
        grid=(t // tm,),
        in_specs=[pl.BlockSpec((tm, d), lambda i: (i, 0)), yspec, yspec, yspec, yspec,
                  gspec(0), gspec(1), gspec(2), gspec(3),
                  pl.BlockSpec((N_BRANCH, w_, d), lambda i: (0, 0, 0)),
                  pl.BlockSpec((d, d), lambda i: (0, 0))],
        out_specs=pl.BlockSpec((tm, d), lambda i: (i, 0)),
        out_shape=jax.ShapeDtypeStruct((t, d), F32),
        compiler_params=_cparams(("parallel",)),
        name="merge_out",
    )(x2, ya, yb, yc, yd, proj, proj, proj, proj, w_branch, w_out)


def _cross_kernel(x_ref, g_ref, wq_ref, kv_ref, wo_ref, o_ref):
    x = x_ref[0]
    hn = _rms(x, g_ref[...]).astype(BF16)
    q = _dot(hn, wq_ref[...]) * (X_HEAD_DIM ** -0.5)
    kv = kv_ref[0]
    hd = X_HEADS * X_HEAD_DIM
    outs = []
    for h in range(X_HEADS):
        cs = slice(h * X_HEAD_DIM, (h + 1) * X_HEAD_DIM)
        lg = _dot_nt(q[:, cs].astype(BF16), kv[:, cs])
        p = jnp.exp(lg - jnp.max(lg, axis=-1, keepdims=True))
        p = p / jnp.sum(p, axis=-1, keepdims=True)
        outs.append(_dot(p.astype(BF16), kv[:, hd + h * X_HEAD_DIM:hd + (h + 1) * X_HEAD_DIM]))
    o = jnp.concatenate(outs, axis=1).astype(BF16)
    o_ref[0] = x + _dot(o, wo_ref[...])


def cross_attn(x3, g, w_q, kv, w_o, *, tm):
    b, s, d = x3.shape
    m = kv.shape[1]
    hd = X_HEADS * X_HEAD_DIM
    return pl.pallas_call(
        _cross_kernel,
        grid=(b, s // tm),
        in_specs=[pl.BlockSpec((1, tm, d), lambda bi, i: (bi, i, 0)),
                  pl.BlockSpec((1, d), lambda bi, i: (0, 0)),
                  pl.BlockSpec((d, hd), lambda bi, i: (0, 0)),
                  pl.BlockSpec((1, m, 2 * hd), lambda bi, i: (bi, 0, 0)),
                  pl.BlockSpec((hd, d), lambda bi, i: (0, 0))],
        out_specs=pl.BlockSpec((1, tm, d), lambda bi, i: (bi, i, 0)),
        out_shape=jax.ShapeDtypeStruct((b, s, d), F32),
        compiler_params=_cparams(("parallel", "arbitrary")),
        name="cross_attn",
    )(x3, g, w_q, kv, w_o)


def _router_kernel(x_ref, g_ref, wr_ref, br_ref, hn_ref, cmb_ref):
    hn = _rms(x_ref[...], g_ref[...])
    hn_ref[...] = hn.astype(BF16)
    lg = jnp.dot(hn, wr_ref[...], precision=lax.Precision.HIGHEST, preferred_element_type=F32) + br_ref[...]
    lane = lax.broadcasted_iota(I32, lg.shape, 1)
    lanef = lane.astype(F32)
    ninf = -jnp.inf
    big = float(LANES)
    first = lambda mask: jnp.min(jnp.where(mask, lanef, big), axis=-1, keepdims=True)

    isg = jnp.logical_and(lane >= N_EXPERTS, lane < N_EXPERTS + N_GROUPS)
    gl = jnp.where(isg, lg, ninf)
    gmax = jnp.max(gl, axis=-1, keepdims=True)
    gsel = first(gl == gmax) - float(N_EXPERTS)
    pg = 1.0 / jnp.sum(jnp.exp(gl - gmax), axis=-1, keepdims=True)

    ise = jnp.floor(lanef * (1.0 / EXPERTS_PER_GROUP)) == gsel
    el = jnp.where(ise, lg, ninf)
    e1 = jnp.max(el, axis=-1, keepdims=True)
    i1 = first(el == e1)
    el2 = jnp.where(lanef == i1, ninf, el)
    e2 = jnp.max(el2, axis=-1, keepdims=True)
    i2 = first(el2 == e2)
    d = jnp.exp(e2 - e1)
    w1 = 1.0 / (1.0 + d)
    w2 = d / (1.0 + d)
    cmb = jnp.where(lanef == i1, pg * w1, jnp.where(lanef == i2, pg * w2, 0.0))
    a = jnp.minimum(i1, i2) - EXPERTS_PER_GROUP * gsel
    b = jnp.maximum(i1, i2) - EXPERTS_PER_GROUP * gsel
    rank = jnp.where(a == 0.0, b - 1.0, jnp.where(a == 1.0, jnp.where(b == 3.0, 3.0, 4.0), 5.0))
    cmb_ref[...] = jnp.where(lane == N_EXPERTS, gsel * MOE_PAIRS + rank, cmb)


def moe_router(x2, g, wr, br, *, tm):
    t, d = x2.shape
    return pl.pallas_call(
        _router_kernel,
        grid=(t // tm,),
        in_specs=[pl.BlockSpec((tm, d), lambda i: (i, 0)), pl.BlockSpec((1, d), lambda i: (0, 0)),
                  pl.BlockSpec((d, LANES), lambda i: (0, 0)), pl.BlockSpec((1, LANES), lambda i: (0, 0))],
        out_specs=[pl.BlockSpec((tm, d), lambda i: (i, 0)), pl.BlockSpec((tm, LANES), lambda i: (i, 0))],
        out_shape=[jax.ShapeDtypeStruct((t, d), BF16), jax.ShapeDtypeStruct((t, LANES), F32)],
        compiler_params=_cparams(("parallel",)),
        name="moe_router",
    )(x2, g, wr, br)


MOE_PAD = 2 * SUBLANES
MOE_SB = 256
MOE_PAIRS = 6
MOE_CLASSES = N_GROUPS * MOE_PAIRS
MOE_FIRST = (0, 0, 1, 2)
MOE_LAST = (2, 4, 5, 5)
MOE_CHUNK = (160, 256, 256, 208)
MOE_SEG = 64


def _moe_rmax(tm):
    return -(-(tm + N_GROUPS * MOE_PAD + max(MOE_CHUNK)) // LANES) * LANES


def _moe_class_matrices():
    c = np.arange(LANES)
    valid = c < MOE_CLASSES
    grp = c // MOE_PAIRS
    both = valid[:, None] & valid[None, :]
    before = both & (c[:, None] < c[None, :])
    same = both & (grp[:, None] == grp[None, :])
    lead = both & (c[:, None] % MOE_PAIRS == 0) & (grp[:, None] < grp[None, :])
    return jnp.asarray(np.stack([before, same, lead]).astype(np.float32))


def _moe_sort_kernel(hn_ref, cmb_ref, lts_ref, cm_ref, hs_ref, cs_ref, meta_ref, seg_ref, *, tm, rmax):
    cmb = cmb_ref[...]
    lanef = lax.broadcasted_iota(I32, (tm, LANES), 1).astype(F32)
    oh = jnp.where(lanef == cmb[:, N_EXPERTS:N_EXPERTS + 1], 1.0, 0.0)
    seen = jnp.zeros((1, LANES), F32)
    pres = []
    for sblk in range(tm // MOE_SB):
        ohs = oh[sblk * MOE_SB:(sblk + 1) * MOE_SB]
        pres.append(_dot(lts_ref[...], ohs.astype(BF16)) + seen)
        seen = seen + jnp.sum(ohs, axis=0, keepdims=True)
    pre = jnp.concatenate(pres, axis=0)
    cnt = jnp.broadcast_to(seen, (SUBLANES, LANES))
    exact = functools.partial(jnp.dot, precision=lax.Precision.HIGHEST, preferred_element_type=F32)
    tot = exact(cnt, cm_ref[1])
    padamt = jnp.ceil(tot * (1.0 / MOE_PAD)) * MOE_PAD - tot
    off = exact(cnt, cm_ref[0]) + exact(padamt, cm_ref[2])
    dest = jnp.sum(oh * (off[0:1] + pre), axis=-1, keepdims=True)
    destb = jnp.broadcast_to(dest, (tm, LANES))
    meta_ref[...] = destb
    row8 = lax.broadcasted_iota(I32, (SUBLANES, LANES), 0)
    seg_ref[0] = jnp.where(row8 == 0, off, jnp.where(row8 == 1, cnt, 0.0)).astype(I32)

    dest_row = destb.T[0:1]
    hn = hn_ref[...]
    chi = cmb.astype(BF16)
    clo = (cmb - chi.astype(F32)).astype(BF16)

    def blk(rb, carry):
        r0 = pl.multiple_of(rb * LANES, LANES)
        rows = (r0 + lax.broadcasted_iota(I32, (LANES, tm), 0)).astype(F32)
        p = jnp.where(rows == dest_row, 1.0, 0.0).astype(BF16)
        hs_ref[0, pl.ds(r0, LANES), :] = _dot(p, hn).astype(BF16)
        cs_ref[0, pl.ds(r0, LANES), :] = _dot(p, chi) + _dot(p, clo)
        return carry

    lax.fori_loop(0, rmax // LANES, blk, 0)


def moe_sort(hn, cmb, *, tm):
    t, d = hn.shape
    nt = t // tm
    rmax = _moe_rmax(tm)
    ri = np.arange(MOE_SB)
    lts = jnp.asarray((ri[None, :] < ri[:, None]).astype(np.float32), BF16)
    return pl.pallas_call(
        functools.partial(_moe_sort_kernel, tm=tm, rmax=rmax),
        grid=(nt,),
        in_specs=[pl.BlockSpec((tm, d), lambda i: (i, 0)), pl.BlockSpec((tm, LANES), lambda i: (i, 0)),
                  pl.BlockSpec((MOE_SB, MOE_SB), lambda i: (0, 0)),
                  pl.BlockSpec((3, LANES, LANES), lambda i: (0, 0, 0))],
        out_specs=[pl.BlockSpec((1, rmax, d), lambda i: (i, 0, 0)),
                   pl.BlockSpec((1, rmax, LANES), lambda i: (i, 0, 0)),
                   pl.BlockSpec((tm, LANES), lambda i: (i, 0)),
                   pl.BlockSpec((1, SUBLANES, LANES), lambda i: (i, 0, 0))],
        out_shape=[jax.ShapeDtypeStruct((nt, rmax, d), BF16), jax.ShapeDtypeStruct((nt, rmax, LANES), F32),
                   jax.ShapeDtypeStruct((t, LANES), F32), jax.ShapeDtypeStruct((nt, SUBLANES, LANES), I32)],
        compiler_params=_cparams(("parallel",)),
        name="moe_sort",
    )(hn, cmb, lts, _moe_class_matrices())


def _moe_expert_kernel(seg_ref, hs_ref, cs_ref, wg_ref, wu_ref, wd_ref, y_ref, acc_ref, *, ns):
    i = pl.program_id(0)
    e = pl.program_id(1)

    @pl.when(e == 0)
    def _():
        acc_ref[...] = jnp.zeros_like(acc_ref)

    g = e // EXPERTS_PER_GROUP

    def run(local, ch):
        for s in range(ns):
            base = (i * ns + s) * MOE_SEG + g * MOE_PAIRS
            start = seg_ref[base + MOE_FIRST[local]]
            end = seg_ref[base + MOE_LAST[local]] + seg_ref[base + MOE_SEG // 2 + MOE_LAST[local]]
            start = (start // MOE_PAD) * MOE_PAD

            def chunk(j, carry, s=s, start=start):
                r0 = pl.multiple_of(start + j * ch, MOE_PAD)
                hsl = hs_ref[s, pl.ds(r0, ch), :]
                gg = _dot(hsl, wg_ref[0])
                uu = _dot(hsl, wu_ref[0])
                c = cs_ref[s, pl.ds(r0, ch), :]
                lane = lax.broadcasted_iota(I32, c.shape, 1)
                cc = jnp.sum(jnp.where(lane == e, c, 0.0), axis=-1, keepdims=True)
                he = (gg * _sigmoid(gg)) * uu * cc
                acc_ref[s, pl.ds(r0, ch), :] += _dot(he.astype(BF16), wd_ref[0])
                return carry

            lax.fori_loop(0, (end - start + ch - 1) // ch, chunk, 0)

    for local in range(EXPERTS_PER_GROUP):
        pl.when(e % EXPERTS_PER_GROUP == local)(functools.partial(run, local, MOE_CHUNK[local]))

    @pl.when(e == pl.num_programs(1) - 1)
    def _():
        y_ref[...] = acc_ref[...].astype(BF16)


def moe_experts(seg, hs, cs, wg, wu, wd, *, ns):
    nt, rmax, d = hs.shape
    ne, _, f = wg.shape
    return pl.pallas_call(
        functools.partial(_moe_expert_kernel, ns=ns),
        grid_spec=pltpu.PrefetchScalarGridSpec(
            num_scalar_prefetch=1,
            grid=(nt // ns, ne),
            in_specs=[pl.BlockSpec((ns, rmax, d), lambda i, e, sref: (i, 0, 0)),
                      pl.BlockSpec((ns, rmax, LANES), lambda i, e, sref: (i, 0, 0)),
                      pl.BlockSpec((1, d, f), lambda i, e, sref: (e, 0, 0)),
                      pl.BlockSpec((1, d, f), lambda i, e, sref: (e, 0, 0)),
                      pl.BlockSpec((1, f, d), lambda i, e, sref: (e, 0, 0))],
            out_specs=pl.BlockSpec((ns, rmax, d), lambda i, e, sref: (i, 0, 0)),
            scratch_shapes=[pltpu.VMEM((ns, rmax, d), F32)]),
        out_shape=jax.ShapeDtypeStruct((nt, rmax, d), BF16),
        compiler_params=_cparams(("parallel", "arbitrary")),
        name="moe_experts",
    )(seg, hs, cs, wg, wu, wd)


def _moe_combine_kernel(x_ref, meta_ref, y_ref, gf_ref, o_ref, *, rmax, final_norm):
    tb = x_ref.shape[0]
    lanef = lax.broadcasted_iota(I32, (tb, rmax), 1).astype(F32)
    pt = jnp.where(lanef == meta_ref[:, 0:1], 1.0, 0.0).astype(BF16)
    out = x_ref[...] + _dot(pt, y_ref[0])
    o_ref[...] = _rms(out, gf_ref[...]) if final_norm else out


def moe_combine(x2, meta, y, g_final, *, tm, tb, final_norm):
    t, d = x2.shape
    nt, rmax, _ = y.shape
    nb = tm // tb
    return pl.pallas_call(
        functools.partial(_moe_combine_kernel, rmax=rmax, final_norm=final_norm),
        grid=(nt, nb),
        in_specs=[pl.BlockSpec((tb, d), lambda i, j: (i * nb + j, 0)),
                  pl.BlockSpec((tb, LANES), lambda i, j: (i * nb + j, 0)),
                  pl.BlockSpec((1, rmax, d), lambda i, j: (i, 0, 0)),
                  pl.BlockSpec((1, d), lambda i, j: (0, 0))],
        out_specs=pl.BlockSpec((tb, d), lambda i, j: (i * nb + j, 0)),
        out_shape=jax.ShapeDtypeStruct((t, d), F32),
        compiler_params=_cparams(("parallel", "arbitrary")),
        name="moe_combine",
    )(x2, meta, y, g_final)


def _permute_w_in(w):
    d = w.shape[0]
    o = 0
    seg = {}
    for name, width in (("c", 768), ("gla", 1024), ("ga", GLA_RANK), ("s5u", 256), ("dq", 256), ("dk", 64),
                        ("dv", 64), ("iq", 256), ("ik", IDX_DIM), ("iw", IDX_HEADS), ("gates", 4096)):
        seg[name] = w[:, o:o + width]
        o += width
    z = lambda n: jnp.zeros((d, n), w.dtype)
    cols = [seg["c"], seg["s5u"], seg["gla"], seg["dq"], seg["iq"], seg["dk"], seg["dv"],
            seg["ik"], seg["iw"], z(LANES - IDX_DIM - IDX_HEADS), seg["ga"], z(LANES - GLA_RANK),
            z(P_GATES - P_GA - LANES), seg["gates"]]
    out = jnp.concatenate(cols, axis=1).astype(BF16)
    assert out.shape[1] == P_TOTAL
    return out


def _s5_matrices(bb_re, bb_im, c_re, c_im):
    g, p, c = S5_GROUPS, S5_STATE, S5_GROUP
    n = g * p
    rows_g = jnp.arange(g * c) // c
    cols_g = jnp.arange(n) // p
    mask = (rows_g[:, None] == cols_g[None, :]).astype(F32)
    bm = jnp.concatenate([jnp.tile(bb_re, (g, 1)) * mask, jnp.tile(bb_im, (g, 1)) * mask], axis=1)
    ct = lambda a: jnp.tile(jnp.transpose(a, (0, 2, 1)).reshape(n, c), (1, g)) * mask.T
    cm = jnp.concatenate([ct(c_re), -ct(c_im)], axis=0)
    return bm.astype(BF16), cm.astype(BF16)


def _pick(s, pref):
    for c in pref:
        if s % c == 0:
            return c
    return s


def kernel(x, mem, positions, norm_mix, w_in, conv_w, conv_b, gla_a_up, gla_a_b, gla_norm, s5_lambda_re,
           s5_lambda_im, s5_log_dt, s5_b_re, s5_b_im, s5_c_re, s5_c_im, s5_d, s5_w_glu, s5_b_glu, w_branch,
           w_out, norm_cross, w_cq, w_ckv, w_co, norm_ffn, w_route_group, b_route_group, w_route_expert,
           b_route_expert, w_e_gate, w_e_up, w_e_down, norm_mem, norm_final):
    b, s, d = x.shape
    t = b * s
    m = mem.shape[1]
    depth = w_in.shape[0]
    topk = min(DSA_TOPK, s // 4)
    ts = _pick(s, (512, 256, 128))
    tm = _pick(t, (1024, 512, 256, 128))
    row = lambda a: a.reshape(1, -1)

    tabs = rope_tables(positions, ts=ts)
    x2 = x.reshape(t, d)
    mem2 = mem.reshape(b * m, d)
    for l in range(depth):
        proj = norm_matmul(x2, row(norm_mix[l]), _permute_w_in(w_in[l]), tm=_pick(t, (2048, 1024, 512)), tn=1024)
        proj3 = proj.reshape(b, s, P_TOTAL)
        ya = conv_branch(proj3, conv_w[l], row(conv_b[l]), ts=ts)
        a_up_p = jnp.concatenate(
            [gla_a_up[l], jnp.zeros((LANES - GLA_RANK, gla_a_up.shape[2]), F32)], axis=0).astype(BF16)
        yb = gla_branch(proj3, a_up_p, row(gla_a_b[l]), row(gla_norm[l]), tt=ts)
        bb_re, bb_im, tab = s5_params(s5_lambda_re[l], s5_lambda_im[l], s5_log_dt[l], s5_b_re[l], s5_b_im[l])
        bm, cm = _s5_matrices(bb_re, bb_im, s5_c_re[l], s5_c_im[l])
        yc = s5_branch(proj3, bm, tab, cm, row(s5_d[l]), s5_w_glu[l].astype(BF16), row(s5_b_glu[l]), tt=ts)
        qs, qis, kr, vt4, kir, wt = dsa_prep(proj3, tabs, ts=ts)
        yd = dsa_attend(qs, qis, wt, kr, vt4, kir, qb=_pick(s, (2 * Q_BLOCK, Q_BLOCK)), kt=ts, topk=topk)
        w2 = lambda a: a.reshape(t, a.shape[-1])
        x2 = merge_out(x2, w2(ya), w2(yb), w2(yc), w2(yd), proj, w_branch[l].astype(BF16),
                       w_out[l].astype(BF16), tm=min(tm, 512))
        kv = norm_matmul(mem2, row(norm_mem), w_ckv[l].astype(BF16), tm=_pick(b * m, (1024, 512, 256)), tn=1024)
        x2 = cross_attn(x2.reshape(b, s, d), row(norm_cross[l]), w_cq[l].astype(BF16),
                        kv.reshape(b, m, -1), w_co[l].astype(BF16), tm=ts).reshape(t, d)
        pad = jnp.zeros((d, LANES - N_EXPERTS - N_GROUPS), F32)
        wr = jnp.concatenate([w_route_expert[l], w_route_group[l], pad], axis=1)
        br = jnp.concatenate([b_route_expert[l], b_route_group[l], pad[0]], axis=0).reshape(1, LANES)
        hn, cmb = moe_router(x2, row(norm_ffn[l]), wr, br, tm=min(tm, 512))
        tmoe = _pick(t, (1024, 512))
        hs, cs, meta, seg = moe_sort(hn, cmb, tm=tmoe)
        half = MOE_SEG // 2
        seg1 = jnp.concatenate([seg[:, 0, :half], seg[:, 1, :half]], axis=1).reshape(-1)
        y = moe_experts(seg1, hs, cs, w_e_gate[l].astype(BF16), w_e_up[l].astype(BF16), w_e_down[l].astype(BF16),
                        ns=2 if (t // tmoe) % 2 == 0 else 1)
        x2 = moe_combine(x2, meta, y, row(norm_final), tm=tmoe, tb=min(tmoe, 512), final_norm=(l == depth - 1))
    return x2.reshape(b, s, d)
```

```python
import functools
import math

import numpy as np
import jax
import jax.numpy as jnp
from jax import lax
from jax.experimental import pallas as pl
from jax.experimental.pallas import tpu as pltpu

F32 = jnp.float32
BF16 = jnp.bfloat16
I32 = jnp.int32

D_MODEL = 1024
DEPTH = 2
EPS = 1e-6
N_BRANCH = 4
BRANCH_W = 256
CONV_W = 3
GLA_HEADS = 4
GLA_DK = 64
GLA_DV = 64
GLA_RANK = 16
GLA_TAU = 16.0
GLA_CHUNK = 64
S5_GROUP = 16
S5_GROUPS = BRANCH_W // S5_GROUP
S5_STATE = 64
DSA_HEADS = 4
DSA_HEAD_DIM = 64
IDX_HEADS = 8
IDX_DIM = 32
DSA_TOPK = 256
Q_BLOCK = 128
ROPE_THETA = 500000.0
ROPE_FRAC = 4
X_HEADS = 4
X_HEAD_DIM = 128
N_GROUPS = 4
EXPERTS_PER_GROUP = 4
N_EXPERTS = N_GROUPS * EXPERTS_PER_GROUP
D_FF_EXPERT = 512

LANES = 128
SUBLANES = 8
VMEM_LIMIT = 48 * 1024 * 1024

P_CONV = 0
P_S5U = 768
P_GLA = 1024
P_DQ = 2048
P_IQ = 2304
P_KV = 2560
P_IKW = 2688
P_GA = 2816
P_GATES = 3072
P_TOTAL = P_GATES + N_BRANCH * D_MODEL

INT_MIN = -2147483648
NEG_INF_KEY = INT_MIN + 0x7FFFFF
VT_ROWS = DSA_HEAD_DIM + 2 * SUBLANES


def _cparams(sem):
    return pltpu.CompilerParams(dimension_semantics=sem, vmem_limit_bytes=VMEM_LIMIT)


def _dot(a, b):
    return jnp.dot(a, b, preferred_element_type=F32)


def _dot_nt(a, b):
    return lax.dot_general(a, b, (((1,), (1,)), ((), ())), preferred_element_type=F32)


def _dot_tn(a, b):
    return lax.dot_general(a, b, (((0,), (0,)), ((), ())), preferred_element_type=F32)


def _split_dot(exact_bf16, x):
    hi = x.astype(BF16)
    lo = (x - hi.astype(F32)).astype(BF16)
    return _dot(exact_bf16, hi) + _dot(exact_bf16, lo)


def _split_dot_r(x, exact_bf16):
    hi = x.astype(BF16)
    lo = (x - hi.astype(F32)).astype(BF16)
    return _dot(hi, exact_bf16) + _dot(lo, exact_bf16)


def _rms(x, g):
    return x * lax.rsqrt(jnp.mean(x * x, axis=-1, keepdims=True) + EPS) * g


def _sigmoid(x):
    return 1.0 / (1.0 + jnp.exp(-x))


def _norm_matmul_kernel(x_ref, g_ref, w_ref, o_ref, hn_ref):
    @pl.when(pl.program_id(1) == 0)
    def _():
        hn_ref[...] = _rms(x_ref[...], g_ref[...]).astype(BF16)

    o_ref[...] = _dot(hn_ref[...], w_ref[...]).astype(o_ref.dtype)


def norm_matmul(x, g, w, *, tm, tn, out_dtype=BF16):
    t, d = x.shape
    n = w.shape[1]
    return pl.pallas_call(
        _norm_matmul_kernel,
        grid=(t // tm, n // tn),
        in_specs=[
            pl.BlockSpec((tm, d), lambda i, j: (i, 0)),
            pl.BlockSpec((1, d), lambda i, j: (0, 0)),
            pl.BlockSpec((d, tn), lambda i, j: (0, j)),
        ],
        out_specs=pl.BlockSpec((tm, tn), lambda i, j: (i, j)),
        out_shape=jax.ShapeDtypeStruct((t, n), out_dtype),
        scratch_shapes=[pltpu.VMEM((tm, d), BF16)],
        compiler_params=_cparams(("parallel", "arbitrary")),
        name="norm_matmul",
    )(x, g, w)


def _conv_kernel(cur_ref, prev_ref, w_ref, b_ref, o_ref):
    i = pl.program_id(1)
    w_ = BRANCH_W
    cur = cur_ref[0].astype(F32)
    u = cur[:, 2 * w_:3 * w_] * cur[:, 0:w_]
    pv = prev_ref[0].astype(F32)
    pu = pv[:, 2 * w_:3 * w_] * pv[:, 0:w_]
    pu = jnp.where(i > 0, pu, 0.0)
    row = lax.broadcasted_iota(I32, u.shape, 0)
    u1 = jnp.where(row == 0, pu[7:8], pltpu.roll(u, 1, 0))
    u2 = jnp.where(row == 0, pu[6:7], jnp.where(row == 1, pu[7:8], pltpu.roll(u, 2, 0)))
    w = w_ref[...]
    y = w[0:1] * u2 + w[1:2] * u1 + w[2:3] * u + b_ref[...]
    o_ref[0] = (cur[:, w_:2 * w_] * y).astype(o_ref.dtype)


def conv_branch(proj3, conv_w, conv_b, *, ts):
    b, s, _ = proj3.shape
    wc = 3 * BRANCH_W
    hb = ts // SUBLANES
    return pl.pallas_call(
        _conv_kernel,
        grid=(b, s // ts),
        in_specs=[
            pl.BlockSpec((1, ts, wc), lambda bi, i: (bi, i, P_CONV // wc)),
            pl.BlockSpec((1, SUBLANES, wc), lambda bi, i: (bi, jnp.maximum(i * hb - 1, 0), P_CONV // wc)),
            pl.BlockSpec((CONV_W, BRANCH_W), lambda bi, i: (0, 0)),
            pl.BlockSpec((1, BRANCH_W), lambda bi, i: (0, 0)),
        ],
        out_specs=pl.BlockSpec((1, ts, BRANCH_W), lambda bi, i: (bi, i, 0)),
        out_shape=jax.ShapeDtypeStruct((b, s, BRANCH_W), BF16),
        compiler_params=_cparams(("parallel", "arbitrary")),
        name="conv_branch",
    )(proj3, proj3, conv_w, conv_b)


def _gla_kernel(g_ref, a_ref, aup_ref, ab_ref, gn_ref, ltb_ref, ltf_ref, bob_ref, bd_ref, bdb_ref,
                o_ref, st_ref, *, tt):
    hw = GLA_HEADS * GLA_DK
    c = GLA_CHUNK

    @pl.when(pl.program_id(1) == 0)
    def _():
        st_ref[...] = jnp.zeros_like(st_ref)

    blk = g_ref[0]
    q = blk[:, 0:hw].astype(F32)
    k = blk[:, hw:2 * hw].astype(F32)
    vb = blk[:, 2 * hw:3 * hw]
    v = vb.astype(F32)
    r = blk[:, 3 * hw:4 * hw].astype(F32)

    pre = _dot(a_ref[0], aup_ref[...]) + ab_ref[...]
    la = (jnp.minimum(pre, 0.0) - jnp.log(1.0 + jnp.exp(-jnp.abs(pre)))) * (1.0 / GLA_TAU)
    cum = _split_dot(ltb_ref[...], la)
    tot = _split_dot(bob_ref[...], la)
    q_dec = q * (GLA_DK ** -0.5) * jnp.exp(cum)
    k_inv = (k * jnp.exp(-cum)).astype(BF16)
    k_end = (k * jnp.exp(tot - cum)).astype(BF16)
    qdb = q_dec.astype(BF16)

    lane = lax.broadcasted_iota(I32, (1, hw), 1)
    ltmask = ltf_ref[...] > 0.0
    o = jnp.zeros((tt, hw), F32)
    for h in range(GLA_HEADS):
        hm = (lane // GLA_DK) == h
        qh = jnp.where(hm, q_dec, 0.0).astype(BF16)
        att = jnp.where(ltmask, _dot_nt(qh, k_inv), 0.0)
        vh = jnp.where(hm, v, 0.0).astype(BF16)
        o = o + _dot(att.astype(BF16), vh)

    st = st_ref[...]
    bd = bd_ref[...]
    inter = []
    for n in range(tt // c):
        rows = slice(n * c, (n + 1) * c)
        inter.append(_dot_nt(qdb[rows], st.astype(BF16)))
        dec = jnp.exp(tot[n * c:n * c + 1, :])
        st = st * dec + _dot_tn(vb[rows], k_end[rows]) * bd
    st_ref[...] = st
    o = o + jnp.concatenate(inter, axis=0)

    msq = _split_dot_r(o * o, bdb_ref[...]) * (1.0 / GLA_DV)
    y = o * lax.rsqrt(msq + EPS) * gn_ref[...] * (r * _sigmoid(r))
    o_ref[0] = y.astype(o_ref.dtype)


def _gla_consts(tt):
    ri = np.arange(tt)[:, None]
    ci = np.arange(tt)[None, :]
    same = (ri // GLA_CHUNK) == (ci // GLA_CHUNK)
    lt = (same & (ci <= ri)).astype(np.float32)
    hw = GLA_HEADS * GLA_DK
    hi = np.arange(hw)
    bd = ((hi[:, None] // GLA_DK) == (hi[None, :] // GLA_DK)).astype(np.float32)
    return (jnp.asarray(lt, BF16), jnp.asarray(lt, F32), jnp.asarray(same.astype(np.float32), BF16),
            jnp.asarray(bd, F32), jnp.asarray(bd, BF16))


def gla_branch(proj3, a_up_p, a_b, g_norm, *, tt):
    b, s, _ = proj3.shape
    hw = GLA_HEADS * GLA_DK
    ltb, ltf, bob, bd, bdb = _gla_consts(tt)
    const = lambda shape: pl.BlockSpec(shape, lambda bi, i: (0,) * len(shape))
    return pl.pallas_call(
        functools.partial(_gla_kernel, tt=tt),
        grid=(b, s // tt),
        in_specs=[
            pl.BlockSpec((1, tt, 4 * hw), lambda bi, i: (bi, i, P_GLA // (4 * hw))),
            pl.BlockSpec((1, tt, LANES), lambda bi, i: (bi, i, P_GA // LANES)),
            const((LANES, hw)), const((1, hw)), const((1, hw)),
            const((tt, tt)), const((tt, tt)), const((tt, tt)), const((hw, hw)), const((hw, hw)),
        ],
        out_specs=pl.BlockSpec((1, tt, hw), lambda bi, i: (bi, i, 0)),
        out_shape=jax.ShapeDtypeStruct((b, s, hw), BF16),
        scratch_shapes=[pltpu.VMEM((hw, hw), F32)],
        compiler_params=_cparams(("parallel", "arbitrary")),
        name="gla_branch",
    )(proj3, proj3, a_up_p, a_b, g_norm, ltb, ltf, bob, bd, bdb)


def _s5_param_kernel(lre_ref, lim_ref, ldt_ref, bre_ref, bim_ref, bbre_ref, bbim_ref, tab_ref):
    lre = lre_ref[...]
    lim = lim_ref[...]
    dt = jnp.exp(ldt_ref[...])
    mag = jnp.exp(lre * dt)
    lbr = mag * jnp.cos(lim * dt)
    lbi = mag * jnp.sin(lim * dt)
    den = lre * lre + lim * lim
    fre = ((lbr - 1.0) * lre + lbi * lim) / den
    fim = (lbi * lre - (lbr - 1.0) * lim) / den
    bre = bre_ref[...]
    bim = bim_ref[...]
    bbre_ref[...] = fre * bre - fim * bim
    bbim_ref[...] = fre * bim + fim * bre

    pw = [None, (lbr, lbi)]
    for _ in range(2, SUBLANES + 1):
        pr, pi = pw[-1]
        pw.append((pr * lbr - pi * lbi, pr * lbi + pi * lbr))
    n = lre.shape[-1]
    row = lax.broadcasted_iota(I32, (SUBLANES, n), 0)
    zero = jnp.zeros((SUBLANES, n), F32)
    for idx, sft in enumerate((1, 2, 4)):
        tab_ref[2 * idx] = jnp.where(row >= sft, jnp.broadcast_to(pw[sft][0], (SUBLANES, n)), zero)
        tab_ref[2 * idx + 1] = jnp.where(row >= sft, jnp.broadcast_to(pw[sft][1], (SUBLANES, n)), zero)
    cr, ci = zero, zero
    for rr in range(SUBLANES):
        cr = jnp.where(row == rr, jnp.broadcast_to(pw[rr + 1][0], (SUBLANES, n)), cr)
        ci = jnp.where(row == rr, jnp.broadcast_to(pw[rr + 1][1], (SUBLANES, n)), ci)
    tab_ref[6] = cr
    tab_ref[7] = ci


def s5_params(lam_re, lam_im, log_dt, b_re, b_im):
    g, p = lam_re.shape
    n = g * p
    row = lambda a: a.reshape(1, n)
    ldt = jnp.broadcast_to(log_dt[:, None], (g, p))
    bt = lambda a: jnp.transpose(a, (2, 0, 1)).reshape(S5_GROUP, n)
    full = lambda shape: pl.BlockSpec(shape, lambda: (0,) * len(shape))
    return pl.pallas_call(
        _s5_param_kernel,
        in_specs=[full((1, n))] * 3 + [full((S5_GROUP, n))] * 2,
        out_specs=[full((S5_GROUP, n)), full((S5_GROUP, n)), full((8, SUBLANES, n))],
        out_shape=[jax.ShapeDtypeStruct((S5_GROUP, n), F32), jax.ShapeDtypeStruct((S5_GROUP, n), F32),
                   jax.ShapeDtypeStruct((8, SUBLANES, n), F32)],
        name="s5_params",
    )(row(lam_re), row(lam_im), row(ldt), bt(b_re), bt(b_im))


def _s5_kernel(u_ref, bm_ref, tab_ref, cm_ref, d_ref, wg_ref, bg_ref, o_ref, xs_ref, car_ref, *, tt, n):
    @pl.when(pl.program_id(1) == 0)
    def _():
        car_ref[...] = jnp.zeros_like(car_ref)

    ub = u_ref[0]
    xs_ref[...] = _dot(ub, bm_ref[...])

    def group(gi, carry):
        r0 = pl.multiple_of(gi * SUBLANES, SUBLANES)
        for j in range(n // LANES):
            cre = slice(j * LANES, (j + 1) * LANES)
            cim = slice(n + j * LANES, n + (j + 1) * LANES)
            re = xs_ref[pl.ds(r0, SUBLANES), cre]
            im = xs_ref[pl.ds(r0, SUBLANES), cim]
            for idx, sft in enumerate((1, 2, 4)):
                ar = tab_ref[2 * idx, :, cre]
                ai = tab_ref[2 * idx + 1, :, cre]
                sr = pltpu.roll(re, sft, 0)
                si = pltpu.roll(im, sft, 0)
                re, im = re + ar * sr - ai * si, im + ar * si + ai * sr
            pr = tab_ref[6, :, cre]
            pi = tab_ref[7, :, cre]
            cr = car_ref[0, :, cre]
            ci = car_ref[1, :, cre]
            re, im = re + pr * cr - pi * ci, im + pr * ci + pi * cr
            xs_ref[pl.ds(r0, SUBLANES), cre] = re
            xs_ref[pl.ds(r0, SUBLANES), cim] = im
            car_ref[0, :, cre] = jnp.broadcast_to(re[SUBLANES - 1:SUBLANES], (SUBLANES, LANES))
            car_ref[1, :, cre] = jnp.broadcast_to(im[SUBLANES - 1:SUBLANES], (SUBLANES, LANES))
        return carry

    lax.fori_loop(0, tt // SUBLANES, group, 0)

    y = _dot(xs_ref[...].astype(BF16), cm_ref[...]) + d_ref[...] * ub.astype(F32)
    y = 0.5 * y * (1.0 + jnp.tanh(math.sqrt(2.0 / math.pi) * (y + 0.044715 * (y * y * y))))
    z = _dot(y.astype(BF16), wg_ref[...]) + bg_ref[...]
    o_ref[0] = (y * _sigmoid(z)).astype(o_ref.dtype)


def s5_branch(proj3, bmat, tab, cmat, d_skip, w_glu, b_glu, *, tt):
    b, s, _ = proj3.shape
    w_ = BRANCH_W
    n = S5_GROUPS * S5_STATE
    const = lambda shape: pl.BlockSpec(shape, lambda bi, i: (0,) * len(shape))
    return pl.pallas_call(
        functools.partial(_s5_kernel, tt=tt, n=n),
        grid=(b, s // tt),
        in_specs=[
            pl.BlockSpec((1, tt, w_), lambda bi, i: (bi, i, P_S5U // w_)),
            const((w_, 2 * n)), const((8, SUBLANES, n)), const((2 * n, w_)),
            const((1, w_)), const((w_, w_)), const((1, w_)),
        ],
        out_specs=pl.BlockSpec((1, tt, w_), lambda bi, i: (bi, i, 0)),
        out_shape=jax.ShapeDtypeStruct((b, s, w_), BF16),
        scratch_shapes=[pltpu.VMEM((tt, 2 * n), F32), pltpu.VMEM((2, SUBLANES, n), F32)],
        compiler_params=_cparams(("parallel", "arbitrary")),
        name="s5_branch",
    )(proj3, bmat, tab, cmat, d_skip, w_glu, b_glu)


def _rope_freq_rows():
    rows = np.zeros((8, LANES), np.float32)
    for pat, dh in enumerate((DSA_HEAD_DIM, IDX_DIM)):
        rd = dh // ROPE_FRAC
        half = rd // 2
        inv = (np.float32(ROPE_THETA) ** (-np.arange(half, dtype=np.float32) * np.float32(2.0 / rd))).astype(np.float32)
        for l in range(LANES):
            i = l % dh
            if i < half:
                rows[3 * pat, l] = inv[i]
                rows[3 * pat + 1, l] = -1.0
            elif i < rd:
                rows[3 * pat, l] = inv[i - half]
                rows[3 * pat + 2, l] = 1.0
    return rows


def _rope_tab_kernel(pos_ref, fr_ref, o_ref):
    pos = pos_ref[0]
    fr = fr_ref[...]
    for pat in range(2):
        ang = pos * fr[3 * pat:3 * pat + 1]
        c = jnp.cos(ang)
        s = jnp.sin(ang)
        o_ref[0, 3 * pat] = c
        o_ref[0, 3 * pat + 1] = s * fr[3 * pat + 1:3 * pat + 2]
        o_ref[0, 3 * pat + 2] = s * fr[3 * pat + 2:3 * pat + 3]


def rope_tables(positions, *, ts):
    b, s = positions.shape
    pos = positions.astype(F32).reshape(b, s, 1)
    fr = jnp.asarray(_rope_freq_rows())
    return pl.pallas_call(
        _rope_tab_kernel,
        grid=(b, s // ts),
        in_specs=[pl.BlockSpec((1, ts, 1), lambda bi, i: (bi, i, 0)),
                  pl.BlockSpec((8, LANES), lambda bi, i: (0, 0))],
        out_specs=pl.BlockSpec((1, 6, ts, LANES), lambda bi, i: (bi, 0, i, 0)),
        out_shape=jax.ShapeDtypeStruct((b, 6, s, LANES), F32),
        compiler_params=_cparams(("parallel", "arbitrary")),
        name="rope_tables",
    )(pos, fr)


def _rope_partner_matrix(n, dh, active):
    rd = dh // ROPE_FRAC
    half = rd // 2
    r = np.zeros((n, n), np.float32)
    for l in range(active):
        i = l % dh
        if i < half:
            r[l + half, l] = -1.0
        elif i < rd:
            r[l - half, l] = 1.0
    return jnp.asarray(r, BF16)


def _rope(tb, c, s, p_ref):
    return tb.astype(F32) * c + s * _dot(tb, p_ref[...])


def _dsa_prep_kernel(dq_ref, iq_ref, kv_ref, ikw_ref, tab_ref, pq_ref, pi_ref, pk_ref, pik_ref,
                     qs_ref, qis_ref, kr_ref, vt_ref, kir_ref, wt_ref):
    two = lambda a: jnp.concatenate([a, a], axis=1)
    c1, s1 = tab_ref[0, 0], tab_ref[0, 2] - tab_ref[0, 1]
    c2, s2 = tab_ref[0, 3], tab_ref[0, 5] - tab_ref[0, 4]
    lane = lax.broadcasted_iota(I32, (1, LANES), 1)

    q = _rope(dq_ref[0], two(c1), two(s1), pq_ref) * (DSA_HEAD_DIM ** -0.5 * math.log2(math.e))
    for h in range(DSA_HEADS):
        qs_ref[0, h] = q[:, h * DSA_HEAD_DIM:(h + 1) * DSA_HEAD_DIM].astype(BF16)
    qi = _rope(iq_ref[0], two(c2), two(s2), pi_ref)
    for h in range(IDX_HEADS):
        qis_ref[0, h] = qi[:, h * IDX_DIM:(h + 1) * IDX_DIM].astype(BF16)

    kvr = _rope(kv_ref[0], jnp.where(lane < DSA_HEAD_DIM, c1, 1.0), s1, pk_ref)
    kr_ref[0] = kvr[:, 0:DSA_HEAD_DIM].astype(BF16)
    ones = jnp.ones((VT_ROWS - DSA_HEAD_DIM, kvr.shape[0]), BF16)
    vt_ref[0, 0] = jnp.concatenate([kvr.T[DSA_HEAD_DIM:2 * DSA_HEAD_DIM].astype(BF16), ones], axis=0)

    ikr = _rope(ikw_ref[0], jnp.where(lane < IDX_DIM, c2, 1.0), s2, pik_ref)
    kir_ref[0] = ikr[:, 0:IDX_DIM].astype(BF16)
    wt_ref[0] = ikr.T[IDX_DIM:IDX_DIM + IDX_HEADS] * ((IDX_HEADS ** -0.5) * (IDX_DIM ** -0.5))


def dsa_prep(proj3, tabs, *, ts):
    b, s, _ = proj3.shape
    qw = DSA_HEADS * DSA_HEAD_DIM
    iw = IDX_HEADS * IDX_DIM
    return pl.pallas_call(
        _dsa_prep_kernel,
        grid=(b, s // ts),
        in_specs=[
            pl.BlockSpec((1, ts, qw), lambda bi, i: (bi, i, P_DQ // qw)),
            pl.BlockSpec((1, ts, iw), lambda bi, i: (bi, i, P_IQ // iw)),
            pl.BlockSpec((1, ts, LANES), lambda bi, i: (bi, i, P_KV // LANES)),
            pl.BlockSpec((1, ts, LANES), lambda bi, i: (bi, i, P_IKW // LANES)),
            pl.BlockSpec((1, 6, ts, LANES), lambda bi, i: (bi, 0, i, 0)),
            pl.BlockSpec((qw, qw), lambda bi, i: (0, 0)),
            pl.BlockSpec((iw, iw), lambda bi, i: (0, 0)),
            pl.BlockSpec((LANES, LANES), lambda bi, i: (0, 0)),
            pl.BlockSpec((LANES, LANES), lambda bi, i: (0, 0)),
        ],
        out_specs=[
            pl.BlockSpec((1, DSA_HEADS, ts, DSA_HEAD_DIM), lambda bi, i: (bi, 0, i, 0)),
            pl.BlockSpec((1, IDX_HEADS, ts, IDX_DIM), lambda bi, i: (bi, 0, i, 0)),
            pl.BlockSpec((1, ts, DSA_HEAD_DIM), lambda bi, i: (bi, i, 0)),
            pl.BlockSpec((1, 1, VT_ROWS, ts), lambda bi, i: (bi, i, 0, 0)),
            pl.BlockSpec((1, ts, IDX_DIM), lambda bi, i: (bi, i, 0)),
            pl.BlockSpec((1, IDX_HEADS, ts), lambda bi, i: (bi, 0, i)),
        ],
        out_shape=[
            jax.ShapeDtypeStruct((b, DSA_HEADS, s, DSA_HEAD_DIM), BF16),
            jax.ShapeDtypeStruct((b, IDX_HEADS, s, IDX_DIM), BF16),
            jax.ShapeDtypeStruct((b, s, DSA_HEAD_DIM), BF16),
            jax.ShapeDtypeStruct((b, s // ts, VT_ROWS, ts), BF16),
            jax.ShapeDtypeStruct((b, s, IDX_DIM), BF16),
            jax.ShapeDtypeStruct((b, IDX_HEADS, s), F32),
        ],
        compiler_params=_cparams(("parallel", "arbitrary")),
        name="dsa_prep",
    )(proj3, proj3, proj3, proj3, tabs,
      _rope_partner_matrix(qw, DSA_HEAD_DIM, qw), _rope_partner_matrix(iw, IDX_DIM, iw),
      _rope_partner_matrix(LANES, DSA_HEAD_DIM, DSA_HEAD_DIM), _rope_partner_matrix(LANES, IDX_DIM, IDX_DIM))


def _dsa_kernel(q_ref, qi_ref, w_ref, k_ref, vt_ref, ki_ref, lts_ref, o_ref, sc_ref, lg_ref, *, qb, kt, topk):
    i = pl.program_id(1)
    q0 = i * qb
    nkt = (q0 + qb + kt - 1) // kt
    kf = float(topk)
    sg = 8 * SUBLANES
    wrow = w_ref[0]
    tq = q0 + lax.broadcasted_iota(I32, (1, qb), 1)

    def to_key(x):
        bits = pltpu.bitcast(x, I32)
        return bits ^ ((bits >> 31) & 0x7FFFFFFF)

    def scores(t, gmax):
        k0 = pl.multiple_of(t * kt, kt)
        sb = LANES
        gm = [gmax[j * sb:(j + 1) * sb] for j in range(topk // sb)]
        for r in range(kt // sb):
            kit = ki_ref[0, pl.ds(k0 + r * sb, sb), :]
            acc = jnp.zeros((sb, qb), F32)
            for h in range(IDX_HEADS):
                acc = acc + jnp.maximum(_dot_nt(kit, qi_ref[0, h]), 0.0) * wrow[h:h + 1]
            acc = jnp.where(acc == 0.0, 0.0, acc)
            causal = (k0 + r * sb + lax.broadcasted_iota(I32, (sb, qb), 0)) <= tq
            sc_ref[t, r * sb:(r + 1) * sb, :] = jnp.where(causal, to_key(acc), INT_MIN)
            j = ((r * sb) % topk) // sb
            gm[j] = jnp.maximum(gm[j], jnp.where(causal, acc, -jnp.inf))
        return jnp.concatenate(gm, axis=0)

    gmax = lax.fori_loop(0, nkt, scores, jnp.full((topk, qb), -jnp.inf, F32))

    def count(pred):
        def body(t, c):
            m = jnp.where(pred(sc_ref[t]), 1.0, 0.0)
            return c + jnp.sum(m.reshape(kt // sg, sg, qb), axis=0)
        c = lax.fori_loop(0, nkt, body, jnp.zeros((sg, qb), F32))
        return jnp.sum(c, axis=0, keepdims=True)

    need = (tq + 1) > topk
    lo0 = to_key(jnp.min(gmax, axis=0, keepdims=True))
    hi0 = to_key(jnp.max(gmax, axis=0, keepdims=True))
    done0 = jnp.where(jnp.logical_and(need, lo0 < hi0), 0.0, 1.0)

    def bs_cond(c):
        it, _, _, _, done, _ = c
        return jnp.logical_and(it < 34, jnp.sum(1.0 - done) > 0.0)

    def bs_step(lo, hi, thr, done, below, probe=None):
        mid = (lo >> 1) + (hi >> 1) + (((lo & 1) + (hi & 1) + 1) >> 1)
        if probe is not None:
            mid = jnp.where(jnp.logical_and(lo < probe, probe <= hi), probe, mid)
        cnt = count(lambda kk: kk >= mid)
        ge = cnt >= kf
        hit = cnt == kf
        near = cnt == kf - 1.0
        lo = jnp.where(ge, mid, lo)
        hi = jnp.where(ge, hi, mid - 1)
        fin = jnp.where(hit, 1.0, jnp.where(near, 1.0, jnp.where(lo == hi, 1.0, 0.0)))
        val = jnp.where(hit, mid, jnp.where(near, mid - 1, lo))
        live = done == 0.0
        thr = jnp.where(live, val, thr)
        below = jnp.where(live, jnp.where(near, 1.0, 0.0), below)
        return lo, hi, thr, jnp.maximum(done, fin), below

    def bs_body(c):
        it = c[0]
        return (it + 2,) + bs_step(*bs_step(*c[1:]))

    st = bs_step(*bs_step(lo0, hi0, lo0, done0, jnp.zeros((1, qb), F32), probe=1), probe=0)
    _, _, _, thr, _, below = lax.while_loop(bs_cond, bs_body, (jnp.int32(2),) + st)

    def below_max(t, c):
        kk = sc_ref[t]
        cand = jnp.where(kk <= thr, kk, NEG_INF_KEY)
        return jnp.maximum(c, jnp.max(cand.reshape(kt // sg, sg, qb), axis=0))

    bm = lax.fori_loop(0, nkt, below_max, jnp.full((sg, qb), NEG_INF_KEY, I32))
    bmax = to_key(jnp.max(pltpu.bitcast(bm ^ ((bm >> 31) & 0x7FFFFFFF), F32), axis=0, keepdims=True))
    thr = jnp.where(below > 0.0, bmax, thr)
    thr = jnp.where(need, thr, INT_MIN + 1)

    cge = count(lambda kk: kk >= thr)
    tie = jnp.logical_and(need, cge > kf)

    @pl.when(jnp.sum(jnp.where(tie, 1.0, 0.0)) > 0.0)
    def _():
        cgt = count(lambda kk: kk > thr)
        room = kf - cgt

        def body(t, seen):
            kk = sc_ref[t]
            eq = jnp.logical_and(kk == thr, tie)
            eqf = jnp.where(eq, 1.0, 0.0)
            before = _dot(lts_ref[...], eqf.astype(BF16)) + seen
            sc_ref[t] = jnp.where(jnp.logical_and(eq, before >= room), INT_MIN, kk)
            return seen + jnp.sum(eqf, axis=0, keepdims=True)

        lax.fori_loop(0, nkt, body, jnp.zeros((1, qb), F32))

    nh = DSA_HEADS
    qa = lg_ref.shape[3]
    dh = DSA_HEAD_DIM
    for part in range(qb // qa):
        cols = slice(part * qa, (part + 1) * qa)
        thr_p = thr[:, cols]

        def logits(t, c, cols=cols, thr_p=thr_p):
            k0 = pl.multiple_of(t * kt, kt)
            kt_ = k_ref[0, pl.ds(k0, kt), :]
            bias = jnp.where(sc_ref[t, :, cols] >= thr_p, 0.0, -jnp.inf)
            out = []
            for h in range(nh):
                lg = _dot_nt(kt_, q_ref[0, h, cols, :]) + bias
                lg_ref[t, h] = lg
                out.append(jnp.maximum(c[h], jnp.max(lg.reshape(kt // sg, sg, qa), axis=0)))
            return tuple(out)

        mparts = lax.fori_loop(0, nkt, logits, (jnp.full((sg, qa), -jnp.inf, F32),) * nh)
        ms = []
        for h in range(nh):
            m = jnp.max(mparts[h], axis=0, keepdims=True)
            ms.append(jnp.where(m == -jnp.inf, 0.0, m))

        def attend(t, c, ms=ms):
            vt = vt_ref[0, t]
            out = []
            for h in range(nh):
                p = jnp.exp2((lg_ref[t, h] - ms[h]).astype(BF16))
                out.append(c[h] + _dot(vt, p))
            return tuple(out)

        res = lax.fori_loop(0, nkt, attend, (jnp.zeros((vt_ref.shape[2], qa), F32),) * nh)
        for h in range(nh):
            ot = res[h][0:dh] / res[h][dh:dh + 1]
            o_ref[0, cols, h * dh:(h + 1) * dh] = ot.T.astype(o_ref.dtype)


def dsa_attend(qs, qis, wt, kr, vt4, kir, *, qb, kt, topk):
    b, nh, s, dh = qs.shape
    nkt = s // kt
    assert topk % LANES == 0 and kt % topk == 0 and s % kt == 0 and s % qb == 0
    ri = np.arange(kt)
    lts = jnp.asarray((ri[None, :] < ri[:, None]).astype(np.float32), BF16)
    return pl.pallas_call(
        functools.partial(_dsa_kernel, qb=qb, kt=kt, topk=topk),
        grid=(b, s // qb),
        in_specs=[
            pl.BlockSpec((1, nh, qb, dh), lambda bi, i: (bi, 0, i, 0)),
            pl.BlockSpec((1, IDX_HEADS, qb, IDX_DIM), lambda bi, i: (bi, 0, i, 0)),
            pl.BlockSpec((1, IDX_HEADS, qb), lambda bi, i: (bi, 0, i)),
            pl.BlockSpec((1, s, dh), lambda bi, i: (bi, 0, 0)),
            pl.BlockSpec((1, nkt, VT_ROWS, kt), lambda bi, i: (bi, 0, 0, 0)),
            pl.BlockSpec((1, s, IDX_DIM), lambda bi, i: (bi, 0, 0)),
            pl.BlockSpec((kt, kt), lambda bi, i: (0, 0)),
        ],
        out_specs=pl.BlockSpec((1, qb, nh * dh), lambda bi, i: (bi, i, 0)),
        out_shape=jax.ShapeDtypeStruct((b, s, nh * dh), BF16),
        scratch_shapes=[pltpu.VMEM((nkt, kt, qb), I32), pltpu.VMEM((nkt, nh, kt, min(qb, 2 * Q_BLOCK)), F32)],
        compiler_params=_cparams(("parallel", "arbitrary")),
        name="dsa_attend",
    )(qs, qis, wt, kr, vt4, kir, lts)


def _merge_kernel(x_ref, ya_ref, yb_ref, yc_ref, yd_ref, g0_ref, g1_ref, g2_ref, g3_ref, wb_ref, wo_ref, o_ref):
    ys = (ya_ref, yb_ref, yc_ref, yd_ref)
    gs = (g0_ref, g1_ref, g2_ref, g3_ref)
    merged = jnp.zeros(o_ref.shape, F32)
    for i in range(N_BRANCH):
        merged = merged + _sigmoid(gs[i][...].astype(F32)) * _dot(ys[i][...], wb_ref[i])
    o_ref[...] = x_ref[...] + _dot(merged.astype(BF16), wo_ref[...])


def merge_out(x2, ya, yb, yc, yd, proj, w_branch, w_out, *, tm):
    t, d = x2.shape
    w_ = BRANCH_W
    yspec = pl.BlockSpec((tm, w_), lambda i: (i, 0))
    gspec = lambda n: pl.BlockSpec((tm, d), lambda i: (i, P_GATES // d + n))
    return pl.pallas_call(
        _merge_kernel,
        grid=(t // tm,),
        in_specs=[pl.BlockSpec((tm, d), lambda i: (i, 0)), yspec, yspec, yspec, yspec,
                  gspec(0), gspec(1), gspec(2), gspec(3),
                  pl.BlockSpec((N_BRANCH, w_, d), lambda i: (0, 0, 0)),
                  pl.BlockSpec((d, d), lambda i: (0, 0))],
        out_specs=pl.BlockSpec((tm, d), lambda i: (i, 0)),
        out_shape=jax.ShapeDtypeStruct((t, d), F32),
        compiler_params=_cparams(("parallel",)),
        name="merge_out",
    )(x2, ya, yb, yc, yd, proj, proj, proj, proj, w_branch, w_out)


def _cross_kernel(x_ref, g_ref, wq_ref, kv_ref, wo_ref, o_ref):
    x = x_ref[0]
    hn = _rms(x, g_ref[...]).astype(BF16)
    q = _dot(hn, wq_ref[...]) * (X_HEAD_DIM ** -0.5)
    kv = kv_ref[0]
    hd = X_HEADS * X_HEAD_DIM
    outs = []
    for h in range(X_HEADS):
        cs = slice(h * X_HEAD_DIM, (h + 1) * X_HEAD_DIM)
        lg = _dot_nt(q[:, cs].astype(BF16), kv[:, cs])
        p = jnp.exp(lg - jnp.max(lg, axis=-1, keepdims=True))
        p = p / jnp.sum(p, axis=-1, keepdims=True)
        outs.append(_dot(p.astype(BF16), kv[:, hd + h * X_HEAD_DIM:hd + (h + 1) * X_HEAD_DIM]))
    o = jnp.concatenate(outs, axis=1).astype(BF16)
    o_ref[0] = x + _dot(o, wo_ref[...])


def cross_attn(x3, g, w_q, kv, w_o, *, tm):
    b, s, d = x3.shape
    m = kv.shape[1]
    hd = X_HEADS * X_HEAD_DIM
    return pl.pallas_call(
        _cross_kernel,
        grid=(b, s // tm),
        in_specs=[pl.BlockSpec((1, tm, d), lambda bi, i: (bi, i, 0)),
                  pl.BlockSpec((1, d), lambda bi, i: (0, 0)),
                  pl.BlockSpec((d, hd), lambda bi, i: (0, 0)),
                  pl.BlockSpec((1, m, 2 * hd), lambda bi, i: (bi, 0, 0)),
                  pl.BlockSpec((hd, d), lambda bi, i: (0, 0))],
        out_specs=pl.BlockSpec((1, tm, d), lambda bi, i: (bi, i, 0)),
        out_shape=jax.ShapeDtypeStruct((b, s, d), F32),
        compiler_params=_cparams(("parallel", "arbitrary")),
        name="cross_attn",
    )(x3, g, w_q, kv, w_o)


def _router_kernel(x_ref, g_ref, wr_ref, br_ref, hn_ref, cmb_ref):
    hn = _rms(x_ref[...], g_ref[...])
    hn_ref[...] = hn.astype(BF16)
    lg = jnp.dot(hn, wr_ref[...], precision=lax.Precision.HIGHEST, preferred_element_type=F32) + br_ref[...]
    lane = lax.broadcasted_iota(I32, lg.shape, 1)
    lanef = lane.astype(F32)
    ninf = -jnp.inf
    big = float(LANES)
    first = lambda mask: jnp.min(jnp.where(mask, lanef, big), axis=-1, keepdims=True)

    isg = jnp.logical_and(lane >= N_EXPERTS, lane < N_EXPERTS + N_GROUPS)
    gl = jnp.where(isg, lg, ninf)
    gmax = jnp.max(gl, axis=-1, keepdims=True)
    gsel = first(gl == gmax) - float(N_EXPERTS)
    pg = 1.0 / jnp.sum(jnp.exp(gl - gmax), axis=-1, keepdims=True)

    ise = jnp.floor(lanef * (1.0 / EXPERTS_PER_GROUP)) == gsel
    el = jnp.where(ise, lg, ninf)
    e1 = jnp.max(el, axis=-1, keepdims=True)
    i1 = first(el == e1)
    el2 = jnp.where(lanef == i1, ninf, el)
    e2 = jnp.max(el2, axis=-1, keepdims=True)
    i2 = first(el2 == e2)
    d = jnp.exp(e2 - e1)
    w1 = 1.0 / (1.0 + d)
    w2 = d / (1.0 + d)
    cmb = jnp.where(lanef == i1, pg * w1, jnp.where(lanef == i2, pg * w2, 0.0))
    a = jnp.minimum(i1, i2) - EXPERTS_PER_GROUP * gsel
    b = jnp.maximum(i1, i2) - EXPERTS_PER_GROUP * gsel
    rank = jnp.where(a == 0.0, b - 1.0, jnp.where(a == 1.0, jnp.where(b == 3.0, 3.0, 4.0), 5.0))
    cmb_ref[...] = jnp.where(lane == N_EXPERTS, gsel * MOE_PAIRS + rank, cmb)


def moe_router(x2, g, wr, br, *, tm):
    t, d = x2.shape
    return pl.pallas_call(
        _router_kernel,
        grid=(t // tm,),
        in_specs=[pl.BlockSpec((tm, d), lambda i: (i, 0)), pl.BlockSpec((1, d), lambda i: (0, 0)),
                  pl.BlockSpec((d, LANES), lambda i: (0, 0)), pl.BlockSpec((1, LANES), lambda i: (0, 0))],
        out_specs=[pl.BlockSpec((tm, d), lambda i: (i, 0)), pl.BlockSpec((tm, LANES), lambda i: (i, 0))],
        out_shape=[jax.ShapeDtypeStruct((t, d), BF16), jax.ShapeDtypeStruct((t, LANES), F32)],
        compiler_params=_cparams(("parallel",)),
        name="moe_router",
    )(x2, g, wr, br)


MOE_PAD = 2 * SUBLANES
MOE_SB = 256
MOE_PAIRS = 6
MOE_CLASSES = N_GROUPS * MOE_PAIRS
MOE_FIRST = (0, 0, 1, 2)
MOE_LAST = (2, 4, 5, 5)
MOE_CHUNK = (160, 256, 256, 208)
MOE_SEG = 64


def _moe_rmax(tm):
    return -(-(tm + N_GROUPS * MOE_PAD + max(MOE_CHUNK)) // LANES) * LANES


def _moe_class_matrices():
    c = np.arange(LANES)
    valid = c < MOE_CLASSES
    grp = c // MOE_PAIRS
    both = valid[:, None] & valid[None, :]
    before = both & (c[:, None] < c[None, :])
    same = both & (grp[:, None] == grp[None, :])
    lead = both & (c[:, None] % MOE_PAIRS == 0) & (grp[:, None] < grp[None, :])
    return jnp.asarray(np.stack([before, same, lead]).astype(np.float32))


def _moe_sort_kernel(hn_ref, cmb_ref, lts_ref, cm_ref, hs_ref, cs_ref, meta_ref, seg_ref, *, tm, rmax):
    cmb = cmb_ref[...]
    lanef = lax.broadcasted_iota(I32, (tm, LANES), 1).astype(F32)
    oh = jnp.where(lanef == cmb[:, N_EXPERTS:N_EXPERTS + 1], 1.0, 0.0)
    seen = jnp.zeros((1, LANES), F32)
    pres = []
    for sblk in range(tm // MOE_SB):
        ohs = oh[sblk * MOE_SB:(sblk + 1) * MOE_SB]
        pres.append(_dot(lts_ref[...], ohs.astype(BF16)) + seen)
        seen = seen + jnp.sum(ohs, axis=0, keepdims=True)
    pre = jnp.concatenate(pres, axis=0)
    cnt = jnp.broadcast_to(seen, (SUBLANES, LANES))
    exact = functools.partial(jnp.dot, precision=lax.Precision.HIGHEST, preferred_element_type=F32)
    tot = exact(cnt, cm_ref[1])
    padamt = jnp.ceil(tot * (1.0 / MOE_PAD)) * MOE_PAD - tot
    off = exact(cnt, cm_ref[0]) + exact(padamt, cm_ref[2])
    dest = jnp.sum(oh * (off[0:1] + pre), axis=-1, keepdims=True)
    destb = jnp.broadcast_to(dest, (tm, LANES))
    meta_ref[...] = destb
    row8 = lax.broadcasted_iota(I32, (SUBLANES, LANES), 0)
    seg_ref[0] = jnp.where(row8 == 0, off, jnp.where(row8 == 1, cnt, 0.0)).astype(I32)

    dest_row = destb.T[0:1]
    hn = hn_ref[...]
    chi = cmb.astype(BF16)
    clo = (cmb - chi.astype(F32)).astype(BF16)

    def blk(rb, carry):
        r0 = pl.multiple_of(rb * LANES, LANES)
        rows = (r0 + lax.broadcasted_iota(I32, (LANES, tm), 0)).astype(F32)
        p = jnp.where(rows == dest_row, 1.0, 0.0).astype(BF16)
        hs_ref[0, pl.ds(r0, LANES), :] = _dot(p, hn).astype(BF16)
        cs_ref[0, pl.ds(r0, LANES), :] = _dot(p, chi) + _dot(p, clo)
        return carry

    lax.fori_loop(0, rmax // LANES, blk, 0)


def moe_sort(hn, cmb, *, tm):
    t, d = hn.shape
    nt = t // tm
    rmax = _moe_rmax(tm)
    ri = np.arange(MOE_SB)
    lts = jnp.asarray((ri[None, :] < ri[:, None]).astype(np.float32), BF16)
    return pl.pallas_call(
        functools.partial(_moe_sort_kernel, tm=tm, rmax=rmax),
        grid=(nt,),
        in_specs=[pl.BlockSpec((tm, d), lambda i: (i, 0)), pl.BlockSpec((tm, LANES), lambda i: (i, 0)),
                  pl.BlockSpec((MOE_SB, MOE_SB), lambda i: (0, 0)),
                  pl.BlockSpec((3, LANES, LANES), lambda i: (0, 0, 0))],
        out_specs=[pl.BlockSpec((1, rmax, d), lambda i: (i, 0, 0)),
                   pl.BlockSpec((1, rmax, LANES), lambda i: (i, 0, 0)),
                   pl.BlockSpec((tm, LANES), lambda i: (i, 0)),
                   pl.BlockSpec((1, SUBLANES, LANES), lambda i: (i, 0, 0))],
        out_shape=[jax.ShapeDtypeStruct((nt, rmax, d), BF16), jax.ShapeDtypeStruct((nt, rmax, LANES), F32),
                   jax.ShapeDtypeStruct((t, LANES), F32), jax.ShapeDtypeStruct((nt, SUBLANES, LANES), I32)],
        compiler_params=_cparams(("parallel",)),
        name="moe_sort",
    )(hn, cmb, lts, _moe_class_matrices())


def _moe_expert_kernel(seg_ref, hs_ref, cs_ref, wg_ref, wu_ref, wd_ref, y_ref, acc_ref, *, ns):
    i = pl.program_id(0)
    e = pl.program_id(1)

    @pl.when(e == 0)
    def _():
        acc_ref[...] = jnp.zeros_like(acc_ref)

    g = e // EXPERTS_PER_GROUP

    def run(local, ch):
        for s in range(ns):
            base = (i * ns + s) * MOE_SEG + g * MOE_PAIRS
            start = seg_ref[base + MOE_FIRST[local]]
            end = seg_ref[base + MOE_LAST[local]] + seg_ref[base + MOE_SEG // 2 + MOE_LAST[local]]
            start = (start // MOE_PAD) * MOE_PAD

            def chunk(j, carry, s=s, start=start):
                r0 = pl.multiple_of(start + j * ch, MOE_PAD)
                hsl = hs_ref[s, pl.ds(r0, ch), :]
                gg = _dot(hsl, wg_ref[0])
                uu = _dot(hsl, wu_ref[0])
                c = cs_ref[s, pl.ds(r0, ch), :]
                lane = lax.broadcasted_iota(I32, c.shape, 1)
                cc = jnp.sum(jnp.where(lane == e, c, 0.0), axis=-1, keepdims=True)
                he = (gg * _sigmoid(gg)) * uu * cc
                acc_ref[s, pl.ds(r0, ch), :] += _dot(he.astype(BF16), wd_ref[0])
                return carry

            lax.fori_loop(0, (end - start + ch - 1) // ch, chunk, 0)

    for local in range(EXPERTS_PER_GROUP):
        pl.when(e % EXPERTS_PER_GROUP == local)(functools.partial(run, local, MOE_CHUNK[local]))

    @pl.when(e == pl.num_programs(1) - 1)
    def _():
        y_ref[...] = acc_ref[...].astype(BF16)


def moe_experts(seg, hs, cs, wg, wu, wd, *, ns):
    nt, rmax, d = hs.shape
    ne, _, f = wg.shape
    return pl.pallas_call(
        functools.partial(_moe_expert_kernel, ns=ns),
        grid_spec=pltpu.PrefetchScalarGridSpec(
            num_scalar_prefetch=1,
            grid=(nt // ns, ne),
            in_specs=[pl.BlockSpec((ns, rmax, d), lambda i, e, sref: (i, 0, 0)),
                      pl.BlockSpec((ns, rmax, LANES), lambda i, e, sref: (i, 0, 0)),
                      pl.BlockSpec((1, d, f), lambda i, e, sref: (e, 0, 0)),
                      pl.BlockSpec((1, d, f), lambda i, e, sref: (e, 0, 0)),
                      pl.BlockSpec((1, f, d), lambda i, e, sref: (e, 0, 0))],
            out_specs=pl.BlockSpec((ns, rmax, d), lambda i, e, sref: (i, 0, 0)),
            scratch_shapes=[pltpu.VMEM((ns, rmax, d), F32)]),
        out_shape=jax.ShapeDtypeStruct((nt, rmax, d), BF16),
        compiler_params=_cparams(("parallel", "arbitrary")),
        name="moe_experts",
    )(seg, hs, cs, wg, wu, wd)


def _moe_combine_kernel(x_ref, meta_ref, y_ref, gf_ref, o_ref, *, rmax, final_norm):
    tb = x_ref.shape[0]
    lanef = lax.broadcasted_iota(I32, (tb, rmax), 1).astype(F32)
    pt = jnp.where(lanef == meta_ref[:, 0:1], 1.0, 0.0).astype(BF16)
    out = x_ref[...] + _dot(pt, y_ref[0])
    o_ref[...] = _rms(out, gf_ref[...]) if final_norm else out


def moe_combine(x2, meta, y, g_final, *, tm, tb, final_norm):
    t, d = x2.shape
    nt, rmax, _ = y.shape
    nb = tm // tb
    return pl.pallas_call(
        functools.partial(_moe_combine_kernel, rmax=rmax, final_norm=final_norm),
        grid=(nt, nb),
        in_specs=[pl.BlockSpec((tb, d), lambda i, j: (i * nb + j, 0)),
                  pl.BlockSpec((tb, LANES), lambda i, j: (i * nb + j, 0)),
                  pl.BlockSpec((1, rmax, d), lambda i, j: (i, 0, 0)),
                  pl.BlockSpec((1, d), lambda i, j: (0, 0))],
        out_specs=pl.BlockSpec((tb, d), lambda i, j: (i * nb + j, 0)),
        out_shape=jax.ShapeDtypeStruct((t, d), F32),
        compiler_params=_cparams(("parallel", "arbitrary")),
        name="moe_combine",
    )(x2, meta, y, g_final)


def _permute_w_in(w):
    d = w.shape[0]
    o = 0
    seg = {}
    for name, width in (("c", 768), ("gla", 1024), ("ga", GLA_RANK), ("s5u", 256), ("dq", 256), ("dk", 64),
                        ("dv", 64), ("iq", 256), ("ik", IDX_DIM), ("iw", IDX_HEADS), ("gates", 4096)):
        seg[name] = w[:, o:o + width]
        o += width
    z = lambda n: jnp.zeros((d, n), w.dtype)
    cols = [seg["c"], seg["s5u"], seg["gla"], seg["dq"], seg["iq"], seg["dk"], seg["dv"],
            seg["ik"], seg["iw"], z(LANES - IDX_DIM - IDX_HEADS), seg["ga"], z(LANES - GLA_RANK),
            z(P_GATES - P_GA - LANES), seg["gates"]]
    out = jnp.concatenate(cols, axis=1).astype(BF16)
    assert out.shape[1] == P_TOTAL
    return out


def _s5_matrices(bb_re, bb_im, c_re, c_im):
    g, p, c = S5_GROUPS, S5_STATE, S5_GROUP
    n = g * p
    rows_g = jnp.arange(g * c) // c
    cols_g = jnp.arange(n) // p
    mask = (rows_g[:, None] == cols_g[None, :]).astype(F32)
    bm = jnp.concatenate([jnp.tile(bb_re, (g, 1)) * mask, jnp.tile(bb_im, (g, 1)) * mask], axis=1)
    ct = lambda a: jnp.tile(jnp.transpose(a, (0, 2, 1)).reshape(n, c), (1, g)) * mask.T
    cm = jnp.concatenate([ct(c_re), -ct(c_im)], axis=0)
    return bm.astype(BF16), cm.astype(BF16)


def _pick(s, pref):
    for c in pref:
        if s % c == 0:
            return c
    return s


def kernel(x, mem, positions, norm_mix, w_in, conv_w, conv_b, gla_a_up, gla_a_b, gla_norm, s5_lambda_re,
           s5_lambda_im, s5_log_dt, s5_b_re, s5_b_im, s5_c_re, s5_c_im, s5_d, s5_w_glu, s5_b_glu, w_branch,
           w_out, norm_cross, w_cq, w_ckv, w_co, norm_ffn, w_route_group, b_route_group, w_route_expert,
           b_route_expert, w_e_gate, w_e_up, w_e_down, norm_mem, norm_final):
    b, s, d = x.shape
    t = b * s
    m = mem.shape[1]
    depth = w_in.shape[0]
    topk = min(DSA_TOPK, s // 4)
    ts = _pick(s, (512, 256, 128))
    tm = _pick(t, (1024, 512, 256, 128))
    row = lambda a: a.reshape(1, -1)

    tabs = rope_tables(positions, ts=ts)
    x2 = x.reshape(t, d)
    mem2 = mem.reshape(b * m, d)
    for l in range(depth):
        proj = norm_matmul(x2, row(norm_mix[l]), _permute_w_in(w_in[l]), tm=_pick(t, (2048, 1024, 512)), tn=1024)
        proj3 = proj.reshape(b, s, P_TOTAL)
        ya = conv_branch(proj3, conv_w[l], row(conv_b[l]), ts=ts)
        a_up_p = jnp.concatenate(
            [gla_a_up[l], jnp.zeros((LANES - GLA_RANK, gla_a_up.shape[2]), F32)], axis=0).astype(BF16)
        yb = gla_branch(proj3, a_up_p, row(gla_a_b[l]), row(gla_norm[l]), tt=ts)
        bb_re, bb_im, tab = s5_params(s5_lambda_re[l], s5_lambda_im[l], s5_log_dt[l], s5_b_re[l], s5_b_im[l])
        bm, cm = _s5_matrices(bb_re, bb_im, s5_c_re[l], s5_c_im[l])
        yc = s5_branch(proj3, bm, tab, cm, row(s5_d[l]), s5_w_glu[l].astype(BF16), row(s5_b_glu[l]), tt=ts)
        qs, qis, kr, vt4, kir, wt = dsa_prep(proj3, tabs, ts=ts)
        yd = dsa_attend(qs, qis, wt, kr, vt4, kir, qb=_pick(s, (4 * Q_BLOCK, 2 * Q_BLOCK, Q_BLOCK)), kt=ts, topk=topk)
        w2 = lambda a: a.reshape(t, a.shape[-1])
        x2 = merge_out(x2, w2(ya), w2(yb), w2(yc), w2(yd), proj, w_branch[l].astype(BF16),
                       w_out[l].astype(BF16), tm=min(tm, 512))
        kv = norm_matmul(mem2, row(norm_mem), w_ckv[l].astype(BF16), tm=_pick(b * m, (1024, 512, 256)), tn=1024)
        x2 = cross_attn(x2.reshape(b, s, d), row(norm_cross[l]), w_cq[l].astype(BF16),
                        kv.reshape(b, m, -1), w_co[l].astype(BF16), tm=ts).reshape(t, d)
        pad = jnp.zeros((d, LANES - N_EXPERTS - N_GROUPS), F32)
        wr = jnp.concatenate([w_route_expert[l], w_route_group[l], pad], axis=1)
        br = jnp.concatenate([b_route_expert[l], b_route_group[l], pad[0]], axis=0).reshape(1, LANES)
        hn, cmb = moe_router(x2, row(norm_ffn[l]), wr, br, tm=min(tm, 512))
        tmoe = _pick(t, (1024, 512))
        hs, cs, meta, seg = moe_sort(hn, cmb, tm=tmoe)
        half = MOE_SEG // 2
        seg1 = jnp.concatenate([seg[:, 0, :half], seg[:, 1, :half]], axis=1).reshape(-1)
        y = moe_experts(seg1, hs, cs, w_e_gate[l].astype(BF16), w_e_up[l].astype(BF16), w_e_down[l].astype(BF16),
                        ns=2 if (t // tmoe) % 2 == 0 else 1)
        x2 = moe_combine(x2, meta, y, row(norm_final), tm=tmoe, tb=min(tmoe, 512), final_norm=(l == depth - 1))
    return x2.reshape(b, s, d)
```

```python
import functools
import math

import numpy as np
import jax
import jax.numpy as jnp
from jax import lax
from jax.experimental import pallas as pl
from jax.experimental.pallas import tpu as pltpu

F32 = jnp.float32
BF16 = jnp.bfloat16
I32 = jnp.int32

D_MODEL = 1024
DEPTH = 2
EPS = 1e-6
N_BRANCH = 4
BRANCH_W = 256
CONV_W = 3
GLA_HEADS = 4
GLA_DK = 64
GLA_DV = 64
GLA_RANK = 16
GLA_TAU = 16.0
GLA_CHUNK = 64
S5_GROUP = 16
S5_GROUPS = BRANCH_W // S5_GROUP
S5_STATE = 64
DSA_HEADS = 4
DSA_HEAD_DIM = 64
IDX_HEADS = 8
IDX_DIM = 32
DSA_TOPK = 256
Q_BLOCK = 128
ROPE_THETA = 500000.0
ROPE_FRAC = 4
X_HEADS = 4
X_HEAD_DIM = 128
N_GROUPS = 4
EXPERTS_PER_GROUP = 4
N_EXPERTS = N_GROUPS * EXPERTS_PER_GROUP
D_FF_EXPERT = 512

LANES = 128
SUBLANES = 8
VMEM_LIMIT = 48 * 1024 * 1024

P_CONV = 0
P_S5U = 768
P_GLA = 1024
P_DQ = 2048
P_IQ = 2304
P_KV = 2560
P_IKW = 2688
P_GA = 2816
P_TOTAL = 3072

INT_MIN = -2147483648
NEG_INF_KEY = INT_MIN + 0x7FFFFF
VT_ROWS = DSA_HEAD_DIM + 2 * SUBLANES


def _cparams(sem):
    return pltpu.CompilerParams(dimension_semantics=sem, vmem_limit_bytes=VMEM_LIMIT)


def _dot(a, b):
    return jnp.dot(a, b, preferred_element_type=F32)


def _dot_nt(a, b):
    return lax.dot_general(a, b, (((1,), (1,)), ((), ())), preferred_element_type=F32)


def _dot_tn(a, b):
    return lax.dot_general(a, b, (((0,), (0,)), ((), ())), preferred_element_type=F32)


def _split_dot(exact_bf16, x):
    hi = x.astype(BF16)
    lo = (x - hi.astype(F32)).astype(BF16)
    return _dot(exact_bf16, hi) + _dot(exact_bf16, lo)


def _split_dot_r(x, exact_bf16):
    hi = x.astype(BF16)
    lo = (x - hi.astype(F32)).astype(BF16)
    return _dot(hi, exact_bf16) + _dot(lo, exact_bf16)


def _rms(x, g):
    return x * lax.rsqrt(jnp.mean(x * x, axis=-1, keepdims=True) + EPS) * g


def _sigmoid(x):
    return 1.0 / (1.0 + jnp.exp(-x))


def _norm_matmul_kernel(x_ref, g_ref, w_ref, o_ref, hn_ref):
    @pl.when(pl.program_id(1) == 0)
    def _():
        hn_ref[...] = _rms(x_ref[...], g_ref[...]).astype(BF16)

    o_ref[...] = _dot(hn_ref[...], w_ref[...]).astype(o_ref.dtype)


def norm_matmul(x, g, w, *, tm, tn, out_dtype=BF16):
    t, d = x.shape
    n = w.shape[1]
    return pl.pallas_call(
        _norm_matmul_kernel,
        grid=(t // tm, n // tn),
        in_specs=[
            pl.BlockSpec((tm, d), lambda i, j: (i, 0)),
            pl.BlockSpec((1, d), lambda i, j: (0, 0)),
            pl.BlockSpec((d, tn), lambda i, j: (0, j)),
        ],
        out_specs=pl.BlockSpec((tm, tn), lambda i, j: (i, j)),
        out_shape=jax.ShapeDtypeStruct((t, n), out_dtype),
        scratch_shapes=[pltpu.VMEM((tm, d), BF16)],
        compiler_params=_cparams(("parallel", "arbitrary")),
        name="norm_matmul",
    )(x, g, w)


def _conv_kernel(cur_ref, prev_ref, w_ref, b_ref, o_ref):
    i = pl.program_id(1)
    w_ = BRANCH_W
    cur = cur_ref[0].astype(F32)
    u = cur[:, 2 * w_:3 * w_] * cur[:, 0:w_]
    pv = prev_ref[0].astype(F32)
    pu = pv[:, 2 * w_:3 * w_] * pv[:, 0:w_]
    pu = jnp.where(i > 0, pu, 0.0)
    row = lax.broadcasted_iota(I32, u.shape, 0)
    u1 = jnp.where(row == 0, pu[7:8], pltpu.roll(u, 1, 0))
    u2 = jnp.where(row == 0, pu[6:7], jnp.where(row == 1, pu[7:8], pltpu.roll(u, 2, 0)))
    w = w_ref[...]
    y = w[0:1] * u2 + w[1:2] * u1 + w[2:3] * u + b_ref[...]
    o_ref[0] = (cur[:, w_:2 * w_] * y).astype(o_ref.dtype)


def conv_branch(proj3, conv_w, conv_b, *, ts):
    b, s, _ = proj3.shape
    wc = 3 * BRANCH_W
    hb = ts // SUBLANES
    return pl.pallas_call(
        _conv_kernel,
        grid=(b, s // ts),
        in_specs=[
            pl.BlockSpec((1, ts, wc), lambda bi, i: (bi, i, P_CONV // wc)),
            pl.BlockSpec((1, SUBLANES, wc), lambda bi, i: (bi, jnp.maximum(i * hb - 1, 0), P_CONV // wc)),
            pl.BlockSpec((CONV_W, BRANCH_W), lambda bi, i: (0, 0)),
            pl.BlockSpec((1, BRANCH_W), lambda bi, i: (0, 0)),
        ],
        out_specs=pl.BlockSpec((1, ts, BRANCH_W), lambda bi, i: (bi, i, 0)),
        out_shape=jax.ShapeDtypeStruct((b, s, BRANCH_W), BF16),
        compiler_params=_cparams(("parallel", "arbitrary")),
        name="conv_branch",
    )(proj3, proj3, conv_w, conv_b)


def _gla_kernel(g_ref, a_ref, aup_ref, ab_ref, gn_ref, ltb_ref, ltf_ref, bob_ref, bd_ref, bdb_ref,
                o_ref, st_ref, *, tt):
    hw = GLA_HEADS * GLA_DK
    c = GLA_CHUNK

    @pl.when(pl.program_id(1) == 0)
    def _():
        st_ref[...] = jnp.zeros_like(st_ref)

    blk = g_ref[0]
    q = blk[:, 0:hw].astype(F32)
    k = blk[:, hw:2 * hw].astype(F32)
    vb = blk[:, 2 * hw:3 * hw]
    v = vb.astype(F32)
    r = blk[:, 3 * hw:4 * hw].astype(F32)

    pre = _dot(a_ref[0], aup_ref[...]) + ab_ref[...]
    la = (jnp.minimum(pre, 0.0) - jnp.log(1.0 + jnp.exp(-jnp.abs(pre)))) * (1.0 / GLA_TAU)
    cum = _split_dot(ltb_ref[...], la)
    tot = _split_dot(bob_ref[...], la)
    q_dec = q * (GLA_DK ** -0.5) * jnp.exp(cum)
    k_inv = (k * jnp.exp(-cum)).astype(BF16)
    k_end = (k * jnp.exp(tot - cum)).astype(BF16)
    qdb = q_dec.astype(BF16)

    lane = lax.broadcasted_iota(I32, (1, hw), 1)
    ltmask = ltf_ref[...] > 0.0
    o = jnp.zeros((tt, hw), F32)
    for h in range(GLA_HEADS):
        hm = (lane // GLA_DK) == h
        qh = jnp.where(hm, q_dec, 0.0).astype(BF16)
        att = jnp.where(ltmask, _dot_nt(qh, k_inv), 0.0)
        vh = jnp.where(hm, v, 0.0).astype(BF16)
        o = o + _dot(att.astype(BF16), vh)

    st = st_ref[...]
    bd = bd_ref[...]
    inter = []
    for n in range(tt // c):
        rows = slice(n * c, (n + 1) * c)
        inter.append(_dot_nt(qdb[rows], st.astype(BF16)))
        dec = jnp.exp(tot[n * c:n * c + 1, :])
        st = st * dec + _dot_tn(vb[rows], k_end[rows]) * bd
    st_ref[...] = st
    o = o + jnp.concatenate(inter, axis=0)

    msq = _split_dot_r(o * o, bdb_ref[...]) * (1.0 / GLA_DV)
    y = o * lax.rsqrt(msq + EPS) * gn_ref[...] * (r * _sigmoid(r))
    o_ref[0] = y.astype(o_ref.dtype)


def _gla_consts(tt):
    ri = np.arange(tt)[:, None]
    ci = np.arange(tt)[None, :]
    same = (ri // GLA_CHUNK) == (ci // GLA_CHUNK)
    lt = (same & (ci <= ri)).astype(np.float32)
    hw = GLA_HEADS * GLA_DK
    hi = np.arange(hw)
    bd = ((hi[:, None] // GLA_DK) == (hi[None, :] // GLA_DK)).astype(np.float32)
    return (jnp.asarray(lt, BF16), jnp.asarray(lt, F32), jnp.asarray(same.astype(np.float32), BF16),
            jnp.asarray(bd, F32), jnp.asarray(bd, BF16))


def gla_branch(proj3, a_up_p, a_b, g_norm, *, tt):
    b, s, _ = proj3.shape
    hw = GLA_HEADS * GLA_DK
    ltb, ltf, bob, bd, bdb = _gla_consts(tt)
    const = lambda shape: pl.BlockSpec(shape, lambda bi, i: (0,) * len(shape))
    return pl.pallas_call(
        functools.partial(_gla_kernel, tt=tt),
        grid=(b, s // tt),
        in_specs=[
            pl.BlockSpec((1, tt, 4 * hw), lambda bi, i: (bi, i, P_GLA // (4 * hw))),
            pl.BlockSpec((1, tt, LANES), lambda bi, i: (bi, i, P_GA // LANES)),
            const((LANES, hw)), const((1, hw)), const((1, hw)),
            const((tt, tt)), const((tt, tt)), const((tt, tt)), const((hw, hw)), const((hw, hw)),
        ],
        out_specs=pl.BlockSpec((1, tt, hw), lambda bi, i: (bi, i, 0)),
        out_shape=jax.ShapeDtypeStruct((b, s, hw), BF16),
        scratch_shapes=[pltpu.VMEM((hw, hw), F32)],
        compiler_params=_cparams(("parallel", "arbitrary")),
        name="gla_branch",
    )(proj3, proj3, a_up_p, a_b, g_norm, ltb, ltf, bob, bd, bdb)


def _s5_param_kernel(lre_ref, lim_ref, ldt_ref, bre_ref, bim_ref, bbre_ref, bbim_ref, tab_ref):
    lre = lre_ref[...]
    lim = lim_ref[...]
    dt = jnp.exp(ldt_ref[...])
    mag = jnp.exp(lre * dt)
    lbr = mag * jnp.cos(lim * dt)
    lbi = mag * jnp.sin(lim * dt)
    den = lre * lre + lim * lim
    fre = ((lbr - 1.0) * lre + lbi * lim) / den
    fim = (lbi * lre - (lbr - 1.0) * lim) / den
    bre = bre_ref[...]
    bim = bim_ref[...]
    bbre_ref[...] = fre * bre - fim * bim
    bbim_ref[...] = fre * bim + fim * bre

    pw = [None, (lbr, lbi)]
    for _ in range(2, SUBLANES + 1):
        pr, pi = pw[-1]
        pw.append((pr * lbr - pi * lbi, pr * lbi + pi * lbr))
    n = lre.shape[-1]
    row = lax.broadcasted_iota(I32, (SUBLANES, n), 0)
    zero = jnp.zeros((SUBLANES, n), F32)
    for idx, sft in enumerate((1, 2, 4)):
        tab_ref[2 * idx] = jnp.where(row >= sft, jnp.broadcast_to(pw[sft][0], (SUBLANES, n)), zero)
        tab_ref[2 * idx + 1] = jnp.where(row >= sft, jnp.broadcast_to(pw[sft][1], (SUBLANES, n)), zero)
    cr, ci = zero, zero
    for rr in range(SUBLANES):
        cr = jnp.where(row == rr, jnp.broadcast_to(pw[rr + 1][0], (SUBLANES, n)), cr)
        ci = jnp.where(row == rr, jnp.broadcast_to(pw[rr + 1][1], (SUBLANES, n)), ci)
    tab_ref[6] = cr
    tab_ref[7] = ci


def s5_params(lam_re, lam_im, log_dt, b_re, b_im):
    g, p = lam_re.shape
    n = g * p
    row = lambda a: a.reshape(1, n)
    ldt = jnp.broadcast_to(log_dt[:, None], (g, p))
    bt = lambda a: jnp.transpose(a, (2, 0, 1)).reshape(S5_GROUP, n)
    full = lambda shape: pl.BlockSpec(shape, lambda: (0,) * len(shape))
    return pl.pallas_call(
        _s5_param_kernel,
        in_specs=[full((1, n))] * 3 + [full((S5_GROUP, n))] * 2,
        out_specs=[full((S5_GROUP, n)), full((S5_GROUP, n)), full((8, SUBLANES, n))],
        out_shape=[jax.ShapeDtypeStruct((S5_GROUP, n), F32), jax.ShapeDtypeStruct((S5_GROUP, n), F32),
                   jax.ShapeDtypeStruct((8, SUBLANES, n), F32)],
        name="s5_params",
    )(row(lam_re), row(lam_im), row(ldt), bt(b_re), bt(b_im))


def _s5_kernel(u_ref, bm_ref, tab_ref, cm_ref, d_ref, wg_ref, bg_ref, o_ref, xs_ref, car_ref, *, tt, n):
    @pl.when(pl.program_id(1) == 0)
    def _():
        car_ref[...] = jnp.zeros_like(car_ref)

    ub = u_ref[0]
    xs_ref[...] = _dot(ub, bm_ref[...])

    def group(gi, carry):
        r0 = pl.multiple_of(gi * SUBLANES, SUBLANES)
        for j in range(n // LANES):
            cre = slice(j * LANES, (j + 1) * LANES)
            cim = slice(n + j * LANES, n + (j + 1) * LANES)
            re = xs_ref[pl.ds(r0, SUBLANES), cre]
            im = xs_ref[pl.ds(r0, SUBLANES), cim]
            for idx, sft in enumerate((1, 2, 4)):
                ar = tab_ref[2 * idx, :, cre]
                ai = tab_ref[2 * idx + 1, :, cre]
                sr = pltpu.roll(re, sft, 0)
                si = pltpu.roll(im, sft, 0)
                re, im = re + ar * sr - ai * si, im + ar * si + ai * sr
            pr = tab_ref[6, :, cre]
            pi = tab_ref[7, :, cre]
            cr = car_ref[0, :, cre]
            ci = car_ref[1, :, cre]
            re, im = re + pr * cr - pi * ci, im + pr * ci + pi * cr
            xs_ref[pl.ds(r0, SUBLANES), cre] = re
            xs_ref[pl.ds(r0, SUBLANES), cim] = im
            car_ref[0, :, cre] = jnp.broadcast_to(re[SUBLANES - 1:SUBLANES], (SUBLANES, LANES))
            car_ref[1, :, cre] = jnp.broadcast_to(im[SUBLANES - 1:SUBLANES], (SUBLANES, LANES))
        return carry

    lax.fori_loop(0, tt // SUBLANES, group, 0)

    y = _dot(xs_ref[...].astype(BF16), cm_ref[...]) + d_ref[...] * ub.astype(F32)
    y = 0.5 * y * (1.0 + jnp.tanh(math.sqrt(2.0 / math.pi) * (y + 0.044715 * (y * y * y))))
    z = _dot(y.astype(BF16), wg_ref[...]) + bg_ref[...]
    o_ref[0] = (y * _sigmoid(z)).astype(o_ref.dtype)


def s5_branch(proj3, bmat, tab, cmat, d_skip, w_glu, b_glu, *, tt):
    b, s, _ = proj3.shape
    w_ = BRANCH_W
    n = S5_GROUPS * S5_STATE
    const = lambda shape: pl.BlockSpec(shape, lambda bi, i: (0,) * len(shape))
    return pl.pallas_call(
        functools.partial(_s5_kernel, tt=tt, n=n),
        grid=(b, s // tt),
        in_specs=[
            pl.BlockSpec((1, tt, w_), lambda bi, i: (bi, i, P_S5U // w_)),
            const((w_, 2 * n)), const((8, SUBLANES, n)), const((2 * n, w_)),
            const((1, w_)), const((w_, w_)), const((1, w_)),
        ],
        out_specs=pl.BlockSpec((1, tt, w_), lambda bi, i: (bi, i, 0)),
        out_shape=jax.ShapeDtypeStruct((b, s, w_), BF16),
        scratch_shapes=[pltpu.VMEM((tt, 2 * n), F32), pltpu.VMEM((2, SUBLANES, n), F32)],
        compiler_params=_cparams(("parallel", "arbitrary")),
        name="s5_branch",
    )(proj3, bmat, tab, cmat, d_skip, w_glu, b_glu)


def _rope_freq_rows():
    rows = np.zeros((8, LANES), np.float32)
    for pat, dh in enumerate((DSA_HEAD_DIM, IDX_DIM)):
        rd = dh // ROPE_FRAC
        half = rd // 2
        inv = (np.float32(ROPE_THETA) ** (-np.arange(half, dtype=np.float32) * np.float32(2.0 / rd))).astype(np.float32)
        for l in range(LANES):
            i = l % dh
            if i < half:
                rows[3 * pat, l] = inv[i]
                rows[3 * pat + 1, l] = -1.0
            elif i < rd:
                rows[3 * pat, l] = inv[i - half]
                rows[3 * pat + 2, l] = 1.0
    return rows


def _rope_tab_kernel(pos_ref, fr_ref, o_ref):
    pos = pos_ref[0]
    fr = fr_ref[...]
    for pat in range(2):
        ang = pos * fr[3 * pat:3 * pat + 1]
        c = jnp.cos(ang)
        s = jnp.sin(ang)
        o_ref[0, 3 * pat] = c
        o_ref[0, 3 * pat + 1] = s * fr[3 * pat + 1:3 * pat + 2]
        o_ref[0, 3 * pat + 2] = s * fr[3 * pat + 2:3 * pat + 3]


def rope_tables(positions, *, ts):
    b, s = positions.shape
    pos = positions.astype(F32).reshape(b, s, 1)
    fr = jnp.asarray(_rope_freq_rows())
    return pl.pallas_call(
        _rope_tab_kernel,
        grid=(b, s // ts),
        in_specs=[pl.BlockSpec((1, ts, 1), lambda bi, i: (bi, i, 0)),
                  pl.BlockSpec((8, LANES), lambda bi, i: (0, 0))],
        out_specs=pl.BlockSpec((1, 6, ts, LANES), lambda bi, i: (bi, 0, i, 0)),
        out_shape=jax.ShapeDtypeStruct((b, 6, s, LANES), F32),
        compiler_params=_cparams(("parallel", "arbitrary")),
        name="rope_tables",
    )(pos, fr)


def _rope_partner_matrix(n, dh, active):
    rd = dh // ROPE_FRAC
    half = rd // 2
    r = np.zeros((n, n), np.float32)
    for l in range(active):
        i = l % dh
        if i < half:
            r[l + half, l] = -1.0
        elif i < rd:
            r[l - half, l] = 1.0
    return jnp.asarray(r, BF16)


def _rope(tb, c, s, p_ref):
    return tb.astype(F32) * c + s * _dot(tb, p_ref[...])


def _dsa_prep_kernel(dq_ref, iq_ref, kv_ref, ikw_ref, tab_ref, pq_ref, pi_ref, pk_ref, pik_ref,
                     qs_ref, qis_ref, kr_ref, vt_ref, kir_ref, wt_ref):
    two = lambda a: jnp.concatenate([a, a], axis=1)
    c1, s1 = tab_ref[0, 0], tab_ref[0, 2] - tab_ref[0, 1]
    c2, s2 = tab_ref[0, 3], tab_ref[0, 5] - tab_ref[0, 4]
    lane = lax.broadcasted_iota(I32, (1, LANES), 1)

    q = _rope(dq_ref[0], two(c1), two(s1), pq_ref) * (DSA_HEAD_DIM ** -0.5 * math.log2(math.e))
    for h in range(DSA_HEADS):
        qs_ref[0, h] = q[:, h * DSA_HEAD_DIM:(h + 1) * DSA_HEAD_DIM].astype(BF16)
    qi = _rope(iq_ref[0], two(c2), two(s2), pi_ref)
    for h in range(IDX_HEADS):
        qis_ref[0, h] = qi[:, h * IDX_DIM:(h + 1) * IDX_DIM].astype(BF16)

    kvr = _rope(kv_ref[0], jnp.where(lane < DSA_HEAD_DIM, c1, 1.0), s1, pk_ref)
    kr_ref[0] = kvr[:, 0:DSA_HEAD_DIM].astype(BF16)
    ones = jnp.ones((VT_ROWS - DSA_HEAD_DIM, kvr.shape[0]), BF16)
    vt_ref[0, 0] = jnp.concatenate([kvr.T[DSA_HEAD_DIM:2 * DSA_HEAD_DIM].astype(BF16), ones], axis=0)

    ikr = _rope(ikw_ref[0], jnp.where(lane < IDX_DIM, c2, 1.0), s2, pik_ref)
    kir_ref[0] = ikr[:, 0:IDX_DIM].astype(BF16)
    wt_ref[0] = ikr.T[IDX_DIM:IDX_DIM + IDX_HEADS] * ((IDX_HEADS ** -0.5) * (IDX_DIM ** -0.5))


def dsa_prep(proj3, tabs, *, ts):
    b, s, _ = proj3.shape
    qw = DSA_HEADS * DSA_HEAD_DIM
    iw = IDX_HEADS * IDX_DIM
    return pl.pallas_call(
        _dsa_prep_kernel,
        grid=(b, s // ts),
        in_specs=[
            pl.BlockSpec((1, ts, qw), lambda bi, i: (bi, i, P_DQ // qw)),
            pl.BlockSpec((1, ts, iw), lambda bi, i: (bi, i, P_IQ // iw)),
            pl.BlockSpec((1, ts, LANES), lambda bi, i: (bi, i, P_KV // LANES)),
            pl.BlockSpec((1, ts, LANES), lambda bi, i: (bi, i, P_IKW // LANES)),
            pl.BlockSpec((1, 6, ts, LANES), lambda bi, i: (bi, 0, i, 0)),
            pl.BlockSpec((qw, qw), lambda bi, i: (0, 0)),
            pl.BlockSpec((iw, iw), lambda bi, i: (0, 0)),
            pl.BlockSpec((LANES, LANES), lambda bi, i: (0, 0)),
            pl.BlockSpec((LANES, LANES), lambda bi, i: (0, 0)),
        ],
        out_specs=[
            pl.BlockSpec((1, DSA_HEADS, ts, DSA_HEAD_DIM), lambda bi, i: (bi, 0, i, 0)),
            pl.BlockSpec((1, IDX_HEADS, ts, IDX_DIM), lambda bi, i: (bi, 0, i, 0)),
            pl.BlockSpec((1, ts, DSA_HEAD_DIM), lambda bi, i: (bi, i, 0)),
            pl.BlockSpec((1, 1, VT_ROWS, ts), lambda bi, i: (bi, i, 0, 0)),
            pl.BlockSpec((1, ts, IDX_DIM), lambda bi, i: (bi, i, 0)),
            pl.BlockSpec((1, IDX_HEADS, ts), lambda bi, i: (bi, 0, i)),
        ],
        out_shape=[
            jax.ShapeDtypeStruct((b, DSA_HEADS, s, DSA_HEAD_DIM), BF16),
            jax.ShapeDtypeStruct((b, IDX_HEADS, s, IDX_DIM), BF16),
            jax.ShapeDtypeStruct((b, s, DSA_HEAD_DIM), BF16),
            jax.ShapeDtypeStruct((b, s // ts, VT_ROWS, ts), BF16),
            jax.ShapeDtypeStruct((b, s, IDX_DIM), BF16),
            jax.ShapeDtypeStruct((b, IDX_HEADS, s), F32),
        ],
        compiler_params=_cparams(("parallel", "arbitrary")),
        name="dsa_prep",
    )(proj3, proj3, proj3, proj3, tabs,
      _rope_partner_matrix(qw, DSA_HEAD_DIM, qw), _rope_partner_matrix(iw, IDX_DIM, iw),
      _rope_partner_matrix(LANES, DSA_HEAD_DIM, DSA_HEAD_DIM), _rope_partner_matrix(LANES, IDX_DIM, IDX_DIM))


def _dsa_kernel(q_ref, qi_ref, w_ref, k_ref, vt_ref, ki_ref, lts_ref, o_ref, sc_ref, lg_ref, *, qb, kt, topk):
    i = pl.program_id(1)
    q0 = i * qb
    nkt = (q0 + qb + kt - 1) // kt
    kf = float(topk)
    sg = 8 * SUBLANES
    wrow = w_ref[0]
    tq = q0 + lax.broadcasted_iota(I32, (1, qb), 1)

    def to_key(x):
        bits = pltpu.bitcast(x, I32)
        return bits ^ ((bits >> 31) & 0x7FFFFFFF)

    def scores(t, gmax):
        k0 = pl.multiple_of(t * kt, kt)
        sb = LANES
        gm = [gmax[j * sb:(j + 1) * sb] for j in range(topk // sb)]
        for r in range(kt // sb):
            kit = ki_ref[0, pl.ds(k0 + r * sb, sb), :]
            acc = jnp.zeros((sb, qb), F32)
            for h in range(IDX_HEADS):
                acc = acc + jnp.maximum(_dot_nt(kit, qi_ref[0, h]), 0.0) * wrow[h:h + 1]
            acc = jnp.where(acc == 0.0, 0.0, acc)
            causal = (k0 + r * sb + lax.broadcasted_iota(I32, (sb, qb), 0)) <= tq
            sc_ref[t, r * sb:(r + 1) * sb, :] = jnp.where(causal, to_key(acc), INT_MIN)
            j = ((r * sb) % topk) // sb
            gm[j] = jnp.maximum(gm[j], jnp.where(causal, acc, -jnp.inf))
        return jnp.concatenate(gm, axis=0)

    gmax = lax.fori_loop(0, nkt, scores, jnp.full((topk, qb), -jnp.inf, F32))

    def count(pred):
        def body(t, c):
            m = jnp.where(pred(sc_ref[t]), 1.0, 0.0)
            return c + jnp.sum(m.reshape(kt // sg, sg, qb), axis=0)
        c = lax.fori_loop(0, nkt, body, jnp.zeros((sg, qb), F32))
        return jnp.sum(c, axis=0, keepdims=True)

    need = (tq + 1) > topk
    lo0 = to_key(jnp.min(gmax, axis=0, keepdims=True))
    hi0 = to_key(jnp.max(gmax, axis=0, keepdims=True))
    done0 = jnp.where(jnp.logical_and(need, lo0 < hi0), 0.0, 1.0)

    def bs_cond(c):
        it, _, _, _, done, _ = c
        return jnp.logical_and(it < 34, jnp.sum(1.0 - done) > 0.0)

    def bs_step(lo, hi, thr, done, below, probe=None):
        mid = (lo >> 1) + (hi >> 1) + (((lo & 1) + (hi & 1) + 1) >> 1)
        if probe is not None:
            mid = jnp.where(jnp.logical_and(lo < probe, probe <= hi), probe, mid)
        cnt = count(lambda kk: kk >= mid)
        ge = cnt >= kf
        hit = cnt == kf
        near = cnt == kf - 1.0
        lo = jnp.where(ge, mid, lo)
        hi = jnp.where(ge, hi, mid - 1)
        fin = jnp.where(hit, 1.0, jnp.where(near, 1.0, jnp.where(lo == hi, 1.0, 0.0)))
        val = jnp.where(hit, mid, jnp.where(near, mid - 1, lo))
        live = done == 0.0
        thr = jnp.where(live, val, thr)
        below = jnp.where(live, jnp.where(near, 1.0, 0.0), below)
        return lo, hi, thr, jnp.maximum(done, fin), below

    def bs_body(c):
        it = c[0]
        return (it + 2,) + bs_step(*bs_step(*c[1:]))

    st = bs_step(*bs_step(lo0, hi0, lo0, done0, jnp.zeros((1, qb), F32), probe=1), probe=0)
    _, _, _, thr, _, below = lax.while_loop(bs_cond, bs_body, (jnp.int32(2),) + st)

    def below_max(t, c):
        kk = sc_ref[t]
        cand = jnp.where(kk <= thr, kk, NEG_INF_KEY)
        return jnp.maximum(c, jnp.max(cand.reshape(kt // sg, sg, qb), axis=0))

    bm = lax.fori_loop(0, nkt, below_max, jnp.full((sg, qb), NEG_INF_KEY, I32))
    bmax = to_key(jnp.max(pltpu.bitcast(bm ^ ((bm >> 31) & 0x7FFFFFFF), F32), axis=0, keepdims=True))
    thr = jnp.where(below > 0.0, bmax, thr)
    thr = jnp.where(need, thr, INT_MIN + 1)

    cge = count(lambda kk: kk >= thr)
    tie = jnp.logical_and(need, cge > kf)

    @pl.when(jnp.sum(jnp.where(tie, 1.0, 0.0)) > 0.0)
    def _():
        cgt = count(lambda kk: kk > thr)
        room = kf - cgt

        def body(t, seen):
            kk = sc_ref[t]
            eq = jnp.logical_and(kk == thr, tie)
            eqf = jnp.where(eq, 1.0, 0.0)
            before = _dot(lts_ref[...], eqf.astype(BF16)) + seen
            sc_ref[t] = jnp.where(jnp.logical_and(eq, before >= room), INT_MIN, kk)
            return seen + jnp.sum(eqf, axis=0, keepdims=True)

        lax.fori_loop(0, nkt, body, jnp.zeros((1, qb), F32))

    nh = DSA_HEADS
    qa = lg_ref.shape[3]
    dh = DSA_HEAD_DIM
    for part in range(qb // qa):
        cols = slice(part * qa, (part + 1) * qa)
        thr_p = thr[:, cols]

        def logits(t, c, cols=cols, thr_p=thr_p):
            k0 = pl.multiple_of(t * kt, kt)
            kt_ = k_ref[0, pl.ds(k0, kt), :]
            bias = jnp.where(sc_ref[t, :, cols] >= thr_p, 0.0, -jnp.inf)
            out = []
            for h in range(nh):
                lg = _dot_nt(kt_, q_ref[0, h, cols, :]) + bias
                lg_ref[t, h] = lg
                out.append(jnp.maximum(c[h], jnp.max(lg.reshape(kt // sg, sg, qa), axis=0)))
            return tuple(out)

        mparts = lax.fori_loop(0, nkt, logits, (jnp.full((sg, qa), -jnp.inf, F32),) * nh)
        ms = []
        for h in range(nh):
            m = jnp.max(mparts[h], axis=0, keepdims=True)
            ms.append(jnp.where(m == -jnp.inf, 0.0, m))

        def attend(t, c, ms=ms):
            vt = vt_ref[0, t]
            out = []
            for h in range(nh):
                p = jnp.exp2((lg_ref[t, h] - ms[h]).astype(BF16))
                out.append(c[h] + _dot(vt, p))
            return tuple(out)

        res = lax.fori_loop(0, nkt, attend, (jnp.zeros((vt_ref.shape[2], qa), F32),) * nh)
        for h in range(nh):
            ot = res[h][0:dh] / res[h][dh:dh + 1]
            o_ref[0, cols, h * dh:(h + 1) * dh] = ot.T.astype(o_ref.dtype)


def dsa_attend(qs, qis, wt, kr, vt4, kir, *, qb, kt, topk):
    b, nh, s, dh = qs.shape
    nkt = s // kt
    assert topk % LANES == 0 and kt % topk == 0 and s % kt == 0 and s % qb == 0
    ri = np.arange(kt)
    lts = jnp.asarray((ri[None, :] < ri[:, None]).astype(np.float32), BF16)
    return pl.pallas_call(
        functools.partial(_dsa_kernel, qb=qb, kt=kt, topk=topk),
        grid=(b, s // qb),
        in_specs=[
            pl.BlockSpec((1, nh, qb, dh), lambda bi, i: (bi, 0, i, 0)),
            pl.BlockSpec((1, IDX_HEADS, qb, IDX_DIM), lambda bi, i: (bi, 0, i, 0)),
            pl.BlockSpec((1, IDX_HEADS, qb), lambda bi, i: (bi, 0, i)),
            pl.BlockSpec((1, s, dh), lambda bi, i: (bi, 0, 0)),
            pl.BlockSpec((1, nkt, VT_ROWS, kt), lambda bi, i: (bi, 0, 0, 0)),
            pl.BlockSpec((1, s, IDX_DIM), lambda bi, i: (bi, 0, 0)),
            pl.BlockSpec((kt, kt), lambda bi, i: (0, 0)),
        ],
        out_specs=pl.BlockSpec((1, qb, nh * dh), lambda bi, i: (bi, i, 0)),
        out_shape=jax.ShapeDtypeStruct((b, s, nh * dh), BF16),
        scratch_shapes=[pltpu.VMEM((nkt, kt, qb), I32), pltpu.VMEM((nkt, nh, kt, min(qb, 2 * Q_BLOCK)), F32)],
        compiler_params=_cparams(("parallel", "arbitrary")),
        name="dsa_attend",
    )(qs, qis, wt, kr, vt4, kir, lts)


def _merge_kernel(x_ref, g_ref, ya_ref, yb_ref, yc_ref, yd_ref, wg_ref, wb_ref, wo_ref, o_ref):
    ys = (ya_ref, yb_ref, yc_ref, yd_ref)
    x = x_ref[...]
    d = x.shape[1]
    hn = _rms(x, g_ref[...]).astype(BF16)
    merged = jnp.zeros(o_ref.shape, F32)
    for i in range(N_BRANCH):
        gate = _sigmoid(_dot(hn, wg_ref[:, i * d:(i + 1) * d]))
        merged = merged + gate * _dot(ys[i][...], wb_ref[i])
    o_ref[...] = x + _dot(merged.astype(BF16), wo_ref[...])


def merge_out(x2, g, ya, yb, yc, yd, w_gates, w_branch, w_out, *, tm):
    t, d = x2.shape
    w_ = BRANCH_W
    yspec = pl.BlockSpec((tm, w_), lambda i: (i, 0))
    return pl.pallas_call(
        _merge_kernel,
        grid=(t // tm,),
        in_specs=[pl.BlockSpec((tm, d), lambda i: (i, 0)), pl.BlockSpec((1, d), lambda i: (0, 0)),
                  yspec, yspec, yspec, yspec,
                  pl.BlockSpec((d, N_BRANCH * d), lambda i: (0, 0)),
                  pl.BlockSpec((N_BRANCH, w_, d), lambda i: (0, 0, 0)),
                  pl.BlockSpec((d, d), lambda i: (0, 0))],
        out_specs=pl.BlockSpec((tm, d), lambda i: (i, 0)),
        out_shape=jax.ShapeDtypeStruct((t, d), F32),
        compiler_params=_cparams(("parallel",)),
        name="merge_out",
    )(x2, g, ya, yb, yc, yd, w_gates, w_branch, w_out)


def _cross_kernel(x_ref, g_ref, wq_ref, kv_ref, wo_ref, o_ref):
    x = x_ref[0]
    hn = _rms(x, g_ref[...]).astype(BF16)
    q = _dot(hn, wq_ref[...]) * (X_HEAD_DIM ** -0.5)
    kv = kv_ref[0]
    hd = X_HEADS * X_HEAD_DIM
    outs = []
    for h in range(X_HEADS):
        cs = slice(h * X_HEAD_DIM, (h + 1) * X_HEAD_DIM)
        lg = _dot_nt(q[:, cs].astype(BF16), kv[:, cs])
        p = jnp.exp(lg - jnp.max(lg, axis=-1, keepdims=True))
        p = p / jnp.sum(p, axis=-1, keepdims=True)
        outs.append(_dot(p.astype(BF16), kv[:, hd + h * X_HEAD_DIM:hd + (h + 1) * X_HEAD_DIM]))
    o = jnp.concatenate(outs, axis=1).astype(BF16)
    o_ref[0] = x + _dot(o, wo_ref[...])


def cross_attn(x3, g, w_q, kv, w_o, *, tm):
    b, s, d = x3.shape
    m = kv.shape[1]
    hd = X_HEADS * X_HEAD_DIM
    return pl.pallas_call(
        _cross_kernel,
        grid=(b, s // tm),
        in_specs=[pl.BlockSpec((1, tm, d), lambda bi, i: (bi, i, 0)),
                  pl.BlockSpec((1, d), lambda bi, i: (0, 0)),
                  pl.BlockSpec((d, hd), lambda bi, i: (0, 0)),
                  pl.BlockSpec((1, m, 2 * hd), lambda bi, i: (bi, 0, 0)),
                  pl.BlockSpec((hd, d), lambda bi, i: (0, 0))],
        out_specs=pl.BlockSpec((1, tm, d), lambda bi, i: (bi, i, 0)),
        out_shape=jax.ShapeDtypeStruct((b, s, d), F32),
        compiler_params=_cparams(("parallel", "arbitrary")),
        name="cross_attn",
    )(x3, g, w_q, kv, w_o)


def _router_kernel(x_ref, g_ref, wr_ref, br_ref, hn_ref, cmb_ref):
    hn = _rms(x_ref[...], g_ref[...])
    hn_ref[...] = hn.astype(BF16)
    lg = jnp.dot(hn, wr_ref[...], precision=lax.Precision.HIGHEST, preferred_element_type=F32) + br_ref[...]
    lane = lax.broadcasted_iota(I32, lg.shape, 1)
    lanef = lane.astype(F32)
    ninf = -jnp.inf
    big = float(LANES)
    first = lambda mask: jnp.min(jnp.where(mask, lanef, big), axis=-1, keepdims=True)

    isg = jnp.logical_and(lane >= N_EXPERTS, lane < N_EXPERTS + N_GROUPS)
    gl = jnp.where(isg, lg, ninf)
    gmax = jnp.max(gl, axis=-1, keepdims=True)
    gsel = first(gl == gmax) - float(N_EXPERTS)
    pg = 1.0 / jnp.sum(jnp.exp(gl - gmax), axis=-1, keepdims=True)

    ise = jnp.floor(lanef * (1.0 / EXPERTS_PER_GROUP)) == gsel
    el = jnp.where(ise, lg, ninf)
    e1 = jnp.max(el, axis=-1, keepdims=True)
    i1 = first(el == e1)
    el2 = jnp.where(lanef == i1, ninf, el)
    e2 = jnp.max(el2, axis=-1, keepdims=True)
    i2 = first(el2 == e2)
    d = jnp.exp(e2 - e1)
    w1 = 1.0 / (1.0 + d)
    w2 = d / (1.0 + d)
    cmb = jnp.where(lanef == i1, pg * w1, jnp.where(lanef == i2, pg * w2, 0.0))
    a = jnp.minimum(i1, i2) - EXPERTS_PER_GROUP * gsel
    b = jnp.maximum(i1, i2) - EXPERTS_PER_GROUP * gsel
    rank = jnp.where(a == 0.0, b - 1.0, jnp.where(a == 1.0, jnp.where(b == 3.0, 3.0, 4.0), 5.0))
    cmb_ref[...] = jnp.where(lane == N_EXPERTS, gsel * MOE_PAIRS + rank, cmb)


def moe_router(x2, g, wr, br, *, tm):
    t, d = x2.shape
    return pl.pallas_call(
        _router_kernel,
        grid=(t // tm,),
        in_specs=[pl.BlockSpec((tm, d), lambda i: (i, 0)), pl.BlockSpec((1, d), lambda i: (0, 0)),
                  pl.BlockSpec((d, LANES), lambda i: (0, 0)), pl.BlockSpec((1, LANES), lambda i: (0, 0))],
        out_specs=[pl.BlockSpec((tm, d), lambda i: (i, 0)), pl.BlockSpec((tm, LANES), lambda i: (i, 0))],
        out_shape=[jax.ShapeDtypeStruct((t, d), BF16), jax.ShapeDtypeStruct((t, LANES), F32)],
        compiler_params=_cparams(("parallel",)),
        name="moe_router",
    )(x2, g, wr, br)


MOE_PAD = 2 * SUBLANES
MOE_SB = 256
MOE_PAIRS = 6
MOE_CLASSES = N_GROUPS * MOE_PAIRS
MOE_FIRST = (0, 0, 1, 2)
MOE_LAST = (2, 4, 5, 5)
MOE_CHUNK = (160, 256, 256, 208)
MOE_SEG = 64


def _moe_rmax(tm):
    return -(-(tm + N_GROUPS * MOE_PAD + max(MOE_CHUNK)) // LANES) * LANES


def _moe_class_matrices():
    c = np.arange(LANES)
    valid = c < MOE_CLASSES
    grp = c // MOE_PAIRS
    both = valid[:, None] & valid[None, :]
    before = both & (c[:, None] < c[None, :])
    same = both & (grp[:, None] == grp[None, :])
    lead = both & (c[:, None] % MOE_PAIRS == 0) & (grp[:, None] < grp[None, :])
    return jnp.asarray(np.stack([before, same, lead]).astype(np.float32))


def _moe_sort_kernel(hn_ref, cmb_ref, lts_ref, cm_ref, hs_ref, cs_ref, meta_ref, seg_ref, *, tm, rmax):
    cmb = cmb_ref[...]
    lanef = lax.broadcasted_iota(I32, (tm, LANES), 1).astype(F32)
    oh = jnp.where(lanef == cmb[:, N_EXPERTS:N_EXPERTS + 1], 1.0, 0.0)
    seen = jnp.zeros((1, LANES), F32)
    pres = []
    for sblk in range(tm // MOE_SB):
        ohs = oh[sblk * MOE_SB:(sblk + 1) * MOE_SB]
        pres.append(_dot(lts_ref[...], ohs.astype(BF16)) + seen)
        seen = seen + jnp.sum(ohs, axis=0, keepdims=True)
    pre = jnp.concatenate(pres, axis=0)
    cnt = jnp.broadcast_to(seen, (SUBLANES, LANES))
    exact = functools.partial(jnp.dot, precision=lax.Precision.HIGHEST, preferred_element_type=F32)
    tot = exact(cnt, cm_ref[1])
    padamt = jnp.ceil(tot * (1.0 / MOE_PAD)) * MOE_PAD - tot
    off = exact(cnt, cm_ref[0]) + exact(padamt, cm_ref[2])
    dest = jnp.sum(oh * (off[0:1] + pre), axis=-1, keepdims=True)
    destb = jnp.broadcast_to(dest, (tm, LANES))
    meta_ref[...] = destb
    row8 = lax.broadcasted_iota(I32, (SUBLANES, LANES), 0)
    seg_ref[0] = jnp.where(row8 == 0, off, jnp.where(row8 == 1, cnt, 0.0)).astype(I32)

    dest_row = destb.T[0:1]
    hn = hn_ref[...]
    chi = cmb.astype(BF16)
    clo = (cmb - chi.astype(F32)).astype(BF16)

    def blk(rb, carry):
        r0 = pl.multiple_of(rb * LANES, LANES)
        rows = (r0 + lax.broadcasted_iota(I32, (LANES, tm), 0)).astype(F32)
        p = jnp.where(rows == dest_row, 1.0, 0.0).astype(BF16)
        hs_ref[0, pl.ds(r0, LANES), :] = _dot(p, hn).astype(BF16)
        cs_ref[0, pl.ds(r0, LANES), :] = _dot(p, chi) + _dot(p, clo)
        return carry

    lax.fori_loop(0, rmax // LANES, blk, 0)


def moe_sort(hn, cmb, *, tm):
    t, d = hn.shape
    nt = t // tm
    rmax = _moe_rmax(tm)
    ri = np.arange(MOE_SB)
    lts = jnp.asarray((ri[None, :] < ri[:, None]).astype(np.float32), BF16)
    return pl.pallas_call(
        functools.partial(_moe_sort_kernel, tm=tm, rmax=rmax),
        grid=(nt,),
        in_specs=[pl.BlockSpec((tm, d), lambda i: (i, 0)), pl.BlockSpec((tm, LANES), lambda i: (i, 0)),
                  pl.BlockSpec((MOE_SB, MOE_SB), lambda i: (0, 0)),
                  pl.BlockSpec((3, LANES, LANES), lambda i: (0, 0, 0))],
        out_specs=[pl.BlockSpec((1, rmax, d), lambda i: (i, 0, 0)),
                   pl.BlockSpec((1, rmax, LANES), lambda i: (i, 0, 0)),
                   pl.BlockSpec((tm, LANES), lambda i: (i, 0)),
                   pl.BlockSpec((1, SUBLANES, LANES), lambda i: (i, 0, 0))],
        out_shape=[jax.ShapeDtypeStruct((nt, rmax, d), BF16), jax.ShapeDtypeStruct((nt, rmax, LANES), F32),
                   jax.ShapeDtypeStruct((t, LANES), F32), jax.ShapeDtypeStruct((nt, SUBLANES, LANES), I32)],
        compiler_params=_cparams(("parallel",)),
        name="moe_sort",
    )(hn, cmb, lts, _moe_class_matrices())


def _moe_expert_kernel(seg_ref, hs_ref, cs_ref, wg_ref, wu_ref, wd_ref, y_ref, acc_ref, *, ns):
    i = pl.program_id(0)
    e = pl.program_id(1)

    @pl.when(e == 0)
    def _():
        acc_ref[...] = jnp.zeros_like(acc_ref)

    g = e // EXPERTS_PER_GROUP

    def run(local, ch):
        for s in range(ns):
            base = (i * ns + s) * MOE_SEG + g * MOE_PAIRS
            start = seg_ref[base + MOE_FIRST[local]]
            end = seg_ref[base + MOE_LAST[local]] + seg_ref[base + MOE_SEG // 2 + MOE_LAST[local]]
            start = (start // MOE_PAD) * MOE_PAD

            def chunk(j, carry, s=s, start=start):
                r0 = pl.multiple_of(start + j * ch, MOE_PAD)
                hsl = hs_ref[s, pl.ds(r0, ch), :]
                gg = _dot(hsl, wg_ref[0])
                uu = _dot(hsl, wu_ref[0])
                c = cs_ref[s, pl.ds(r0, ch), :]
                lane = lax.broadcasted_iota(I32, c.shape, 1)
                cc = jnp.sum(jnp.where(lane == e, c, 0.0), axis=-1, keepdims=True)
                he = (gg * _sigmoid(gg)) * uu * cc
                acc_ref[s, pl.ds(r0, ch), :] += _dot(he.astype(BF16), wd_ref[0])
                return carry

            lax.fori_loop(0, (end - start + ch - 1) // ch, chunk, 0)

    for local in range(EXPERTS_PER_GROUP):
        pl.when(e % EXPERTS_PER_GROUP == local)(functools.partial(run, local, MOE_CHUNK[local]))

    @pl.when(e == pl.num_programs(1) - 1)
    def _():
        y_ref[...] = acc_ref[...].astype(BF16)


def moe_experts(seg, hs, cs, wg, wu, wd, *, ns):
    nt, rmax, d = hs.shape
    ne, _, f = wg.shape
    return pl.pallas_call(
        functools.partial(_moe_expert_kernel, ns=ns),
        grid_spec=pltpu.PrefetchScalarGridSpec(
            num_scalar_prefetch=1,
            grid=(nt // ns, ne),
            in_specs=[pl.BlockSpec((ns, rmax, d), lambda i, e, sref: (i, 0, 0)),
                      pl.BlockSpec((ns, rmax, LANES), lambda i, e, sref: (i, 0, 0)),
                      pl.BlockSpec((1, d, f), lambda i, e, sref: (e, 0, 0)),
                      pl.BlockSpec((1, d, f), lambda i, e, sref: (e, 0, 0)),
                      pl.BlockSpec((1, f, d), lambda i, e, sref: (e, 0, 0))],
            out_specs=pl.BlockSpec((ns, rmax, d), lambda i, e, sref: (i, 0, 0)),
            scratch_shapes=[pltpu.VMEM((ns, rmax, d), F32)]),
        out_shape=jax.ShapeDtypeStruct((nt, rmax, d), BF16),
        compiler_params=_cparams(("parallel", "arbitrary")),
        name="moe_experts",
    )(seg, hs, cs, wg, wu, wd)


def _moe_combine_kernel(x_ref, meta_ref, y_ref, gf_ref, o_ref, *, rmax, final_norm):
    tb = x_ref.shape[0]
    lanef = lax.broadcasted_iota(I32, (tb, rmax), 1).astype(F32)
    pt = jnp.where(lanef == meta_ref[:, 0:1], 1.0, 0.0).astype(BF16)
    out = x_ref[...] + _dot(pt, y_ref[0])
    o_ref[...] = _rms(out, gf_ref[...]) if final_norm else out


def moe_combine(x2, meta, y, g_final, *, tm, tb, final_norm):
    t, d = x2.shape
    nt, rmax, _ = y.shape
    nb = tm // tb
    return pl.pallas_call(
        functools.partial(_moe_combine_kernel, rmax=rmax, final_norm=final_norm),
        grid=(nt, nb),
        in_specs=[pl.BlockSpec((tb, d), lambda i, j: (i * nb + j, 0)),
                  pl.BlockSpec((tb, LANES), lambda i, j: (i * nb + j, 0)),
                  pl.BlockSpec((1, rmax, d), lambda i, j: (i, 0, 0)),
                  pl.BlockSpec((1, d), lambda i, j: (0, 0))],
        out_specs=pl.BlockSpec((tb, d), lambda i, j: (i * nb + j, 0)),
        out_shape=jax.ShapeDtypeStruct((t, d), F32),
        compiler_params=_cparams(("parallel", "arbitrary")),
        name="moe_combine",
    )(x2, meta, y, g_final)


def _permute_w_in(w):
    d = w.shape[0]
    o = 0
    seg = {}
    for name, width in (("c", 768), ("gla", 1024), ("ga", GLA_RANK), ("s5u", 256), ("dq", 256), ("dk", 64),
                        ("dv", 64), ("iq", 256), ("ik", IDX_DIM), ("iw", IDX_HEADS), ("gates", 4096)):
        seg[name] = w[:, o:o + width]
        o += width
    z = lambda n: jnp.zeros((d, n), w.dtype)
    cols = [seg["c"], seg["s5u"], seg["gla"], seg["dq"], seg["iq"], seg["dk"], seg["dv"],
            seg["ik"], seg["iw"], z(LANES - IDX_DIM - IDX_HEADS), seg["ga"], z(LANES - GLA_RANK),
            z(P_TOTAL - P_GA - LANES)]
    out = jnp.concatenate(cols, axis=1).astype(BF16)
    assert out.shape[1] == P_TOTAL
    return out, seg["gates"].astype(BF16)


def _s5_matrices(bb_re, bb_im, c_re, c_im):
    g, p, c = S5_GROUPS, S5_STATE, S5_GROUP
    n = g * p
    rows_g = jnp.arange(g * c) // c
    cols_g = jnp.arange(n) // p
    mask = (rows_g[:, None] == cols_g[None, :]).astype(F32)
    bm = jnp.concatenate([jnp.tile(bb_re, (g, 1)) * mask, jnp.tile(bb_im, (g, 1)) * mask], axis=1)
    ct = lambda a: jnp.tile(jnp.transpose(a, (0, 2, 1)).reshape(n, c), (1, g)) * mask.T
    cm = jnp.concatenate([ct(c_re), -ct(c_im)], axis=0)
    return bm.astype(BF16), cm.astype(BF16)


def _pick(s, pref):
    for c in pref:
        if s % c == 0:
            return c
    return s


def kernel(x, mem, positions, norm_mix, w_in, conv_w, conv_b, gla_a_up, gla_a_b, gla_norm, s5_lambda_re,
           s5_lambda_im, s5_log_dt, s5_b_re, s5_b_im, s5_c_re, s5_c_im, s5_d, s5_w_glu, s5_b_glu, w_branch,
           w_out, norm_cross, w_cq, w_ckv, w_co, norm_ffn, w_route_group, b_route_group, w_route_expert,
           b_route_expert, w_e_gate, w_e_up, w_e_down, norm_mem, norm_final):
    b, s, d = x.shape
    t = b * s
    m = mem.shape[1]
    depth = w_in.shape[0]
    topk = min(DSA_TOPK, s // 4)
    ts = _pick(s, (512, 256, 128))
    tm = _pick(t, (1024, 512, 256, 128))
    row = lambda a: a.reshape(1, -1)

    tabs = rope_tables(positions, ts=ts)
    x2 = x.reshape(t, d)
    mem2 = mem.reshape(b * m, d)
    for l in range(depth):
        w_mix, w_gates = _permute_w_in(w_in[l])
        proj = norm_matmul(x2, row(norm_mix[l]), w_mix, tm=_pick(t, (2048, 1024, 512)), tn=1024)
        proj3 = proj.reshape(b, s, P_TOTAL)
        ya = conv_branch(proj3, conv_w[l], row(conv_b[l]), ts=ts)
        a_up_p = jnp.concatenate(
            [gla_a_up[l], jnp.zeros((LANES - GLA_RANK, gla_a_up.shape[2]), F32)], axis=0).astype(BF16)
        yb = gla_branch(proj3, a_up_p, row(gla_a_b[l]), row(gla_norm[l]), tt=ts)
        bb_re, bb_im, tab = s5_params(s5_lambda_re[l], s5_lambda_im[l], s5_log_dt[l], s5_b_re[l], s5_b_im[l])
        bm, cm = _s5_matrices(bb_re, bb_im, s5_c_re[l], s5_c_im[l])
        yc = s5_branch(proj3, bm, tab, cm, row(s5_d[l]), s5_w_glu[l].astype(BF16), row(s5_b_glu[l]), tt=ts)
        qs, qis, kr, vt4, kir, wt = dsa_prep(proj3, tabs, ts=ts)
        yd = dsa_attend(qs, qis, wt, kr, vt4, kir, qb=_pick(s, (2 * Q_BLOCK, Q_BLOCK)), kt=ts, topk=topk)
        w2 = lambda a: a.reshape(t, a.shape[-1])
        x2 = merge_out(x2, row(norm_mix[l]), w2(ya), w2(yb), w2(yc), w2(yd), w_gates, w_branch[l].astype(BF16),
                       w_out[l].astype(BF16), tm=min(tm, 512))
        kv = norm_matmul(mem2, row(norm_mem), w_ckv[l].astype(BF16), tm=_pick(b * m, (1024, 512, 256)), tn=1024)
        x2 = cross_attn(x2.reshape(b, s, d), row(norm_cross[l]), w_cq[l].astype(BF16),
                        kv.reshape(b, m, -1), w_co[l].astype(BF16), tm=ts).reshape(t, d)
        pad = jnp.zeros((d, LANES - N_EXPERTS - N_GROUPS), F32)
        wr = jnp.concatenate([w_route_expert[l], w_route_group[l], pad], axis=1)
        br = jnp.concatenate([b_route_expert[l], b_route_group[l], pad[0]], axis=0).reshape(1, LANES)
        hn, cmb = moe_router(x2, row(norm_ffn[l]), wr, br, tm=min(tm, 512))
        tmoe = _pick(t, (1024, 512))
        hs, cs, meta, seg = moe_sort(hn, cmb, tm=tmoe)
        half = MOE_SEG // 2
        seg1 = jnp.concatenate([seg[:, 0, :half], seg[:, 1, :half]], axis=1).reshape(-1)
        y = moe_experts(seg1, hs, cs, w_e_gate[l].astype(BF16), w_e_up[l].astype(BF16), w_e_down[l].astype(BF16),
                        ns=2 if (t // tmoe) % 2 == 0 else 1)
        x2 = moe_combine(x2, meta, y, row(norm_final), tm=tmoe, tb=min(tmoe, 512), final_norm=(l == depth - 1))
    return x2.reshape(b, s, d)
```

```python
import functools
import math

import numpy as np
import jax
import jax.numpy as jnp
from jax import lax
from jax.experimental import pallas as pl
from jax.experimental.pallas import tpu as pltpu

F32 = jnp.float32
BF16 = jnp.bfloat16
I32 = jnp.int32

D_MODEL = 1024
DEPTH = 2
EPS = 1e-6
N_BRANCH = 4
BRANCH_W = 256
CONV_W = 3
GLA_HEADS = 4
GLA_DK = 64
GLA_DV = 64
GLA_RANK = 16
GLA_TAU = 16.0
GLA_CHUNK = 64
S5_GROUP = 16
S5_GROUPS = BRANCH_W // S5_GROUP
S5_STATE = 64
DSA_HEADS = 4
DSA_HEAD_DIM = 64
IDX_HEADS = 8
IDX_DIM = 32
DSA_TOPK = 256
Q_BLOCK = 128
ROPE_THETA = 500000.0
ROPE_FRAC = 4
X_HEADS = 4
X_HEAD_DIM = 128
N_GROUPS = 4
EXPERTS_PER_GROUP = 4
N_EXPERTS = N_GROUPS * EXPERTS_PER_GROUP
D_FF_EXPERT = 512

LANES = 128
SUBLANES = 8
VMEM_LIMIT = 48 * 1024 * 1024

P_CONV = 0
P_S5U = 768
P_GLA = 1024
P_DQ = 2048
P_IQ = 2304
P_KV = 2560
P_IKW = 2688
P_GA = 2816
P_TOTAL = 3072

INT_MIN = -2147483648
NEG_INF_KEY = INT_MIN + 0x7FFFFF
VT_ROWS = DSA_HEAD_DIM + 2 * SUBLANES


def _cparams(sem):
    return pltpu.CompilerParams(dimension_semantics=sem, vmem_limit_bytes=VMEM_LIMIT)


def _dot(a, b):
    return jnp.dot(a, b, preferred_element_type=F32)


def _dot_nt(a, b):
    return lax.dot_general(a, b, (((1,), (1,)), ((), ())), preferred_element_type=F32)


def _dot_tn(a, b):
    return lax.dot_general(a, b, (((0,), (0,)), ((), ())), preferred_element_type=F32)


def _split_dot(exact_bf16, x):
    hi = x.astype(BF16)
    lo = (x - hi.astype(F32)).astype(BF16)
    return _dot(exact_bf16, hi) + _dot(exact_bf16, lo)


def _split_dot_r(x, exact_bf16):
    hi = x.astype(BF16)
    lo = (x - hi.astype(F32)).astype(BF16)
    return _dot(hi, exact_bf16) + _dot(lo, exact_bf16)


def _rms(x, g):
    return x * lax.rsqrt(jnp.mean(x * x, axis=-1, keepdims=True) + EPS) * g


def _sigmoid(x):
    return 1.0 / (1.0 + jnp.exp(-x))


def _norm_matmul_kernel(x_ref, g_ref, w_ref, o_ref, hn_ref):
    @pl.when(pl.program_id(1) == 0)
    def _():
        hn_ref[...] = _rms(x_ref[...], g_ref[...]).astype(BF16)

    o_ref[...] = _dot(hn_ref[...], w_ref[...]).astype(o_ref.dtype)


def norm_matmul(x, g, w, *, tm, tn, out_dtype=BF16):
    t, d = x.shape
    n = w.shape[1]
    return pl.pallas_call(
        _norm_matmul_kernel,
        grid=(t // tm, n // tn),
        in_specs=[
            pl.BlockSpec((tm, d), lambda i, j: (i, 0)),
            pl.BlockSpec((1, d), lambda i, j: (0, 0)),
            pl.BlockSpec((d, tn), lambda i, j: (0, j)),
        ],
        out_specs=pl.BlockSpec((tm, tn), lambda i, j: (i, j)),
        out_shape=jax.ShapeDtypeStruct((t, n), out_dtype),
        scratch_shapes=[pltpu.VMEM((tm, d), BF16)],
        compiler_params=_cparams(("parallel", "arbitrary")),
        name="norm_matmul",
    )(x, g, w)


def _conv_kernel(cur_ref, prev_ref, w_ref, b_ref, o_ref):
    i = pl.program_id(1)
    w_ = BRANCH_W
    cur = cur_ref[0].astype(F32)
    u = cur[:, 2 * w_:3 * w_] * cur[:, 0:w_]
    pv = prev_ref[0].astype(F32)
    pu = pv[:, 2 * w_:3 * w_] * pv[:, 0:w_]
    pu = jnp.where(i > 0, pu, 0.0)
    row = lax.broadcasted_iota(I32, u.shape, 0)
    u1 = jnp.where(row == 0, pu[7:8], pltpu.roll(u, 1, 0))
    u2 = jnp.where(row == 0, pu[6:7], jnp.where(row == 1, pu[7:8], pltpu.roll(u, 2, 0)))
    w = w_ref[...]
    y = w[0:1] * u2 + w[1:2] * u1 + w[2:3] * u + b_ref[...]
    o_ref[0] = (cur[:, w_:2 * w_] * y).astype(o_ref.dtype)


def conv_branch(proj3, conv_w, conv_b, *, ts):
    b, s, _ = proj3.shape
    wc = 3 * BRANCH_W
    hb = ts // SUBLANES
    return pl.pallas_call(
        _conv_kernel,
        grid=(b, s // ts),
        in_specs=[
            pl.BlockSpec((1, ts, wc), lambda bi, i: (bi, i, P_CONV // wc)),
            pl.BlockSpec((1, SUBLANES, wc), lambda bi, i: (bi, jnp.maximum(i * hb - 1, 0), P_CONV // wc)),
            pl.BlockSpec((CONV_W, BRANCH_W), lambda bi, i: (0, 0)),
            pl.BlockSpec((1, BRANCH_W), lambda bi, i: (0, 0)),
        ],
        out_specs=pl.BlockSpec((1, ts, BRANCH_W), lambda bi, i: (bi, i, 0)),
        out_shape=jax.ShapeDtypeStruct((b, s, BRANCH_W), BF16),
        compiler_params=_cparams(("parallel", "arbitrary")),
        name="conv_branch",
    )(proj3, proj3, conv_w, conv_b)


def _gla_kernel(g_ref, a_ref, aup_ref, ab_ref, gn_ref, ltb_ref, ltf_ref, bob_ref, bd_ref, bdb_ref,
                o_ref, st_ref, *, tt):
    hw = GLA_HEADS * GLA_DK
    c = GLA_CHUNK

    @pl.when(pl.program_id(1) == 0)
    def _():
        st_ref[...] = jnp.zeros_like(st_ref)

    blk = g_ref[0]
    q = blk[:, 0:hw].astype(F32)
    k = blk[:, hw:2 * hw].astype(F32)
    vb = blk[:, 2 * hw:3 * hw]
    v = vb.astype(F32)
    r = blk[:, 3 * hw:4 * hw].astype(F32)

    pre = _dot(a_ref[0], aup_ref[...]) + ab_ref[...]
    la = (jnp.minimum(pre, 0.0) - jnp.log(1.0 + jnp.exp(-jnp.abs(pre)))) * (1.0 / GLA_TAU)
    cum = _split_dot(ltb_ref[...], la)
    tot = _split_dot(bob_ref[...], la)
    q_dec = q * (GLA_DK ** -0.5) * jnp.exp(cum)
    k_inv = (k * jnp.exp(-cum)).astype(BF16)
    k_end = (k * jnp.exp(tot - cum)).astype(BF16)
    qdb = q_dec.astype(BF16)

    lane = lax.broadcasted_iota(I32, (1, hw), 1)
    ltmask = ltf_ref[...] > 0.0
    o = jnp.zeros((tt, hw), F32)
    for h in range(GLA_HEADS):
        hm = (lane // GLA_DK) == h
        qh = jnp.where(hm, q_dec, 0.0).astype(BF16)
        att = jnp.where(ltmask, _dot_nt(qh, k_inv), 0.0)
        vh = jnp.where(hm, v, 0.0).astype(BF16)
        o = o + _dot(att.astype(BF16), vh)

    st = st_ref[...]
    bd = bd_ref[...]
    inter = []
    for n in range(tt // c):
        rows = slice(n * c, (n + 1) * c)
        inter.append(_dot_nt(qdb[rows], st.astype(BF16)))
        dec = jnp.exp(tot[n * c:n * c + 1, :])
        st = st * dec + _dot_tn(vb[rows], k_end[rows]) * bd
    st_ref[...] = st
    o = o + jnp.concatenate(inter, axis=0)

    msq = _split_dot_r(o * o, bdb_ref[...]) * (1.0 / GLA_DV)
    y = o * lax.rsqrt(msq + EPS) * gn_ref[...] * (r * _sigmoid(r))
    o_ref[0] = y.astype(o_ref.dtype)


def _gla_consts(tt):
    ri = np.arange(tt)[:, None]
    ci = np.arange(tt)[None, :]
    same = (ri // GLA_CHUNK) == (ci // GLA_CHUNK)
    lt = (same & (ci <= ri)).astype(np.float32)
    hw = GLA_HEADS * GLA_DK
    hi = np.arange(hw)
    bd = ((hi[:, None] // GLA_DK) == (hi[None, :] // GLA_DK)).astype(np.float32)
    return (jnp.asarray(lt, BF16), jnp.asarray(lt, F32), jnp.asarray(same.astype(np.float32), BF16),
            jnp.asarray(bd, F32), jnp.asarray(bd, BF16))


def gla_branch(proj3, a_up_p, a_b, g_norm, *, tt):
    b, s, _ = proj3.shape
    hw = GLA_HEADS * GLA_DK
    ltb, ltf, bob, bd, bdb = _gla_consts(tt)
    const = lambda shape: pl.BlockSpec(shape, lambda bi, i: (0,) * len(shape))
    return pl.pallas_call(
        functools.partial(_gla_kernel, tt=tt),
        grid=(b, s // tt),
        in_specs=[
            pl.BlockSpec((1, tt, 4 * hw), lambda bi, i: (bi, i, P_GLA // (4 * hw))),
            pl.BlockSpec((1, tt, LANES), lambda bi, i: (bi, i, P_GA // LANES)),
            const((LANES, hw)), const((1, hw)), const((1, hw)),
            const((tt, tt)), const((tt, tt)), const((tt, tt)), const((hw, hw)), const((hw, hw)),
        ],
        out_specs=pl.BlockSpec((1, tt, hw), lambda bi, i: (bi, i, 0)),
        out_shape=jax.ShapeDtypeStruct((b, s, hw), BF16),
        scratch_shapes=[pltpu.VMEM((hw, hw), F32)],
        compiler_params=_cparams(("parallel", "arbitrary")),
        name="gla_branch",
    )(proj3, proj3, a_up_p, a_b, g_norm, ltb, ltf, bob, bd, bdb)


def _s5_param_kernel(lre_ref, lim_ref, ldt_ref, bre_ref, bim_ref, bbre_ref, bbim_ref, tab_ref):
    lre = lre_ref[...]
    lim = lim_ref[...]
    dt = jnp.exp(ldt_ref[...])
    mag = jnp.exp(lre * dt)
    lbr = mag * jnp.cos(lim * dt)
    lbi = mag * jnp.sin(lim * dt)
    den = lre * lre + lim * lim
    fre = ((lbr - 1.0) * lre + lbi * lim) / den
    fim = (lbi * lre - (lbr - 1.0) * lim) / den
    bre = bre_ref[...]
    bim = bim_ref[...]
    bbre_ref[...] = fre * bre - fim * bim
    bbim_ref[...] = fre * bim + fim * bre

    pw = [None, (lbr, lbi)]
    for _ in range(2, SUBLANES + 1):
        pr, pi = pw[-1]
        pw.append((pr * lbr - pi * lbi, pr * lbi + pi * lbr))
    n = lre.shape[-1]
    row = lax.broadcasted_iota(I32, (SUBLANES, n), 0)
    zero = jnp.zeros((SUBLANES, n), F32)
    for idx, sft in enumerate((1, 2, 4)):
        tab_ref[2 * idx] = jnp.where(row >= sft, jnp.broadcast_to(pw[sft][0], (SUBLANES, n)), zero)
        tab_ref[2 * idx + 1] = jnp.where(row >= sft, jnp.broadcast_to(pw[sft][1], (SUBLANES, n)), zero)
    cr, ci = zero, zero
    for rr in range(SUBLANES):
        cr = jnp.where(row == rr, jnp.broadcast_to(pw[rr + 1][0], (SUBLANES, n)), cr)
        ci = jnp.where(row == rr, jnp.broadcast_to(pw[rr + 1][1], (SUBLANES, n)), ci)
    tab_ref[6] = cr
    tab_ref[7] = ci


def s5_params(lam_re, lam_im, log_dt, b_re, b_im):
    g, p = lam_re.shape
    n = g * p
    row = lambda a: a.reshape(1, n)
    ldt = jnp.broadcast_to(log_dt[:, None], (g, p))
    bt = lambda a: jnp.transpose(a, (2, 0, 1)).reshape(S5_GROUP, n)
    full = lambda shape: pl.BlockSpec(shape, lambda: (0,) * len(shape))
    return pl.pallas_call(
        _s5_param_kernel,
        in_specs=[full((1, n))] * 3 + [full((S5_GROUP, n))] * 2,
        out_specs=[full((S5_GROUP, n)), full((S5_GROUP, n)), full((8, SUBLANES, n))],
        out_shape=[jax.ShapeDtypeStruct((S5_GROUP, n), F32), jax.ShapeDtypeStruct((S5_GROUP, n), F32),
                   jax.ShapeDtypeStruct((8, SUBLANES, n), F32)],
        name="s5_params",
    )(row(lam_re), row(lam_im), row(ldt), bt(b_re), bt(b_im))


def _s5_kernel(u_ref, bm_ref, tab_ref, cm_ref, d_ref, wg_ref, bg_ref, o_ref, xs_ref, car_ref, *, tt, n):
    @pl.when(pl.program_id(1) == 0)
    def _():
        car_ref[...] = jnp.zeros_like(car_ref)

    ub = u_ref[0]
    xs_ref[...] = _dot(ub, bm_ref[...])

    def group(gi, carry):
        r0 = pl.multiple_of(gi * SUBLANES, SUBLANES)
        for j in range(n // LANES):
            cre = slice(j * LANES, (j + 1) * LANES)
            cim = slice(n + j * LANES, n + (j + 1) * LANES)
            re = xs_ref[pl.ds(r0, SUBLANES), cre]
            im = xs_ref[pl.ds(r0, SUBLANES), cim]
            for idx, sft in enumerate((1, 2, 4)):
                ar = tab_ref[2 * idx, :, cre]
                ai = tab_ref[2 * idx + 1, :, cre]
                sr = pltpu.roll(re, sft, 0)
                si = pltpu.roll(im, sft, 0)
                re, im = re + ar * sr - ai * si, im + ar * si + ai * sr
            pr = tab_ref[6, :, cre]
            pi = tab_ref[7, :, cre]
            cr = car_ref[0, :, cre]
            ci = car_ref[1, :, cre]
            re, im = re + pr * cr - pi * ci, im + pr * ci + pi * cr
            xs_ref[pl.ds(r0, SUBLANES), cre] = re
            xs_ref[pl.ds(r0, SUBLANES), cim] = im
            car_ref[0, :, cre] = jnp.broadcast_to(re[SUBLANES - 1:SUBLANES], (SUBLANES, LANES))
            car_ref[1, :, cre] = jnp.broadcast_to(im[SUBLANES - 1:SUBLANES], (SUBLANES, LANES))
        return carry

    lax.fori_loop(0, tt // SUBLANES, group, 0)

    y = _dot(xs_ref[...].astype(BF16), cm_ref[...]) + d_ref[...] * ub.astype(F32)
    y = 0.5 * y * (1.0 + jnp.tanh(math.sqrt(2.0 / math.pi) * (y + 0.044715 * (y * y * y))))
    z = _dot(y.astype(BF16), wg_ref[...]) + bg_ref[...]
    o_ref[0] = (y * _sigmoid(z)).astype(o_ref.dtype)


def s5_branch(proj3, bmat, tab, cmat, d_skip, w_glu, b_glu, *, tt):
    b, s, _ = proj3.shape
    w_ = BRANCH_W
    n = S5_GROUPS * S5_STATE
    const = lambda shape: pl.BlockSpec(shape, lambda bi, i: (0,) * len(shape))
    return pl.pallas_call(
        functools.partial(_s5_kernel, tt=tt, n=n),
        grid=(b, s // tt),
        in_specs=[
            pl.BlockSpec((1, tt, w_), lambda bi, i: (bi, i, P_S5U // w_)),
            const((w_, 2 * n)), const((8, SUBLANES, n)), const((2 * n, w_)),
            const((1, w_)), const((w_, w_)), const((1, w_)),
        ],
        out_specs=pl.BlockSpec((1, tt, w_), lambda bi, i: (bi, i, 0)),
        out_shape=jax.ShapeDtypeStruct((b, s, w_), BF16),
        scratch_shapes=[pltpu.VMEM((tt, 2 * n), F32), pltpu.VMEM((2, SUBLANES, n), F32)],
        compiler_params=_cparams(("parallel", "arbitrary")),
        name="s5_branch",
    )(proj3, bmat, tab, cmat, d_skip, w_glu, b_glu)


def _rope_freq_rows():
    rows = np.zeros((8, LANES), np.float32)
    for pat, dh in enumerate((DSA_HEAD_DIM, IDX_DIM)):
        rd = dh // ROPE_FRAC
        half = rd // 2
        inv = (np.float32(ROPE_THETA) ** (-np.arange(half, dtype=np.float32) * np.float32(2.0 / rd))).astype(np.float32)
        for l in range(LANES):
            i = l % dh
            if i < half:
                rows[3 * pat, l] = inv[i]
                rows[3 * pat + 1, l] = -1.0
            elif i < rd:
                rows[3 * pat, l] = inv[i - half]
                rows[3 * pat + 2, l] = 1.0
    return rows


def _rope_tab_kernel(pos_ref, fr_ref, o_ref):
    pos = pos_ref[0]
    fr = fr_ref[...]
    for pat in range(2):
        ang = pos * fr[3 * pat:3 * pat + 1]
        c = jnp.cos(ang)
        s = jnp.sin(ang)
        o_ref[0, 3 * pat] = c
        o_ref[0, 3 * pat + 1] = s * fr[3 * pat + 1:3 * pat + 2]
        o_ref[0, 3 * pat + 2] = s * fr[3 * pat + 2:3 * pat + 3]


def rope_tables(positions, *, ts):
    b, s = positions.shape
    pos = positions.astype(F32).reshape(b, s, 1)
    fr = jnp.asarray(_rope_freq_rows())
    return pl.pallas_call(
        _rope_tab_kernel,
        grid=(b, s // ts),
        in_specs=[pl.BlockSpec((1, ts, 1), lambda bi, i: (bi, i, 0)),
                  pl.BlockSpec((8, LANES), lambda bi, i: (0, 0))],
        out_specs=pl.BlockSpec((1, 6, ts, LANES), lambda bi, i: (bi, 0, i, 0)),
        out_shape=jax.ShapeDtypeStruct((b, 6, s, LANES), F32),
        compiler_params=_cparams(("parallel", "arbitrary")),
        name="rope_tables",
    )(pos, fr)


def _rope_partner_matrix(n, dh, active):
    rd = dh // ROPE_FRAC
    half = rd // 2
    r = np.zeros((n, n), np.float32)
    for l in range(active):
        i = l % dh
        if i < half:
            r[l + half, l] = -1.0
        elif i < rd:
            r[l - half, l] = 1.0
    return jnp.asarray(r, BF16)


def _rope(tb, c, s, p_ref):
    return tb.astype(F32) * c + s * _dot(tb, p_ref[...])


def _dsa_prep_kernel(dq_ref, iq_ref, kv_ref, ikw_ref, tab_ref, pq_ref, pi_ref, pk_ref, pik_ref,
                     qs_ref, qis_ref, kr_ref, vt_ref, kir_ref, wt_ref):
    two = lambda a: jnp.concatenate([a, a], axis=1)
    c1, s1 = tab_ref[0, 0], tab_ref[0, 2] - tab_ref[0, 1]
    c2, s2 = tab_ref[0, 3], tab_ref[0, 5] - tab_ref[0, 4]
    lane = lax.broadcasted_iota(I32, (1, LANES), 1)

    q = _rope(dq_ref[0], two(c1), two(s1), pq_ref) * (DSA_HEAD_DIM ** -0.5 * math.log2(math.e))
    for h in range(DSA_HEADS):
        qs_ref[0, h] = q[:, h * DSA_HEAD_DIM:(h + 1) * DSA_HEAD_DIM].astype(BF16)
    qi = _rope(iq_ref[0], two(c2), two(s2), pi_ref)
    for h in range(IDX_HEADS):
        qis_ref[0, h] = qi[:, h * IDX_DIM:(h + 1) * IDX_DIM].astype(BF16)

    kvr = _rope(kv_ref[0], jnp.where(lane < DSA_HEAD_DIM, c1, 1.0), s1, pk_ref)
    kr_ref[0] = kvr[:, 0:DSA_HEAD_DIM].astype(BF16)
    ones = jnp.ones((VT_ROWS - DSA_HEAD_DIM, kvr.shape[0]), BF16)
    vt_ref[0, 0] = jnp.concatenate([kvr.T[DSA_HEAD_DIM:2 * DSA_HEAD_DIM].astype(BF16), ones], axis=0)

    ikr = _rope(ikw_ref[0], jnp.where(lane < IDX_DIM, c2, 1.0), s2, pik_ref)
    kir_ref[0] = ikr[:, 0:IDX_DIM].astype(BF16)
    wt_ref[0] = ikr.T[IDX_DIM:IDX_DIM + IDX_HEADS] * ((IDX_HEADS ** -0.5) * (IDX_DIM ** -0.5))


def dsa_prep(proj3, tabs, *, ts):
    b, s, _ = proj3.shape
    qw = DSA_HEADS * DSA_HEAD_DIM
    iw = IDX_HEADS * IDX_DIM
    return pl.pallas_call(
        _dsa_prep_kernel,
        grid=(b, s // ts),
        in_specs=[
            pl.BlockSpec((1, ts, qw), lambda bi, i: (bi, i, P_DQ // qw)),
            pl.BlockSpec((1, ts, iw), lambda bi, i: (bi, i, P_IQ // iw)),
            pl.BlockSpec((1, ts, LANES), lambda bi, i: (bi, i, P_KV // LANES)),
            pl.BlockSpec((1, ts, LANES), lambda bi, i: (bi, i, P_IKW // LANES)),
            pl.BlockSpec((1, 6, ts, LANES), lambda bi, i: (bi, 0, i, 0)),
            pl.BlockSpec((qw, qw), lambda bi, i: (0, 0)),
            pl.BlockSpec((iw, iw), lambda bi, i: (0, 0)),
            pl.BlockSpec((LANES, LANES), lambda bi, i: (0, 0)),
            pl.BlockSpec((LANES, LANES), lambda bi, i: (0, 0)),
        ],
        out_specs=[
            pl.BlockSpec((1, DSA_HEADS, ts, DSA_HEAD_DIM), lambda bi, i: (bi, 0, i, 0)),
            pl.BlockSpec((1, IDX_HEADS, ts, IDX_DIM), lambda bi, i: (bi, 0, i, 0)),
            pl.BlockSpec((1, ts, DSA_HEAD_DIM), lambda bi, i: (bi, i, 0)),
            pl.BlockSpec((1, 1, VT_ROWS, ts), lambda bi, i: (bi, i, 0, 0)),
            pl.BlockSpec((1, ts, IDX_DIM), lambda bi, i: (bi, i, 0)),
            pl.BlockSpec((1, IDX_HEADS, ts), lambda bi, i: (bi, 0, i)),
        ],
        out_shape=[
            jax.ShapeDtypeStruct((b, DSA_HEADS, s, DSA_HEAD_DIM), BF16),
            jax.ShapeDtypeStruct((b, IDX_HEADS, s, IDX_DIM), BF16),
            jax.ShapeDtypeStruct((b, s, DSA_HEAD_DIM), BF16),
            jax.ShapeDtypeStruct((b, s // ts, VT_ROWS, ts), BF16),
            jax.ShapeDtypeStruct((b, s, IDX_DIM), BF16),
            jax.ShapeDtypeStruct((b, IDX_HEADS, s), F32),
        ],
        compiler_params=_cparams(("parallel", "arbitrary")),
        name="dsa_prep",
    )(proj3, proj3, proj3, proj3, tabs,
      _rope_partner_matrix(qw, DSA_HEAD_DIM, qw), _rope_partner_matrix(iw, IDX_DIM, iw),
      _rope_partner_matrix(LANES, DSA_HEAD_DIM, DSA_HEAD_DIM), _rope_partner_matrix(LANES, IDX_DIM, IDX_DIM))


def _dsa_kernel(q_ref, qi_ref, w_ref, k_ref, vt_ref, ki_ref, lts_ref, o_ref, sc_ref, lg_ref, *, qb, kt, topk):
    i = pl.program_id(1)
    q0 = i * qb
    nkt = (q0 + qb + kt - 1) // kt
    kf = float(topk)
    sg = 8 * SUBLANES
    wrow = w_ref[0]
    tq = q0 + lax.broadcasted_iota(I32, (1, qb), 1)

    def to_key(x):
        bits = pltpu.bitcast(x, I32)
        return bits ^ ((bits >> 31) & 0x7FFFFFFF)

    def scores(t, gmax):
        k0 = pl.multiple_of(t * kt, kt)
        sb = LANES
        gm = [gmax[j * sb:(j + 1) * sb] for j in range(topk // sb)]
        for r in range(kt // sb):
            kit = ki_ref[0, pl.ds(k0 + r * sb, sb), :]
            acc = jnp.zeros((sb, qb), F32)
            for h in range(IDX_HEADS):
                acc = acc + jnp.maximum(_dot_nt(kit, qi_ref[0, h]), 0.0) * wrow[h:h + 1]
            acc = jnp.where(acc == 0.0, 0.0, acc)
            causal = (k0 + r * sb + lax.broadcasted_iota(I32, (sb, qb), 0)) <= tq
            sc_ref[t, r * sb:(r + 1) * sb, :] = jnp.where(causal, to_key(acc), INT_MIN)
            j = ((r * sb) % topk) // sb
            gm[j] = jnp.maximum(gm[j], jnp.where(causal, acc, -jnp.inf))
        return jnp.concatenate(gm, axis=0)

    gmax = lax.fori_loop(0, nkt, scores, jnp.full((topk, qb), -jnp.inf, F32))

    def count(pred):
        def body(t, c):
            m = jnp.where(pred(sc_ref[t]), 1.0, 0.0)
            return c + jnp.sum(m.reshape(kt // sg, sg, qb), axis=0)
        c = lax.fori_loop(0, nkt, body, jnp.zeros((sg, qb), F32))
        return jnp.sum(c, axis=0, keepdims=True)

    need = (tq + 1) > topk
    lo0 = to_key(jnp.min(gmax, axis=0, keepdims=True))
    hi0 = to_key(jnp.max(gmax, axis=0, keepdims=True))
    done0 = jnp.where(jnp.logical_and(need, lo0 < hi0), 0.0, 1.0)

    def bs_cond(c):
        it, _, _, _, done, _ = c
        return jnp.logical_and(it < 34, jnp.sum(1.0 - done) > 0.0)

    def bs_step(lo, hi, thr, done, below, probe=None):
        mid = (lo >> 1) + (hi >> 1) + (((lo & 1) + (hi & 1) + 1) >> 1)
        if probe is not None:
            mid = jnp.where(jnp.logical_and(lo < probe, probe <= hi), probe, mid)
        cnt = count(lambda kk: kk >= mid)
        ge = cnt >= kf
        hit = cnt == kf
        near = cnt == kf - 1.0
        lo = jnp.where(ge, mid, lo)
        hi = jnp.where(ge, hi, mid - 1)
        fin = jnp.where(hit, 1.0, jnp.where(near, 1.0, jnp.where(lo == hi, 1.0, 0.0)))
        val = jnp.where(hit, mid, jnp.where(near, mid - 1, lo))
        live = done == 0.0
        thr = jnp.where(live, val, thr)
        below = jnp.where(live, jnp.where(near, 1.0, 0.0), below)
        return lo, hi, thr, jnp.maximum(done, fin), below

    def bs_body(c):
        it = c[0]
        return (it + 2,) + bs_step(*bs_step(*c[1:]))

    st = bs_step(*bs_step(lo0, hi0, lo0, done0, jnp.zeros((1, qb), F32), probe=1), probe=0)
    _, _, _, thr, _, below = lax.while_loop(bs_cond, bs_body, (jnp.int32(2),) + st)

    def below_max(t, c):
        kk = sc_ref[t]
        cand = jnp.where(kk <= thr, kk, NEG_INF_KEY)
        return jnp.maximum(c, jnp.max(cand.reshape(kt // sg, sg, qb), axis=0))

    bm = lax.fori_loop(0, nkt, below_max, jnp.full((sg, qb), NEG_INF_KEY, I32))
    bmax = to_key(jnp.max(pltpu.bitcast(bm ^ ((bm >> 31) & 0x7FFFFFFF), F32), axis=0, keepdims=True))
    thr = jnp.where(below > 0.0, bmax, thr)
    thr = jnp.where(need, thr, INT_MIN + 1)

    cge = count(lambda kk: kk >= thr)
    tie = jnp.logical_and(need, cge > kf)

    @pl.when(jnp.sum(jnp.where(tie, 1.0, 0.0)) > 0.0)
    def _():
        cgt = count(lambda kk: kk > thr)
        room = kf - cgt

        def body(t, seen):
            kk = sc_ref[t]
            eq = jnp.logical_and(kk == thr, tie)
            eqf = jnp.where(eq, 1.0, 0.0)
            before = _dot(lts_ref[...], eqf.astype(BF16)) + seen
            sc_ref[t] = jnp.where(jnp.logical_and(eq, before >= room), INT_MIN, kk)
            return seen + jnp.sum(eqf, axis=0, keepdims=True)

        lax.fori_loop(0, nkt, body, jnp.zeros((1, qb), F32))

    nh = DSA_HEADS
    qa = lg_ref.shape[3]
    dh = DSA_HEAD_DIM
    for part in range(qb // qa):
        cols = slice(part * qa, (part + 1) * qa)
        thr_p = thr[:, cols]

        def logits(t, c, cols=cols, thr_p=thr_p):
            k0 = pl.multiple_of(t * kt, kt)
            kt_ = k_ref[0, pl.ds(k0, kt), :]
            bias = jnp.where(sc_ref[t, :, cols] >= thr_p, 0.0, -jnp.inf)
            out = []
            for h in range(nh):
                lg = _dot_nt(kt_, q_ref[0, h, cols, :]) + bias
                lg_ref[t, h] = lg
                out.append(jnp.maximum(c[h], jnp.max(lg.reshape(kt // sg, sg, qa), axis=0)))
            return tuple(out)

        mparts = lax.fori_loop(0, nkt, logits, (jnp.full((sg, qa), -jnp.inf, F32),) * nh)
        ms = []
        for h in range(nh):
            m = jnp.max(mparts[h], axis=0, keepdims=True)
            ms.append(jnp.where(m == -jnp.inf, 0.0, m))

        def attend(t, c, ms=ms):
            vt = vt_ref[0, t]
            out = []
            for h in range(nh):
                p = jnp.exp2((lg_ref[t, h] - ms[h]).astype(BF16))
                out.append(c[h] + _dot(vt, p))
            return tuple(out)

        res = lax.fori_loop(0, nkt, attend, (jnp.zeros((vt_ref.shape[2], qa), F32),) * nh)
        for h in range(nh):
            ot = res[h][0:dh] / res[h][dh:dh + 1]
            o_ref[0, cols, h * dh:(h + 1) * dh] = ot.T.astype(o_ref.dtype)


def dsa_attend(qs, qis, wt, kr, vt4, kir, *, qb, kt, topk):
    b, nh, s, dh = qs.shape
    nkt = s // kt
    assert topk % LANES == 0 and kt % topk == 0 and s % kt == 0 and s % qb == 0
    ri = np.arange(kt)
    lts = jnp.asarray((ri[None, :] < ri[:, None]).astype(np.float32), BF16)
    return pl.pallas_call(
        functools.partial(_dsa_kernel, qb=qb, kt=kt, topk=topk),
        grid=(b, s // qb),
        in_specs=[
            pl.BlockSpec((1, nh, qb, dh), lambda bi, i: (bi, 0, i, 0)),
            pl.BlockSpec((1, IDX_HEADS, qb, IDX_DIM), lambda bi, i: (bi, 0, i, 0)),
            pl.BlockSpec((1, IDX_HEADS, qb), lambda bi, i: (bi, 0, i)),
            pl.BlockSpec((1, s, dh), lambda bi, i: (bi, 0, 0)),
            pl.BlockSpec((1, nkt, VT_ROWS, kt), lambda bi, i: (bi, 0, 0, 0)),
            pl.BlockSpec((1, s, IDX_DIM), lambda bi, i: (bi, 0, 0)),
            pl.BlockSpec((kt, kt), lambda bi, i: (0, 0)),
        ],
        out_specs=pl.BlockSpec((1, qb, nh * dh), lambda bi, i: (bi, i, 0)),
        out_shape=jax.ShapeDtypeStruct((b, s, nh * dh), BF16),
        scratch_shapes=[pltpu.VMEM((nkt, kt, qb), I32), pltpu.VMEM((nkt, nh, kt, min(qb, 2 * Q_BLOCK)), F32)],
        compiler_params=_cparams(("parallel", "arbitrary")),
        name="dsa_attend",
    )(qs, qis, wt, kr, vt4, kir, lts)


def _merge_kernel(x_ref, g_ref, ya_ref, yb_ref, yc_ref, yd_ref, wg_ref, wb_ref, wo_ref, o_ref):
    ys = (ya_ref, yb_ref, yc_ref, yd_ref)
    x = x_ref[...]
    d = x.shape[1]
    hn = _rms(x, g_ref[...]).astype(BF16)
    merged = jnp.zeros(o_ref.shape, F32)
    for i in range(N_BRANCH):
        gate = _sigmoid(_dot(hn, wg_ref[:, i * d:(i + 1) * d]))
        merged = merged + gate * _dot(ys[i][...], wb_ref[i])
    o_ref[...] = x + _dot(merged.astype(BF16), wo_ref[...])


def merge_out(x2, g, ya, yb, yc, yd, w_gates, w_branch, w_out, *, tm):
    t, d = x2.shape
    w_ = BRANCH_W
    yspec = pl.BlockSpec((tm, w_), lambda i: (i, 0))
    return pl.pallas_call(
        _merge_kernel,
        grid=(t // tm,),
        in_specs=[pl.BlockSpec((tm, d), lambda i: (i, 0)), pl.BlockSpec((1, d), lambda i: (0, 0)),
                  yspec, yspec, yspec, yspec,
                  pl.BlockSpec((d, N_BRANCH * d), lambda i: (0, 0)),
                  pl.BlockSpec((N_BRANCH, w_, d), lambda i: (0, 0, 0)),
                  pl.BlockSpec((d, d), lambda i: (0, 0))],
        out_specs=pl.BlockSpec((tm, d), lambda i: (i, 0)),
        out_shape=jax.ShapeDtypeStruct((t, d), F32),
        compiler_params=_cparams(("parallel",)),
        name="merge_out",
    )(x2, g, ya, yb, yc, yd, w_gates, w_branch, w_out)


def _cross_kernel(x_ref, g_ref, wq_ref, kv_ref, wo_ref, o_ref):
    x = x_ref[0]
    hn = _rms(x, g_ref[...]).astype(BF16)
    q = _dot(hn, wq_ref[...]) * (X_HEAD_DIM ** -0.5)
    kv = kv_ref[0]
    hd = X_HEADS * X_HEAD_DIM
    outs = []
    for h in range(X_HEADS):
        cs = slice(h * X_HEAD_DIM, (h + 1) * X_HEAD_DIM)
        lg = _dot_nt(q[:, cs].astype(BF16), kv[:, cs])
        p = jnp.exp(lg - jnp.max(lg, axis=-1, keepdims=True))
        p = p / jnp.sum(p, axis=-1, keepdims=True)
        outs.append(_dot(p.astype(BF16), kv[:, hd + h * X_HEAD_DIM:hd + (h + 1) * X_HEAD_DIM]))
    o = jnp.concatenate(outs, axis=1).astype(BF16)
    o_ref[0] = x + _dot(o, wo_ref[...])


def cross_attn(x3, g, w_q, kv, w_o, *, tm):
    b, s, d = x3.shape
    m = kv.shape[1]
    hd = X_HEADS * X_HEAD_DIM
    return pl.pallas_call(
        _cross_kernel,
        grid=(b, s // tm),
        in_specs=[pl.BlockSpec((1, tm, d), lambda bi, i: (bi, i, 0)),
                  pl.BlockSpec((1, d), lambda bi, i: (0, 0)),
                  pl.BlockSpec((d, hd), lambda bi, i: (0, 0)),
                  pl.BlockSpec((1, m, 2 * hd), lambda bi, i: (bi, 0, 0)),
                  pl.BlockSpec((hd, d), lambda bi, i: (0, 0))],
        out_specs=pl.BlockSpec((1, tm, d), lambda bi, i: (bi, i, 0)),
        out_shape=jax.ShapeDtypeStruct((b, s, d), F32),
        compiler_params=_cparams(("parallel", "arbitrary")),
        name="cross_attn",
    )(x3, g, w_q, kv, w_o)


def _route(hn, wr, br):
    lg = jnp.dot(hn, wr, precision=lax.Precision.HIGHEST, preferred_element_type=F32) + br
    lane = lax.broadcasted_iota(I32, lg.shape, 1)
    lanef = lane.astype(F32)
    ninf = -jnp.inf
    big = float(LANES)
    first = lambda mask: jnp.min(jnp.where(mask, lanef, big), axis=-1, keepdims=True)

    isg = jnp.logical_and(lane >= N_EXPERTS, lane < N_EXPERTS + N_GROUPS)
    gl = jnp.where(isg, lg, ninf)
    gmax = jnp.max(gl, axis=-1, keepdims=True)
    gsel = first(gl == gmax) - float(N_EXPERTS)
    pg = 1.0 / jnp.sum(jnp.exp(gl - gmax), axis=-1, keepdims=True)

    ise = jnp.floor(lanef * (1.0 / EXPERTS_PER_GROUP)) == gsel
    el = jnp.where(ise, lg, ninf)
    e1 = jnp.max(el, axis=-1, keepdims=True)
    i1 = first(el == e1)
    el2 = jnp.where(lanef == i1, ninf, el)
    e2 = jnp.max(el2, axis=-1, keepdims=True)
    i2 = first(el2 == e2)
    d = jnp.exp(e2 - e1)
    w1 = 1.0 / (1.0 + d)
    w2 = d / (1.0 + d)
    cmb = jnp.where(lanef == i1, pg * w1, jnp.where(lanef == i2, pg * w2, 0.0))
    a = jnp.minimum(i1, i2) - EXPERTS_PER_GROUP * gsel
    b = jnp.maximum(i1, i2) - EXPERTS_PER_GROUP * gsel
    rank = jnp.where(a == 0.0, b - 1.0, jnp.where(a == 1.0, jnp.where(b == 3.0, 3.0, 4.0), 5.0))
    return jnp.where(lane == N_EXPERTS, gsel * MOE_PAIRS + rank, cmb)


MOE_PAD = 2 * SUBLANES
MOE_SB = 256
MOE_PAIRS = 6
MOE_CLASSES = N_GROUPS * MOE_PAIRS
MOE_FIRST = (0, 0, 1, 2)
MOE_LAST = (2, 4, 5, 5)
MOE_CHUNK = (160, 256, 256, 208)
MOE_SEG = 64


def _moe_rmax(tm):
    return -(-(tm + N_GROUPS * MOE_PAD + max(MOE_CHUNK)) // LANES) * LANES


def _moe_class_matrices():
    c = np.arange(LANES)
    valid = c < MOE_CLASSES
    grp = c // MOE_PAIRS
    both = valid[:, None] & valid[None, :]
    before = both & (c[:, None] < c[None, :])
    same = both & (grp[:, None] == grp[None, :])
    lead = both & (c[:, None] % MOE_PAIRS == 0) & (grp[:, None] < grp[None, :])
    return jnp.asarray(np.stack([before, same, lead]).astype(np.float32))


def _moe_sort_kernel(x_ref, g_ref, wr_ref, br_ref, lts_ref, cm_ref, hs_ref, cs_ref, meta_ref, seg_ref, *, tm, rmax):
    hn32 = _rms(x_ref[...], g_ref[...])
    cmb = _route(hn32, wr_ref[...], br_ref[...])
    hn = hn32.astype(BF16)
    lanef = lax.broadcasted_iota(I32, (tm, LANES), 1).astype(F32)
    oh = jnp.where(lanef == cmb[:, N_EXPERTS:N_EXPERTS + 1], 1.0, 0.0)
    seen = jnp.zeros((1, LANES), F32)
    pres = []
    for sblk in range(tm // MOE_SB):
        ohs = oh[sblk * MOE_SB:(sblk + 1) * MOE_SB]
        pres.append(_dot(lts_ref[...], ohs.astype(BF16)) + seen)
        seen = seen + jnp.sum(ohs, axis=0, keepdims=True)
    pre = jnp.concatenate(pres, axis=0)
    cnt = jnp.broadcast_to(seen, (SUBLANES, LANES))
    exact = functools.partial(jnp.dot, precision=lax.Precision.HIGHEST, preferred_element_type=F32)
    tot = exact(cnt, cm_ref[1])
    padamt = jnp.ceil(tot * (1.0 / MOE_PAD)) * MOE_PAD - tot
    off = exact(cnt, cm_ref[0]) + exact(padamt, cm_ref[2])
    dest = jnp.sum(oh * (off[0:1] + pre), axis=-1, keepdims=True)
    destb = jnp.broadcast_to(dest, (tm, LANES))
    meta_ref[...] = destb
    row8 = lax.broadcasted_iota(I32, (SUBLANES, LANES), 0)
    seg_ref[0] = jnp.where(row8 == 0, off, jnp.where(row8 == 1, cnt, 0.0)).astype(I32)

    dest_row = destb.T[0:1]
    chi = cmb.astype(BF16)
    clo = (cmb - chi.astype(F32)).astype(BF16)

    def blk(rb, carry):
        r0 = pl.multiple_of(rb * LANES, LANES)
        rows = (r0 + lax.broadcasted_iota(I32, (LANES, tm), 0)).astype(F32)
        p = jnp.where(rows == dest_row, 1.0, 0.0).astype(BF16)
        hs_ref[0, pl.ds(r0, LANES), :] = _dot(p, hn).astype(BF16)
        cs_ref[0, pl.ds(r0, LANES), :] = _dot(p, chi) + _dot(p, clo)
        return carry

    lax.fori_loop(0, rmax // LANES, blk, 0)


def moe_sort(x2, g, wr, br, *, tm):
    t, d = x2.shape
    nt = t // tm
    rmax = _moe_rmax(tm)
    ri = np.arange(MOE_SB)
    lts = jnp.asarray((ri[None, :] < ri[:, None]).astype(np.float32), BF16)
    return pl.pallas_call(
        functools.partial(_moe_sort_kernel, tm=tm, rmax=rmax),
        grid=(nt,),
        in_specs=[pl.BlockSpec((tm, d), lambda i: (i, 0)), pl.BlockSpec((1, d), lambda i: (0, 0)),
                  pl.BlockSpec((d, LANES), lambda i: (0, 0)), pl.BlockSpec((1, LANES), lambda i: (0, 0)),
                  pl.BlockSpec((MOE_SB, MOE_SB), lambda i: (0, 0)),
                  pl.BlockSpec((3, LANES, LANES), lambda i: (0, 0, 0))],
        out_specs=[pl.BlockSpec((1, rmax, d), lambda i: (i, 0, 0)),
                   pl.BlockSpec((1, rmax, LANES), lambda i: (i, 0, 0)),
                   pl.BlockSpec((tm, LANES), lambda i: (i, 0)),
                   pl.BlockSpec((1, SUBLANES, LANES), lambda i: (i, 0, 0))],
        out_shape=[jax.ShapeDtypeStruct((nt, rmax, d), BF16), jax.ShapeDtypeStruct((nt, rmax, LANES), F32),
                   jax.ShapeDtypeStruct((t, LANES), F32), jax.ShapeDtypeStruct((nt, SUBLANES, LANES), I32)],
        compiler_params=_cparams(("parallel",)),
        name="moe_sort",
    )(x2, g, wr, br, lts, _moe_class_matrices())


def _moe_expert_kernel(seg_ref, hs_ref, cs_ref, wg_ref, wu_ref, wd_ref, y_ref, acc_ref, *, ns):
    i = pl.program_id(0)
    e = pl.program_id(1)

    @pl.when(e == 0)
    def _():
        acc_ref[...] = jnp.zeros_like(acc_ref)

    g = e // EXPERTS_PER_GROUP

    def run(local, ch):
        for s in range(ns):
            base = (i * ns + s) * MOE_SEG + g * MOE_PAIRS
            start = seg_ref[base + MOE_FIRST[local]]
            end = seg_ref[base + MOE_LAST[local]] + seg_ref[base + MOE_SEG // 2 + MOE_LAST[local]]
            start = (start // MOE_PAD) * MOE_PAD

            def chunk(j, carry, s=s, start=start):
                r0 = pl.multiple_of(start + j * ch, MOE_PAD)
                hsl = hs_ref[s, pl.ds(r0, ch), :]
                gg = _dot(hsl, wg_ref[0])
                uu = _dot(hsl, wu_ref[0])
                c = cs_ref[s, pl.ds(r0, ch), :]
                lane = lax.broadcasted_iota(I32, c.shape, 1)
                cc = jnp.sum(jnp.where(lane == e, c, 0.0), axis=-1, keepdims=True)
                he = (gg * _sigmoid(gg)) * uu * cc
                acc_ref[s, pl.ds(r0, ch), :] += _dot(he.astype(BF16), wd_ref[0])
                return carry

            lax.fori_loop(0, (end - start + ch - 1) // ch, chunk, 0)

    for local in range(EXPERTS_PER_GROUP):
        pl.when(e % EXPERTS_PER_GROUP == local)(functools.partial(run, local, MOE_CHUNK[local]))

    @pl.when(e == pl.num_programs(1) - 1)
    def _():
        y_ref[...] = acc_ref[...].astype(BF16)


def moe_experts(seg, hs, cs, wg, wu, wd, *, ns):
    nt, rmax, d = hs.shape
    ne, _, f = wg.shape
    return pl.pallas_call(
        functools.partial(_moe_expert_kernel, ns=ns),
        grid_spec=pltpu.PrefetchScalarGridSpec(
            num_scalar_prefetch=1,
            grid=(nt // ns, ne),
            in_specs=[pl.BlockSpec((ns, rmax, d), lambda i, e, sref: (i, 0, 0)),
                      pl.BlockSpec((ns, rmax, LANES), lambda i, e, sref: (i, 0, 0)),
                      pl.BlockSpec((1, d, f), lambda i, e, sref: (e, 0, 0)),
                      pl.BlockSpec((1, d, f), lambda i, e, sref: (e, 0, 0)),
                      pl.BlockSpec((1, f, d), lambda i, e, sref: (e, 0, 0))],
            out_specs=pl.BlockSpec((ns, rmax, d), lambda i, e, sref: (i, 0, 0)),
            scratch_shapes=[pltpu.VMEM((ns, rmax, d), F32)]),
        out_shape=jax.ShapeDtypeStruct((nt, rmax, d), BF16),
        compiler_params=_cparams(("parallel", "arbitrary")),
        name="moe_experts",
    )(seg, hs, cs, wg, wu, wd)


def _moe_combine_kernel(x_ref, meta_ref, y_ref, gf_ref, o_ref, *, rmax, final_norm):
    tb = x_ref.shape[0]
    lanef = lax.broadcasted_iota(I32, (tb, rmax), 1).astype(F32)
    pt = jnp.where(lanef == meta_ref[:, 0:1], 1.0, 0.0).astype(BF16)
    out = x_ref[...] + _dot(pt, y_ref[0])
    o_ref[...] = _rms(out, gf_ref[...]) if final_norm else out


def moe_combine(x2, meta, y, g_final, *, tm, tb, final_norm):
    t, d = x2.shape
    nt, rmax, _ = y.shape
    nb = tm // tb
    return pl.pallas_call(
        functools.partial(_moe_combine_kernel, rmax=rmax, final_norm=final_norm),
        grid=(nt, nb),
        in_specs=[pl.BlockSpec((tb, d), lambda i, j: (i * nb + j, 0)),
                  pl.BlockSpec((tb, LANES), lambda i, j: (i * nb + j, 0)),
                  pl.BlockSpec((1, rmax, d), lambda i, j: (i, 0, 0)),
                  pl.BlockSpec((1, d), lambda i, j: (0, 0))],
        out_specs=pl.BlockSpec((tb, d), lambda i, j: (i * nb + j, 0)),
        out_shape=jax.ShapeDtypeStruct((t, d), F32),
        compiler_params=_cparams(("parallel", "arbitrary")),
        name="moe_combine",
    )(x2, meta, y, g_final)


def _permute_w_in(w):
    d = w.shape[0]
    o = 0
    seg = {}
    for name, width in (("c", 768), ("gla", 1024), ("ga", GLA_RANK), ("s5u", 256), ("dq", 256), ("dk", 64),
                        ("dv", 64), ("iq", 256), ("ik", IDX_DIM), ("iw", IDX_HEADS), ("gates", 4096)):
        seg[name] = w[:, o:o + width]
        o += width
    z = lambda n: jnp.zeros((d, n), w.dtype)
    cols = [seg["c"], seg["s5u"], seg["gla"], seg["dq"], seg["iq"], seg["dk"], seg["dv"],
            seg["ik"], seg["iw"], z(LANES - IDX_DIM - IDX_HEADS), seg["ga"], z(LANES - GLA_RANK),
            z(P_TOTAL - P_GA - LANES)]
    out = jnp.concatenate(cols, axis=1).astype(BF16)
    assert out.shape[1] == P_TOTAL
    return out, seg["gates"].astype(BF16)


def _s5_matrices(bb_re, bb_im, c_re, c_im):
    g, p, c = S5_GROUPS, S5_STATE, S5_GROUP
    n = g * p
    rows_g = jnp.arange(g * c) // c
    cols_g = jnp.arange(n) // p
    mask = (rows_g[:, None] == cols_g[None, :]).astype(F32)
    bm = jnp.concatenate([jnp.tile(bb_re, (g, 1)) * mask, jnp.tile(bb_im, (g, 1)) * mask], axis=1)
    ct = lambda a: jnp.tile(jnp.transpose(a, (0, 2, 1)).reshape(n, c), (1, g)) * mask.T
    cm = jnp.concatenate([ct(c_re), -ct(c_im)], axis=0)
    return bm.astype(BF16), cm.astype(BF16)


def _pick(s, pref):
    for c in pref:
        if s % c == 0:
            return c
    return s


def kernel(x, mem, positions, norm_mix, w_in, conv_w, conv_b, gla_a_up, gla_a_b, gla_norm, s5_lambda_re,
           s5_lambda_im, s5_log_dt, s5_b_re, s5_b_im, s5_c_re, s5_c_im, s5_d, s5_w_glu, s5_b_glu, w_branch,
           w_out, norm_cross, w_cq, w_ckv, w_co, norm_ffn, w_route_group, b_route_group, w_route_expert,
           b_route_expert, w_e_gate, w_e_up, w_e_down, norm_mem, norm_final):
    b, s, d = x.shape
    t = b * s
    m = mem.shape[1]
    depth = w_in.shape[0]
    topk = min(DSA_TOPK, s // 4)
    ts = _pick(s, (512, 256, 128))
    tm = _pick(t, (1024, 512, 256, 128))
    row = lambda a: a.reshape(1, -1)

    tabs = rope_tables(positions, ts=ts)
    x2 = x.reshape(t, d)
    mem2 = mem.reshape(b * m, d)
    for l in range(depth):
        w_mix, w_gates = _permute_w_in(w_in[l])
        proj = norm_matmul(x2, row(norm_mix[l]), w_mix, tm=_pick(t, (2048, 1024, 512)), tn=1024)
        proj3 = proj.reshape(b, s, P_TOTAL)
        ya = conv_branch(proj3, conv_w[l], row(conv_b[l]), ts=ts)
        a_up_p = jnp.concatenate(
            [gla_a_up[l], jnp.zeros((LANES - GLA_RANK, gla_a_up.shape[2]), F32)], axis=0).astype(BF16)
        yb = gla_branch(proj3, a_up_p, row(gla_a_b[l]), row(gla_norm[l]), tt=ts)
        bb_re, bb_im, tab = s5_params(s5_lambda_re[l], s5_lambda_im[l], s5_log_dt[l], s5_b_re[l], s5_b_im[l])
        bm, cm = _s5_matrices(bb_re, bb_im, s5_c_re[l], s5_c_im[l])
        yc = s5_branch(proj3, bm, tab, cm, row(s5_d[l]), s5_w_glu[l].astype(BF16), row(s5_b_glu[l]), tt=ts)
        qs, qis, kr, vt4, kir, wt = dsa_prep(proj3, tabs, ts=ts)
        yd = dsa_attend(qs, qis, wt, kr, vt4, kir, qb=_pick(s, (2 * Q_BLOCK, Q_BLOCK)), kt=ts, topk=topk)
        w2 = lambda a: a.reshape(t, a.shape[-1])
        x2 = merge_out(x2, row(norm_mix[l]), w2(ya), w2(yb), w2(yc), w2(yd), w_gates, w_branch[l].astype(BF16),
                       w_out[l].astype(BF16), tm=min(tm, 512))
        kv = norm_matmul(mem2, row(norm_mem), w_ckv[l].astype(BF16), tm=_pick(b * m, (1024, 512, 256)), tn=1024)
        x2 = cross_attn(x2.reshape(b, s, d), row(norm_cross[l]), w_cq[l].astype(BF16),
                        kv.reshape(b, m, -1), w_co[l].astype(BF16), tm=ts).reshape(t, d)
        pad = jnp.zeros((d, LANES - N_EXPERTS - N_GROUPS), F32)
        wr = jnp.concatenate([w_route_expert[l], w_route_group[l], pad], axis=1)
        br = jnp.concatenate([b_route_expert[l], b_route_group[l], pad[0]], axis=0).reshape(1, LANES)
        tmoe = _pick(t, (1024, 512))
        hs, cs, meta, seg = moe_sort(x2, row(norm_ffn[l]), wr, br, tm=tmoe)
        half = MOE_SEG // 2
        seg1 = jnp.concatenate([seg[:, 0, :half], seg[:, 1, :half]], axis=1).reshape(-1)
        y = moe_experts(seg1, hs, cs, w_e_gate[l].astype(BF16), w_e_up[l].astype(BF16), w_e_down[l].astype(BF16),
                        ns=2 if (t // tmoe) % 2 == 0 else 1)
        x2 = moe_combine(x2, meta, y, row(norm_final), tm=tmoe, tb=min(tmoe, 512), final_norm=(l == depth - 1))
    return x2.reshape(b, s, d)
```
